```python
import math
import jax
import jax.numpy as jnp
from jax import lax
import numpy as np

D_MODEL = 2048
BATCH = 16
SEQ = 256
DEPTH = 2
DEC_BATCH = 2
DEC_SEQ = 2048
PAST_LEN = 512

GRID_W = 64
SSD_HEADS = 12
SSD_P = 64
D_SSD = SSD_HEADS * SSD_P
SSD_N = 128
SSD_GROUPS = 2
SSD_CONV_K = 5
SSD_CONV_CH = D_SSD + 2 * SSD_GROUPS * SSD_N
S5_CH = 16
S5_GROUPS = 32
D_S5 = S5_CH * S5_GROUPS
S5_STATE = 64
RET_HEADS = 6
RET_DK = 128
RET_DV = 128
D_RET = RET_HEADS * RET_DV
D_MIX = D_SSD + D_S5 + D_RET
IN_COLS = D_SSD + SSD_CONV_CH + 2 * SSD_HEADS + D_S5 + 4 * D_RET
CHUNK = 128
ROPE_BASE = 10000.0
PEER_HEADS = 8
PEER_DQ = 256
PEER_NKEYS = 128
PEER_EXPERTS = PEER_NKEYS * PEER_NKEYS
PEER_TOPK = 16
PEER_BLOCK = 128
N_MOD = 6
EPS = 1e-6

kernel_name = 'hybrid_ssd_s5_retention_peer_diffusion_step'


def rmsnorm(x, g):
    xf = x.astype(jnp.float32)
    y = xf * lax.rsqrt(jnp.mean(xf * xf, axis=-1, keepdims=True) + EPS)
    return (y * g.astype(jnp.float32)).astype(x.dtype)


def flip_seq(t):
    return jnp.flip(t, axis=1)


def chunked_scan(q, k, v, log_a, s0):
    f32 = jnp.float32
    b, l, h, n = q.shape
    p = v.shape[-1]
    nc = l // CHUNK
    q = q.astype(f32).reshape(b, nc, CHUNK, h, n)
    k = k.astype(f32).reshape(b, nc, CHUNK, h, n)
    v = v.astype(f32).reshape(b, nc, CHUNK, h, p)
    a = jnp.cumsum(log_a.astype(f32).reshape(b, nc, CHUNK, h), axis=2)
    a_h = jnp.moveaxis(a, 3, 2)
    seg = a_h[..., :, None] - a_h[..., None, :]
    lower = jnp.tril(jnp.ones((CHUNK, CHUNK), dtype=bool))
    decay = jnp.exp(jnp.where(lower, seg, -jnp.inf))
    scores = jnp.einsum('bzihn,bzjhn->bzhij', q, k) * decay
    y_diag = jnp.einsum('bzhij,bzjhp->bzihp', scores, v)
    w_end = jnp.exp(a[:, :, -1:, :] - a)
    chunk_states = jnp.einsum('bzjhn,bzjh,bzjhp->bzhpn', k, w_end, v)
    chunk_decay = jnp.exp(a[:, :, -1, :])

    def step(s, inp):
        cs, cd = inp
        return s * cd[:, :, None, None] + cs, s

    s_final, s_in = lax.scan(step, s0.astype(f32),
                             (jnp.moveaxis(chunk_states, 1, 0), jnp.moveaxis(chunk_decay, 1, 0)))
    s_in = jnp.moveaxis(s_in, 0, 1)
    y_off = jnp.einsum('bzihn,bzhpn->bzihp', q, s_in) * jnp.exp(a)[..., None]
    return (y_diag + y_off).reshape(b, l, h, p), s_final


def bidirectional_scan(q, k, v_f, v_b, log_a_f, log_a_b, s0_f, s0_b):
    y_f, s_f = chunked_scan(q, k, v_f, log_a_f, s0_f)
    y_b, s_b = chunked_scan(flip_seq(q), flip_seq(k), flip_seq(v_b), flip_seq(log_a_b), s0_b)
    return y_f + flip_seq(y_b), jnp.stack([s_f, s_b], axis=1)


def grid_positions(n_tokens):
    rows = n_tokens // GRID_W
    t = jnp.arange(rows * GRID_W)
    return (t // GRID_W).astype(jnp.float32), (t % GRID_W).astype(jnp.float32)


def rope_1d(x, pos):
    half = x.shape[-1] // 2
    freqs = ROPE_BASE ** (-jnp.arange(half, dtype=jnp.float32) / half)
    ang = pos[:, None] * freqs[None, :]
    cos = jnp.cos(ang)[:, None, :]
    sin = jnp.sin(ang)[:, None, :]
    x1 = x[..., :half].astype(jnp.float32)
    x2 = x[..., half:].astype(jnp.float32)
    return jnp.concatenate([x1 * cos - x2 * sin, x1 * sin + x2 * cos], axis=-1).astype(x.dtype)


def rope_2d(x, row, col):
    half = x.shape[-1] // 2
    return jnp.concatenate([rope_1d(x[..., :half], row), rope_1d(x[..., half:], col)], axis=-1)


def depthwise_conv(x, w, bias):
    y = lax.conv_general_dilated(x, w[:, None, :], window_strides=(1,),
                                 padding=[(SSD_CONV_K // 2, SSD_CONV_K // 2)],
                                 dimension_numbers=('NWC', 'WIO', 'NWC'),
                                 feature_group_count=x.shape[-1])
    return y + bias


def ssd_mixer(z, xbc, dt_raw, conv_w, conv_b, dt_bias, a_log, d_skip, norm_g, s0):
    f32 = jnp.float32
    b, l, _ = z.shape
    xbc = jax.nn.silu(depthwise_conv(xbc, conv_w, conv_b)).astype(f32)
    gn = SSD_GROUPS * SSD_N
    rep = SSD_HEADS // SSD_GROUPS
    x = xbc[..., :D_SSD].reshape(b, l, SSD_HEADS, SSD_P)
    bm = jnp.repeat(xbc[..., D_SSD:D_SSD + gn].reshape(b, l, SSD_GROUPS, SSD_N), rep, axis=2)
    cm = jnp.repeat(xbc[..., D_SSD + gn:].reshape(b, l, SSD_GROUPS, SSD_N), rep, axis=2)
    dt = jax.nn.softplus(dt_raw.astype(f32).reshape(b, l, 2, SSD_HEADS) + dt_bias.astype(f32))
    log_a = -dt * jnp.exp(a_log.astype(f32))
    y, s_new = bidirectional_scan(cm, bm, x * dt[:, :, 0, :, None], x * dt[:, :, 1, :, None],
                                  log_a[:, :, 0], log_a[:, :, 1], s0[:, 0], s0[:, 1])
    y = y + x * d_skip.astype(f32)[:, None]
    y = y.reshape(b, l, D_SSD) * jax.nn.silu(z.astype(f32))
    return rmsnorm(y, norm_g), s_new


def complex_affine_combine(e1, e2):
    a1r, a1i, b1r, b1i = e1
    a2r, a2i, b2r, b2i = e2
    return (a2r * a1r - a2i * a1i,
            a2r * a1i + a2i * a1r,
            a2r * b1r - a2i * b1i + b2r,
            a2r * b1i + a2i * b1r + b2i)


def s5_direction(u, lam_re, lam_im, log_dt, b_re, b_im, c_re, c_im, x0_re, x0_im):
    f32 = jnp.float32
    lam_re = lam_re.astype(f32)
    lam_im = lam_im.astype(f32)
    b_re = b_re.astype(f32)
    b_im = b_im.astype(f32)
    dt = jnp.exp(log_dt.astype(f32))[:, None]
    mag = jnp.exp(lam_re * dt)
    a_re = mag * jnp.cos(lam_im * dt)
    a_im = mag * jnp.sin(lam_im * dt)
    den = lam_re * lam_re + lam_im * lam_im
    num_re = a_re - 1.0
    k_re = ((num_re * lam_re + a_im * lam_im) / den)[..., None]
    k_im = ((a_im * lam_re - num_re * lam_im) / den)[..., None]
    bb_re = k_re * b_re - k_im * b_im
    bb_im = k_re * b_im + k_im * b_re
    bu_re = jnp.einsum('blgc,gpc->blgp', u, bb_re)
    bu_im = jnp.einsum('blgc,gpc->blgp', u, bb_im)
    x0_re = x0_re.astype(f32)
    x0_im = x0_im.astype(f32)
    bu_re = bu_re.at[:, 0].add(a_re * x0_re - a_im * x0_im)
    bu_im = bu_im.at[:, 0].add(a_re * x0_im + a_im * x0_re)
    ar = jnp.broadcast_to(a_re, bu_re.shape)
    ai = jnp.broadcast_to(a_im, bu_im.shape)
    _, _, x_re, x_im = lax.associative_scan(complex_affine_combine, (ar, ai, bu_re, bu_im), axis=1)
    y = (jnp.einsum('blgp,gcp->blgc', x_re, c_re.astype(f32))
         - jnp.einsum('blgp,gcp->blgc', x_im, c_im.astype(f32)))
    return y, x_re[:, -1], x_im[:, -1]


def s5_mixer(u, lam_re, lam_im, log_dt, b_re, b_im, c_re, c_im, d_skip, glu_w, glu_b, s0_re, s0_im):
    f32 = jnp.float32
    b, l, _ = u.shape
    uf = u.astype(f32)
    ug = uf.reshape(b, l, S5_GROUPS, S5_CH)
    y_f, f_re, f_im = s5_direction(ug, lam_re[0], lam_im[0], log_dt[0], b_re[0], b_im[0],
                                   c_re[0], c_im[0], s0_re[:, 0], s0_im[:, 0])
    y_b, r_re, r_im = s5_direction(flip_seq(ug), lam_re[1], lam_im[1], log_dt[1], b_re[1], b_im[1],
                                   c_re[1], c_im[1], s0_re[:, 1], s0_im[:, 1])
    y = (y_f + flip_seq(y_b)).reshape(b, l, D_S5) + uf * d_skip.astype(f32)
    y = jax.nn.gelu(y)
    y = y * jax.nn.sigmoid(y @ glu_w.astype(f32) + glu_b.astype(f32))
    return y, jnp.stack([f_re, r_re], axis=1), jnp.stack([f_im, r_im], axis=1)


def retention_mixer(q, k, v, g, gn_g, s0, pos):
    f32 = jnp.float32
    b, l, _ = q.shape
    q = q.reshape(b, l, RET_HEADS, RET_DK)
    k = k.reshape(b, l, RET_HEADS, RET_DK)
    v = v.reshape(b, l, RET_HEADS, RET_DV)
    if pos is not None:
        q = rope_2d(q, pos[0], pos[1])
        k = rope_2d(k, pos[0], pos[1])
    k = k.astype(f32) * (RET_DK ** -0.5)
    heads = jnp.arange(RET_HEADS, dtype=f32)
    log_gamma_f = jnp.log1p(-jnp.exp2(-5.0 - heads))
    log_gamma_b = jnp.log1p(-jnp.exp2(-5.5 - heads))
    y, s_new = bidirectional_scan(q, k, v, v,
                                  jnp.broadcast_to(log_gamma_f, (b, l, RET_HEADS)),
                                  jnp.broadcast_to(log_gamma_b, (b, l, RET_HEADS)),
                                  s0[:, 0], s0[:, 1])
    yc = y - jnp.mean(y, axis=-1, keepdims=True)
    y = yc * lax.rsqrt(jnp.mean(yc * yc, axis=-1, keepdims=True) + EPS)
    y = y.reshape(b, l, D_RET) * gn_g.astype(f32) * jax.nn.silu(g.astype(f32))
    return y, s_new


def token_mixers(h, p, s0, pos):
    s0_ssd, s0_re, s0_im, s0_ret = s0
    sizes = (D_SSD, SSD_CONV_CH, 2 * SSD_HEADS, D_S5, D_RET, D_RET, D_RET, D_RET)
    cuts = [int(c) for c in np.cumsum(sizes)[:-1]]
    z, xbc, dt_raw, u, rq, rk, rv, rg = jnp.split(h @ p['w_in'], cuts, axis=-1)
    y_ssd, ns_ssd = ssd_mixer(z, xbc, dt_raw, p['ssd_conv_w'], p['ssd_conv_b'], p['ssd_dt_bias'],
                              p['ssd_a_log'], p['ssd_d'], p['ssd_norm_g'], s0_ssd)
    y_s5, ns_re, ns_im = s5_mixer(u, p['s5_lambda_re'], p['s5_lambda_im'], p['s5_log_dt'],
                                  p['s5_b_re'], p['s5_b_im'], p['s5_c_re'], p['s5_c_im'],
                                  p['s5_d'], p['s5_glu_w'], p['s5_glu_b'], s0_re, s0_im)
    y_ret, ns_ret = retention_mixer(rq, rk, rv, rg, p['ret_gn_g'], s0_ret, pos)
    y = jnp.concatenate([y_ssd, y_s5, y_ret], axis=-1).astype(h.dtype) @ p['w_out']
    return y, (ns_ssd, ns_re, ns_im, ns_ret)


def peer_ffn(h, wq, keys, u_tab, v_tab):
    b, l, d = h.shape
    n_tok = b * l
    t = h.reshape(n_tok, d)
    q = (t @ wq).reshape(n_tok, PEER_HEADS, PEER_DQ)
    half = PEER_DQ // 2
    s1 = jnp.einsum('thd,hkd->thk', q[..., :half], keys[:, 0]).astype(jnp.float32)
    s2 = jnp.einsum('thd,hkd->thk', q[..., half:], keys[:, 1]).astype(jnp.float32)
    v1, i1 = lax.top_k(s1, PEER_TOPK)
    v2, i2 = lax.top_k(s2, PEER_TOPK)
    n_cand = PEER_TOPK * PEER_TOPK
    cand_s = (v1[..., :, None] + v2[..., None, :]).reshape(n_tok, PEER_HEADS, n_cand)
    cand_i = (i1[..., :, None] * PEER_NKEYS + i2[..., None, :]).reshape(n_tok, PEER_HEADS, n_cand)
    top_s, top_j = lax.top_k(cand_s, PEER_TOPK)
    idx = jnp.take_along_axis(cand_i, top_j, axis=-1)
    gate = jax.nn.softmax(top_s, axis=-1).astype(h.dtype)
    n_blk = n_tok // PEER_BLOCK
    k_all = PEER_HEADS * PEER_TOPK

    def expert_block(args):
        xb, ib, gb = args
        act = jax.nn.gelu(jnp.einsum('tkd,td->tk', jnp.take(u_tab, ib, axis=0), xb))
        return jnp.einsum('tk,tkd->td', gb * act, jnp.take(v_tab, ib, axis=0))

    out = lax.map(expert_block, (t.reshape(n_blk, PEER_BLOCK, d),
                                 idx.reshape(n_blk, PEER_BLOCK, k_all),
                                 gate.reshape(n_blk, PEER_BLOCK, k_all)))
    return out.reshape(b, l, d)


def adaln(cond, w, bias):
    m = jax.nn.silu(cond) @ w + bias
    return m.reshape(m.shape[0], 1, N_MOD, D_MODEL)


def trunk_layer(x, cond, p, s0, pos):
    mod = adaln(cond, p['ada_w'], p['ada_b'])
    sh1, sc1, g1, sh2, sc2, g2 = [mod[:, :, j] for j in range(N_MOD)]
    h = rmsnorm(x, p['norm1_g']) * (1.0 + sc1) + sh1
    y, s_new = token_mixers(h, p, s0, pos)
    x = x + g1 * y
    h = rmsnorm(x, p['norm2_g']) * (1.0 + sc2) + sh2
    x = x + g2 * peer_ffn(h, p['peer_wq'], p['peer_keys'], p['peer_u'], p['peer_v'])
    return x, s_new


def setup_inputs(seed: int = 0) -> dict:
    f32 = jnp.float32
    keys = jax.random.split(jax.random.key(seed), 48)
    counter = [0]

    def nk():
        kk = keys[counter[0]]
        counter[0] += 1
        return kk

    def nrm(shape, scale):
        return jax.random.normal(nk(), shape, f32) * scale

    def gain(shape):
        return 1.0 + nrm(shape, 0.02)

    def unif(shape, lo, hi):
        return jax.random.uniform(nk(), shape, f32, lo, hi)

    x_prompt = nrm((BATCH, SEQ, D_MODEL), 1.0)
    x_sample = nrm((DEC_BATCH, DEC_SEQ, D_MODEL), 1.0)
    c = nrm((DEC_BATCH, D_MODEL), 1.0)
    state_ssd = nrm((DEC_BATCH, DEPTH, 2, SSD_HEADS, SSD_P, SSD_N), 0.3)
    state_s5_re = nrm((DEC_BATCH, DEPTH, 2, S5_GROUPS, S5_STATE), 0.3)
    state_s5_im = nrm((DEC_BATCH, DEPTH, 2, S5_GROUPS, S5_STATE), 0.3)
    state_ret = nrm((DEC_BATCH, DEPTH, 2, RET_HEADS, RET_DV, RET_DK), 1.0)
    c_ctx = nrm((D_MODEL,), 1.0)
    ada_w = nrm((DEPTH, D_MODEL, N_MOD * D_MODEL), 0.5 * D_MODEL ** -0.5)
    ada_b = nrm((DEPTH, N_MOD * D_MODEL), 0.02)
    norm1_g = gain((DEPTH, D_MODEL))
    norm2_g = gain((DEPTH, D_MODEL))
    w_in = nrm((DEPTH, D_MODEL, IN_COLS), D_MODEL ** -0.5)
    w_out = nrm((DEPTH, D_MIX, D_MODEL), D_MIX ** -0.5)
    ssd_conv_w = nrm((DEPTH, SSD_CONV_K, SSD_CONV_CH), SSD_CONV_K ** -0.5)
    ssd_conv_b = nrm((DEPTH, SSD_CONV_CH), 0.02)
    dt0 = jnp.exp(unif((DEPTH, 2, SSD_HEADS), math.log(1e-3), math.log(1e-1)))
    ssd_dt_bias = dt0 + jnp.log(-jnp.expm1(-dt0))
    ssd_a_log = jnp.log(unif((DEPTH, 2, SSD_HEADS), 1.0, 16.0))
    ssd_d = gain((DEPTH, SSD_HEADS))
    ssd_norm_g = gain((DEPTH, D_SSD))
    s5_lambda_re = -0.5 + nrm((DEPTH, 2, S5_GROUPS, S5_STATE), 0.01)
    s5_lambda_im = jnp.pi * jnp.arange(S5_STATE, dtype=f32) + nrm((DEPTH, 2, S5_GROUPS, S5_STATE), 0.01)
    s5_log_dt = unif((DEPTH, 2, S5_GROUPS), math.log(1e-3), math.log(1e-1))
    s5_b_re = nrm((DEPTH, 2, S5_GROUPS, S5_STATE, S5_CH), (2 * S5_CH) ** -0.5)
    s5_b_im = nrm((DEPTH, 2, S5_GROUPS, S5_STATE, S5_CH), (2 * S5_CH) ** -0.5)
    s5_c_re = nrm((DEPTH, 2, S5_GROUPS, S5_CH, S5_STATE), (2 * S5_STATE) ** -0.5)
    s5_c_im = nrm((DEPTH, 2, S5_GROUPS, S5_CH, S5_STATE), (2 * S5_STATE) ** -0.5)
    s5_d = nrm((DEPTH, D_S5), 1.0)
    s5_glu_w = nrm((DEPTH, D_S5, D_S5), D_S5 ** -0.5)
    s5_glu_b = nrm((DEPTH, D_S5), 0.02)
    ret_gn_g = gain((DEPTH, D_RET))
    peer_wq = nrm((DEPTH, D_MODEL, PEER_HEADS * PEER_DQ), D_MODEL ** -0.5)
    peer_keys = nrm((DEPTH, PEER_HEADS, 2, PEER_NKEYS, PEER_DQ // 2), (PEER_DQ // 2) ** -0.5)
    peer_u = nrm((DEPTH, PEER_EXPERTS, D_MODEL), D_MODEL ** -0.5)
    peer_v = nrm((DEPTH, PEER_EXPERTS, D_MODEL), 1.0)
    final_norm_g = gain((D_MODEL,))
    return {'x_prompt': x_prompt, 'x_sample': x_sample, 'c': c,
            'state_ssd': state_ssd, 'state_s5_re': state_s5_re, 'state_s5_im': state_s5_im,
            'state_ret': state_ret, 'c_ctx': c_ctx, 'ada_w': ada_w, 'ada_b': ada_b,
            'norm1_g': norm1_g, 'norm2_g': norm2_g, 'w_in': w_in, 'w_out': w_out,
            'ssd_conv_w': ssd_conv_w, 'ssd_conv_b': ssd_conv_b, 'ssd_dt_bias': ssd_dt_bias,
            'ssd_a_log': ssd_a_log, 'ssd_d': ssd_d, 'ssd_norm_g': ssd_norm_g,
            's5_lambda_re': s5_lambda_re, 's5_lambda_im': s5_lambda_im, 's5_log_dt': s5_log_dt,
            's5_b_re': s5_b_re, 's5_b_im': s5_b_im, 's5_c_re': s5_c_re, 's5_c_im': s5_c_im,
            's5_d': s5_d, 's5_glu_w': s5_glu_w, 's5_glu_b': s5_glu_b, 'ret_gn_g': ret_gn_g,
            'peer_wq': peer_wq, 'peer_keys': peer_keys, 'peer_u': peer_u, 'peer_v': peer_v,
            'final_norm_g': final_norm_g}


def reference(x_prompt, x_sample, c, state_ssd, state_s5_re, state_s5_im, state_ret, c_ctx,
              ada_w, ada_b, norm1_g, norm2_g, w_in, w_out, ssd_conv_w, ssd_conv_b, ssd_dt_bias,
              ssd_a_log, ssd_d, ssd_norm_g, s5_lambda_re, s5_lambda_im, s5_log_dt, s5_b_re, s5_b_im,
              s5_c_re, s5_c_im, s5_d, s5_glu_w, s5_glu_b, ret_gn_g, peer_wq, peer_keys, peer_u,
              peer_v, final_norm_g):
    f32 = jnp.float32

    def layer_params(i):
        return {'ada_w': ada_w[i], 'ada_b': ada_b[i], 'norm1_g': norm1_g[i], 'norm2_g': norm2_g[i],
                'w_in': w_in[i], 'w_out': w_out[i], 'ssd_conv_w': ssd_conv_w[i],
                'ssd_conv_b': ssd_conv_b[i], 'ssd_dt_bias': ssd_dt_bias[i], 'ssd_a_log': ssd_a_log[i],
                'ssd_d': ssd_d[i], 'ssd_norm_g': ssd_norm_g[i], 's5_lambda_re': s5_lambda_re[i],
                's5_lambda_im': s5_lambda_im[i], 's5_log_dt': s5_log_dt[i], 's5_b_re': s5_b_re[i],
                's5_b_im': s5_b_im[i], 's5_c_re': s5_c_re[i], 's5_c_im': s5_c_im[i], 's5_d': s5_d[i],
                's5_glu_w': s5_glu_w[i], 's5_glu_b': s5_glu_b[i], 'ret_gn_g': ret_gn_g[i],
                'peer_wq': peer_wq[i], 'peer_keys': peer_keys[i], 'peer_u': peer_u[i],
                'peer_v': peer_v[i]}

    n_ctx = x_prompt.shape[0]
    zero_state = (jnp.zeros((n_ctx, 2, SSD_HEADS, SSD_P, SSD_N), f32),
                  jnp.zeros((n_ctx, 2, S5_GROUPS, S5_STATE), f32),
                  jnp.zeros((n_ctx, 2, S5_GROUPS, S5_STATE), f32),
                  jnp.zeros((n_ctx, 2, RET_HEADS, RET_DV, RET_DK), f32))
    x = x_prompt
    ctx_states = []
    for i in range(DEPTH):
        x, s_new = trunk_layer(x, c_ctx[None, :], layer_params(i), zero_state, None)
        ctx_states.append(s_new)
    y_prompt = rmsnorm(x, final_norm_g)

    pos = grid_positions(x_sample.shape[1])
    x = x_sample
    for i in range(DEPTH):
        cached = (state_ssd[:, i], state_s5_re[:, i], state_s5_im[:, i], state_ret[:, i])
        x, _ = trunk_layer(x, c, layer_params(i), cached, pos)
    y_sample = rmsnorm(x, final_norm_g)

    new_state_ssd = jnp.stack([s[0] for s in ctx_states], axis=1)
    new_state_s5_re = jnp.stack([s[1] for s in ctx_states], axis=1)
    new_state_s5_im = jnp.stack([s[2] for s in ctx_states], axis=1)
    new_state_ret = jnp.stack([s[3] for s in ctx_states], axis=1)
    return (y_prompt, y_sample, new_state_ssd, new_state_s5_re, new_state_s5_im, new_state_ret)
```

```python
import functools
import math

import jax
import jax.numpy as jnp
import numpy as np
from jax import lax
from jax.experimental import pallas as pl
from jax.experimental.pallas import tpu as pltpu

F32 = jnp.float32
BF16 = jnp.bfloat16

D_MODEL = 2048
N_CTX_SEQ = 16
CTX_LEN = 256
N_DEC_SEQ = 2
DEC_LEN = 2048
N_CTX_TOK = N_CTX_SEQ * CTX_LEN
N_TOK = N_CTX_TOK + N_DEC_SEQ * DEC_LEN
DEPTH = 2
GRID_W = 64
CHUNK = 128

SSD_HEADS = 12
SSD_P = 64
D_SSD = SSD_HEADS * SSD_P
SSD_N = 128
SSD_GROUPS = 2
SSD_HPG = SSD_HEADS // SSD_GROUPS
SSD_CONV_K = 5
SSD_CONV_CH = D_SSD + 2 * SSD_GROUPS * SSD_N
S5_CH = 16
S5_GROUPS = 32
D_S5 = S5_CH * S5_GROUPS
S5_STATE = 64
S5_LANES = S5_GROUPS * S5_STATE
RET_HEADS = 6
RET_DK = 128
RET_DV = 128
D_RET = RET_HEADS * RET_DV
ROPE_BASE = 10000.0
PEER_HEADS = 8
PEER_DQ = 256
PEER_NKEYS = 128
PEER_EXPERTS = PEER_NKEYS * PEER_NKEYS
PEER_TOPK = 16
N_MOD = 6
N_COND = 8
EPS = 1e-6

COL_Z, COL_Q, COL_K, COL_V, COL_G = 0, 768, 1536, 2304, 3072
COL_XBC = 3840
COL_U = 5120
COL_DT = 5632
PROJ_W = 6144

VMEM_LIMIT = 56 * 1024 * 1024

NEG_INF = float("-inf")


def _cparams(sem):
    return pltpu.CompilerParams(dimension_semantics=sem, vmem_limit_bytes=VMEM_LIMIT)


def _split3(a):
    hi = a.astype(BF16)
    r1 = a - hi.astype(F32)
    mid = r1.astype(BF16)
    lo = (r1 - mid.astype(F32)).astype(BF16)
    return hi, mid, lo


def _dot(a, b):
    return jnp.dot(a, b, preferred_element_type=F32)


def _dot_split_lhs(a, b_exact):
    hi, mid, lo = _split3(a)
    return _dot(hi, b_exact) + _dot(mid, b_exact) + _dot(lo, b_exact)


def _dot_split_rhs(a_exact, b):
    hi, mid, lo = _split3(b)
    return _dot(a_exact, hi) + _dot(a_exact, mid) + _dot(a_exact, lo)


def _dot_nt(a, b):
    return lax.dot_general(a, b, (((1,), (1,)), ((), ())), preferred_element_type=F32)


def _silu(x):
    return x * jax.nn.sigmoid(x)


def _softplus(x):
    return jnp.maximum(x, 0.0) + jnp.log1p(jnp.exp(-jnp.abs(x)))


ADA_TN = 1536


def _adaln_kernel(c_ref, w_ref, b_ref, o_ref):
    s = _silu(c_ref[...]).astype(BF16)
    o_ref[...] = _dot(s, w_ref[...].astype(BF16)) + b_ref[...]


def _adaln(cond, ada_w, ada_b):
    n_out = N_MOD * D_MODEL
    return pl.pallas_call(
        _adaln_kernel,
        grid=(DEPTH, n_out // ADA_TN),
        in_specs=[
            pl.BlockSpec((N_COND, D_MODEL), lambda l, j: (0, 0)),
            pl.BlockSpec((None, D_MODEL, ADA_TN), lambda l, j: (l, 0, j)),
            pl.BlockSpec((None, 1, ADA_TN), lambda l, j: (l, 0, j)),
        ],
        out_specs=pl.BlockSpec((None, N_COND, ADA_TN), lambda l, j: (l, 0, j)),
        out_shape=jax.ShapeDtypeStruct((DEPTH, N_COND, n_out), F32),
        compiler_params=_cparams(("arbitrary", "arbitrary")),
        name="adaln",
    )(cond, ada_w, ada_b.reshape(DEPTH, 1, n_out))


def _mod_spec(layer, which, tm):
    n_ctx_tiles = N_CTX_TOK // tm
    tiles_per_dec = DEC_LEN // tm

    def index(i, *_):
        cond = jnp.where(i < n_ctx_tiles, 0, 1 + (i - n_ctx_tiles) // tiles_per_dec)
        return ((layer * N_COND + cond) * N_MOD + which, 0, 0)

    return pl.BlockSpec((None, 1, D_MODEL), index)


def _rms_modulate(x, g, sc, sh):
    var = jnp.mean(x * x, axis=-1, keepdims=True)
    y = x * lax.rsqrt(var + EPS) * g
    return y * (1.0 + sc) + sh


INPROJ_TM = 512
INPROJ_TN = 2048


def _inproj_kernel(x_ref, sh_ref, sc_ref, g_ref, w_ref, o_ref, h_scr):
    @pl.when(pl.program_id(1) == 0)
    def _():
        h_scr[...] = _rms_modulate(x_ref[...], g_ref[...], sc_ref[...], sh_ref[...]).astype(BF16)

    o_ref[...] = _dot(h_scr[...], w_ref[...])


def _inproj(x, mod3, layer, norm_g, w_perm):
    tm, tn = INPROJ_TM, INPROJ_TN
    return pl.pallas_call(
        _inproj_kernel,
        grid=(N_TOK // tm, PROJ_W // tn),
        in_specs=[
            pl.BlockSpec((tm, D_MODEL), lambda i, j: (i, 0)),
            _mod_spec(layer, 0, tm),
            _mod_spec(layer, 1, tm),
            pl.BlockSpec((1, D_MODEL), lambda i, j: (0, 0)),
            pl.BlockSpec((D_MODEL, tn), lambda i, j: (0, j)),
        ],
        out_specs=pl.BlockSpec((tm, tn), lambda i, j: (i, j)),
        out_shape=jax.ShapeDtypeStruct((N_TOK, PROJ_W), F32),
        scratch_shapes=[pltpu.VMEM((tm, D_MODEL), BF16)],
        compiler_params=_cparams(("arbitrary", "arbitrary")),
        name="inproj",
    )(x, mod3, mod3, norm_g.reshape(1, D_MODEL), w_perm)


def _walk(nc):
    def direction(t):
        return jnp.where(t < nc, 1, 0)

    def chunk(t):
        return jnp.where(t < nc, nc - 1 - t, t - nc)

    return direction, chunk


def _seq_specs(nc, row_off):
    direction, chunk = _walk(nc)

    def rows(s, t):
        return row_off + s * nc + chunk(t)

    def out_rows(s, t):
        return row_off + s * nc + jnp.where(t < nc, 0, t - nc)

    return direction, chunk, rows, out_rows


def _ssd_consts():
    idx = np.arange(CHUNK)
    tri = np.stack([(idx[None, :] <= idx[:, None]), (idx[None, :] >= idx[:, None])]).astype(np.float32)
    e_p = np.zeros((CHUNK, D_SSD), np.float32)
    e_n = np.zeros((CHUNK, SSD_HEADS * CHUNK), np.float32)
    for h in range(SSD_HEADS):
        e_p[h, h * SSD_P:(h + 1) * SSD_P] = 1.0
        e_n[h, h * CHUNK:(h + 1) * CHUNK] = 1.0
    lane = np.arange(CHUNK)
    pair = np.concatenate([np.broadcast_to(lane < SSD_P, (CHUNK, CHUNK)),
                           np.broadcast_to(lane >= SSD_P, (CHUNK, CHUNK))]).astype(np.float32)
    return tri, e_p, e_n, pair


def _ssd_kernel(nc, zero_init, emit_state, *refs):
    (z_ref, xc_ref, xp_ref, xn_ref, dt_ref, cw_ref, cb_ref, dtb_ref, aexp_ref, dsk_ref, ng_ref,
     tri_ref, ep_ref, en_ref, pair_ref) = refs[:15]
    pos = 15
    s0_ref = None
    if not zero_init:
        s0_ref = refs[pos]
        pos += 1
    y_ref = refs[pos]
    pos += 1
    st_ref = None
    if emit_state:
        st_ref = refs[pos]
        pos += 1
    state_scr, stash_scr = refs[pos:]

    t = pl.program_id(1)
    fwd = t >= nc
    step = jnp.where(fwd, t - nc, t)
    chunk = jnp.where(fwd, t - nc, nc - 1 - t)
    row0 = pl.multiple_of(chunk * CHUNK, CHUNK)

    @pl.when(step == 0)
    def _():
        if zero_init:
            state_scr[...] = jnp.zeros_like(state_scr)
        else:
            state_scr[...] = s0_ref[...]

    prev = jnp.where(chunk > 0, xp_ref[...], 0.0)
    nxt = jnp.where(chunk < nc - 1, xn_ref[...], 0.0)
    ext = jnp.concatenate([prev, xc_ref[...], nxt], axis=0)
    conv = cb_ref[...]
    for k in range(SSD_CONV_K):
        off = 8 + k - SSD_CONV_K // 2
        conv = conv + cw_ref[k:k + 1, :] * ext[off:off + CHUNK, :]
    xbc = _silu(conv)
    x = xbc[:, :D_SSD]
    bm = xbc[:, D_SSD:D_SSD + SSD_GROUPS * SSD_N]
    cm = xbc[:, D_SSD + SSD_GROUPS * SSD_N:]

    dt = _softplus(dt_ref[...] + dtb_ref[...])
    la = -dt * aexp_ref[...]
    tri = tri_ref[...]
    cum = _dot_split_rhs(tri.astype(BF16), la)
    ep = ep_ref[...]
    cum_p = _dot_split_lhs(cum, ep)
    dt_p = _dot_split_lhs(dt, ep)
    cum_col = _dot_split_lhs(cum, en_ref[...])
    cum_t = cum.T
    tot_p = jnp.where(fwd, cum_p[CHUNK - 1:CHUNK, :], cum_p[0:1, :])

    v = x * dt_p
    vb = v.astype(BF16)
    mask = tri > 0.5
    pair = pair_ref[...]
    y_parts = []
    for g in range(SSD_GROUPS):
        cg = cm[:, g * SSD_N:(g + 1) * SSD_N].astype(BF16)
        bg = bm[:, g * SSD_N:(g + 1) * SSD_N]
        gmat = _dot_nt(cg, bg.astype(BF16))
        for hp in range(SSD_HPG // 2):
            scs = []
            for h in (g * SSD_HPG + 2 * hp, g * SSD_HPG + 2 * hp + 1):
                ci = cum_col[:, h * CHUNK:(h + 1) * CHUNK]
                cj = cum_t[h:h + 1, :]
                dec = jnp.exp(jnp.where(mask, ci - cj, NEG_INF))
                scs.append((gmat * dec).astype(BF16))
            c0 = (g * SSD_HPG + 2 * hp) * SSD_P
            v2 = vb[:, c0:c0 + 2 * SSD_P]
            vv = jnp.concatenate([v2, v2], axis=0) * pair
            y_parts.append(_dot(jnp.concatenate(scs, axis=1), vv))
    y = jnp.concatenate(y_parts, axis=1)

    w_p = SSD_HPG * SSD_P
    y_off = jnp.concatenate(
        [_dot(cm[:, g * SSD_N:(g + 1) * SSD_N].astype(BF16), state_scr[g].astype(BF16))
         for g in range(SSD_GROUPS)], axis=1)
    y = y + y_off * jnp.exp(cum_p)

    vw = (v * jnp.exp(tot_p - cum_p)).astype(BF16)
    cdec = jnp.exp(tot_p)
    for g in range(SSD_GROUPS):
        bt = bm[:, g * SSD_N:(g + 1) * SSD_N].T.astype(BF16)
        state_scr[g] = state_scr[g] * cdec[:, g * w_p:(g + 1) * w_p] + _dot(bt, vw[:, g * w_p:(g + 1) * w_p])

    @pl.when(jnp.logical_not(fwd))
    def _():
        stash_scr[pl.ds(row0, CHUNK), :] = y

    @pl.when(fwd)
    def _():
        ytot = y + stash_scr[pl.ds(row0, CHUNK), :] + x * dsk_ref[...]
        gated = ytot * _silu(z_ref[...])
        var = jnp.mean(gated * gated, axis=-1, keepdims=True)
        y_ref[...] = (gated * lax.rsqrt(var + EPS) * ng_ref[...]).astype(y_ref.dtype)

    if emit_state:
        @pl.when(step == nc - 1)
        def _():
            st_ref[...] = state_scr[...]


def _ssd(proj, p, n_seq, nc, row_off, s0):
    zero_init = s0 is None
    emit_state = zero_init
    direction, chunk, rows, out_rows = _seq_specs(nc, row_off)
    tri, e_p, e_n, pair = _ssd_consts()
    n8 = N_TOK // 8
    w_st = SSD_HPG * SSD_P
    in_specs = [
        pl.BlockSpec((CHUNK, D_SSD), lambda s, t: (rows(s, t), COL_Z // D_SSD)),
        pl.BlockSpec((CHUNK, SSD_CONV_CH), lambda s, t: (rows(s, t), COL_XBC // SSD_CONV_CH)),
        pl.BlockSpec((8, SSD_CONV_CH),
                     lambda s, t: (jnp.maximum(rows(s, t) * (CHUNK // 8) - 1, 0), COL_XBC // SSD_CONV_CH)),
        pl.BlockSpec((8, SSD_CONV_CH),
                     lambda s, t: (jnp.minimum((rows(s, t) + 1) * (CHUNK // 8), n8 - 1), COL_XBC // SSD_CONV_CH)),
        pl.BlockSpec((CHUNK, CHUNK), lambda s, t: (rows(s, t), COL_DT // CHUNK + direction(t))),
        pl.BlockSpec((8, SSD_CONV_CH), lambda s, t: (0, 0)),
        pl.BlockSpec((1, SSD_CONV_CH), lambda s, t: (0, 0)),
        pl.BlockSpec((None, 1, CHUNK), lambda s, t: (direction(t), 0, 0)),
        pl.BlockSpec((None, 1, CHUNK), lambda s, t: (direction(t), 0, 0)),
        pl.BlockSpec((1, D_SSD), lambda s, t: (0, 0)),
        pl.BlockSpec((1, D_SSD), lambda s, t: (0, 0)),
        pl.BlockSpec((None, CHUNK, CHUNK), lambda s, t: (direction(t), 0, 0)),
        pl.BlockSpec((CHUNK, D_SSD), lambda s, t: (0, 0)),
        pl.BlockSpec((CHUNK, SSD_HEADS * CHUNK), lambda s, t: (0, 0)),
        pl.BlockSpec((2 * CHUNK, CHUNK), lambda s, t: (0, 0)),
    ]
    args = [proj, proj, proj, proj, proj, p["conv_w"], p["conv_b"], p["dt_bias"], p["a_exp"], p["d_skip"],
            p["norm_g"], jnp.asarray(tri), jnp.asarray(e_p, BF16), jnp.asarray(e_n, BF16), jnp.asarray(pair, BF16)]
    st_spec = pl.BlockSpec((None, None, SSD_GROUPS, SSD_N, w_st), lambda s, t: (s, direction(t), 0, 0, 0))
    if not zero_init:
        in_specs.append(st_spec)
        args.append(s0)
    out_specs = [pl.BlockSpec((CHUNK, D_SSD), lambda s, t: (out_rows(s, t) - row_off, 0))]
    out_shape = [jax.ShapeDtypeStruct((n_seq * nc * CHUNK, D_SSD), BF16)]
    if emit_state:
        out_specs.append(st_spec)
        out_shape.append(jax.ShapeDtypeStruct((n_seq, 2, SSD_GROUPS, SSD_N, w_st), F32))
    res = pl.pallas_call(
        functools.partial(_ssd_kernel, nc, zero_init, emit_state),
        grid=(n_seq, 2 * nc),
        in_specs=in_specs,
        out_specs=out_specs,
        out_shape=out_shape,
        scratch_shapes=[pltpu.VMEM((SSD_GROUPS, SSD_N, w_st), F32), pltpu.VMEM((nc * CHUNK, D_SSD), F32)],
        compiler_params=_cparams(("arbitrary", "arbitrary")),
        name="ssd_ctx" if zero_init else "ssd_dec",
    )(*args)
    return res if emit_state else (res[0], None)


def _s5_prep_kernel(lr_ref, li_ref, ldt_ref, pre_ref, pim_ref, kre_ref, kim_ref):
    lr = lr_ref[...]
    li = li_ref[...]
    dt = jnp.exp(ldt_ref[...])
    kk = (lax.broadcasted_iota(jnp.int32, (CHUNK, S5_LANES), 0) + 1).astype(F32)
    mag = jnp.exp(kk * (lr * dt))
    ang = kk * (li * dt)
    p_re = mag * jnp.cos(ang)
    p_im = mag * jnp.sin(ang)
    pre_ref[...] = p_re
    pim_ref[...] = p_im
    a_re = p_re[0:1, :]
    a_im = p_im[0:1, :]
    den = lr * lr + li * li
    num_re = a_re - 1.0
    kre_ref[...] = (num_re * lr + a_im * li) / den
    kim_ref[...] = (a_im * lr - num_re * li) / den


def _s5_prep(lam_re, lam_im, log_dt):
    n = DEPTH * 2
    lr = lam_re.reshape(n, 1, S5_LANES)
    li = lam_im.reshape(n, 1, S5_LANES)
    ldt = jnp.broadcast_to(log_dt[..., None], (DEPTH, 2, S5_GROUPS, S5_STATE)).reshape(n, 1, S5_LANES)
    row = pl.BlockSpec((None, 1, S5_LANES), lambda i: (i, 0, 0))
    tab = pl.BlockSpec((None, CHUNK, S5_LANES), lambda i: (i, 0, 0))
    return pl.pallas_call(
        _s5_prep_kernel,
        grid=(n,),
        in_specs=[row, row, row],
        out_specs=[tab, tab, row, row],
        out_shape=[jax.ShapeDtypeStruct((n, CHUNK, S5_LANES), F32)] * 2
        + [jax.ShapeDtypeStruct((n, 1, S5_LANES), F32)] * 2,
        compiler_params=_cparams(("arbitrary",)),
        name="s5_prep",
    )(lr, li, ldt)


def _shift_rows(x, d):
    if d % 8 == 0:
        return jnp.concatenate([jnp.zeros((d, x.shape[1]), x.dtype), x[:x.shape[0] - d, :]], axis=0)
    rolled = pltpu.roll(x, d, axis=0)
    rows = lax.broadcasted_iota(jnp.int32, x.shape, 0)
    return jnp.where(rows >= d, rolled, 0.0)


def _s5_kernel(nc, zero_init, emit_state, *refs):
    (u_ref, jm_ref, wb_ref, pre_ref, pim_ref, wc_ref, dsk_ref, gw_ref, gb_ref) = refs[:9]
    pos = 9
    x0re_ref = x0im_ref = None
    if not zero_init:
        x0re_ref, x0im_ref = refs[pos:pos + 2]
        pos += 2
    y_ref = refs[pos]
    pos += 1
    sre_ref = sim_ref = None
    if emit_state:
        sre_ref, sim_ref = refs[pos:pos + 2]
        pos += 2
    cre_scr, cim_scr, stash_scr = refs[pos:]

    t = pl.program_id(1)
    fwd = t >= nc
    step = jnp.where(fwd, t - nc, t)
    chunk = jnp.where(fwd, t - nc, nc - 1 - t)
    row0 = pl.multiple_of(chunk * CHUNK, CHUNK)

    @pl.when(step == 0)
    def _():
        if zero_init:
            cre_scr[...] = jnp.zeros_like(cre_scr)
            cim_scr[...] = jnp.zeros_like(cim_scr)
        else:
            cre_scr[...] = x0re_ref[...]
            cim_scr[...] = x0im_ref[...]

    u = u_ref[...]
    jm = jm_ref[...]
    us = _dot(jm, u.astype(BF16)).astype(BF16)
    bu = _dot(us, wb_ref[...])
    xr = bu[:, :S5_LANES]
    xi = bu[:, S5_LANES:]
    d = 1
    while d < CHUNK:
        ar = pre_ref[d - 1:d, :]
        ai = pim_ref[d - 1:d, :]
        sr = _shift_rows(xr, d)
        si = _shift_rows(xi, d)
        xr, xi = xr + (ar * sr - ai * si), xi + (ar * si + ai * sr)
        d *= 2
    cr = cre_scr[...]
    ci = cim_scr[...]
    pr = pre_ref[...]
    pi = pim_ref[...]
    xr, xi = xr + (pr * cr - pi * ci), xi + (pr * ci + pi * cr)
    cre_scr[...] = xr[CHUNK - 1:CHUNK, :]
    cim_scr[...] = xi[CHUNK - 1:CHUNK, :]

    ys = _dot(xr.astype(BF16), wc_ref[:S5_LANES, :]) + _dot(xi.astype(BF16), wc_ref[S5_LANES:, :])
    yh = ys.astype(BF16)
    yl = (ys - yh.astype(F32)).astype(BF16)
    y = _dot(jm, yh) + _dot(jm, yl)

    @pl.when(jnp.logical_not(fwd))
    def _():
        stash_scr[pl.ds(row0, CHUNK), :] = y

    @pl.when(fwd)
    def _():
        yt = y + stash_scr[pl.ds(row0, CHUNK), :] + u * dsk_ref[...]
        yt = jax.nn.gelu(yt)
        gate = jax.nn.sigmoid(_dot(yt.astype(BF16), gw_ref[...]) + gb_ref[...])
        y_ref[...] = (yt * gate).astype(y_ref.dtype)

    if emit_state:
        @pl.when(step == nc - 1)
        def _():
            sre_ref[...] = cre_scr[...]
            sim_ref[...] = cim_scr[...]


def _s5(proj, p, n_seq, nc, row_off, x0):
    zero_init = x0 is None
    emit_state = zero_init
    direction, chunk, rows, out_rows = _seq_specs(nc, row_off)
    eye = np.eye(CHUNK, dtype=np.float32)
    jm = jnp.asarray(np.stack([eye, eye[::-1]]), BF16)
    lyr = p["layer"]

    def dsel(t):
        return lyr * 2 + direction(t)

    in_specs = [
        pl.BlockSpec((CHUNK, D_S5), lambda s, t: (rows(s, t), COL_U // D_S5)),
        pl.BlockSpec((None, CHUNK, CHUNK), lambda s, t: (direction(t), 0, 0)),
        pl.BlockSpec((None, D_S5, 2 * S5_LANES), lambda s, t: (direction(t), 0, 0)),
        pl.BlockSpec((None, CHUNK, S5_LANES), lambda s, t: (dsel(t), 0, 0)),
        pl.BlockSpec((None, CHUNK, S5_LANES), lambda s, t: (dsel(t), 0, 0)),
        pl.BlockSpec((None, 2 * S5_LANES, D_S5), lambda s, t: (direction(t), 0, 0)),
        pl.BlockSpec((1, D_S5), lambda s, t: (0, 0)),
        pl.BlockSpec((D_S5, D_S5), lambda s, t: (0, 0)),
        pl.BlockSpec((1, D_S5), lambda s, t: (0, 0)),
    ]
    args = [proj, jm, p["wb"], p["p_re"], p["p_im"], p["wc"], p["d_skip"], p["glu_w"], p["glu_b"]]
    st_spec = pl.BlockSpec((None, None, 1, S5_LANES), lambda s, t: (s, direction(t), 0, 0))
    if not zero_init:
        in_specs += [st_spec, st_spec]
        args += [x0[0], x0[1]]
    out_specs = [pl.BlockSpec((CHUNK, D_S5), lambda s, t: (out_rows(s, t) - row_off, 0))]
    out_shape = [jax.ShapeDtypeStruct((n_seq * nc * CHUNK, D_S5), BF16)]
    if emit_state:
        out_specs += [st_spec, st_spec]
        out_shape += [jax.ShapeDtypeStruct((n_seq, 2, 1, S5_LANES), F32)] * 2
    res = pl.pallas_call(
        functools.partial(_s5_kernel, nc, zero_init, emit_state),
        grid=(n_seq, 2 * nc),
        in_specs=in_specs,
        out_specs=out_specs,
        out_shape=out_shape,
        scratch_shapes=[pltpu.VMEM((1, S5_LANES), F32), pltpu.VMEM((1, S5_LANES), F32),
                        pltpu.VMEM((nc * CHUNK, D_S5), F32)],
        compiler_params=_cparams(("arbitrary", "arbitrary")),
        name="s5_ctx" if zero_init else "s5_dec",
    )(*args)
    return res if emit_state else (res[0], None, None)


def _ret_consts():
    heads = np.arange(RET_HEADS, dtype=np.float64)
    lg = np.stack([np.log1p(-np.exp2(-5.0 - heads)), np.log1p(-np.exp2(-5.5 - heads))])
    i = np.arange(CHUNK, dtype=np.float64)
    diff = i[:, None] - i[None, :]
    dmat = np.zeros((2, RET_HEADS, CHUNK, CHUNK))
    rowdec = np.zeros((2, CHUNK, D_RET))
    wend = np.zeros((2, 8, CHUNK))
    cdec = np.zeros((2, 8, CHUNK))
    for h in range(RET_HEADS):
        dmat[0, h] = np.where(diff >= 0, np.exp(lg[0, h] * diff), 0.0)
        dmat[1, h] = np.where(diff <= 0, np.exp(-lg[1, h] * diff), 0.0)
        rowdec[0, :, h * RET_DV:(h + 1) * RET_DV] = np.exp(lg[0, h] * (i + 1))[:, None]
        rowdec[1, :, h * RET_DV:(h + 1) * RET_DV] = np.exp(lg[1, h] * (CHUNK - i))[:, None]
        wend[0, h] = np.exp(lg[0, h] * (CHUNK - 1 - i))
        wend[1, h] = np.exp(lg[1, h] * i)
        cdec[:, h] = np.exp(lg[:, h] * CHUNK)[:, None]
    return [jnp.asarray(a, F32) for a in (dmat, rowdec, wend, cdec)]


def _rope_tables():
    t = np.arange(DEC_LEN)
    row = (t // GRID_W).astype(np.float32)
    col = (t % GRID_W).astype(np.float32)
    quarter = RET_DK // 4
    freqs = (ROPE_BASE ** (-np.arange(quarter, dtype=np.float32) / quarter)).astype(np.float32)
    ar = (row[:, None] * freqs[None, :]).astype(np.float64)
    ac = (col[:, None] * freqs[None, :]).astype(np.float64)
    cos = np.concatenate([np.cos(ar), np.cos(ar), np.cos(ac), np.cos(ac)], axis=1)
    sin = np.concatenate([-np.sin(ar), np.sin(ar), -np.sin(ac), np.sin(ac)], axis=1)
    return jnp.asarray(cos, F32), jnp.asarray(sin, F32)


def _ret_kernel(nc, zero_init, emit_state, rope, *refs):
    q_ref, k_ref, v_ref, g_ref = refs[:4]
    pos = 4
    cos_ref = sin_ref = None
    if rope:
        cos_ref, sin_ref = refs[pos:pos + 2]
        pos += 2
    dmat_ref, rowdec_ref, wend_ref, cdec_ref, gn_ref = refs[pos:pos + 5]
    pos += 5
    s0_ref = None
    if not zero_init:
        s0_ref = refs[pos]
        pos += 1
    y_ref = refs[pos]
    pos += 1
    st_ref = None
    if emit_state:
        st_ref = refs[pos]
        pos += 1
    state_scr, stash_scr = refs[pos:]

    t = pl.program_id(1)
    fwd = t >= nc
    step = jnp.where(fwd, t - nc, t)
    chunk = jnp.where(fwd, t - nc, nc - 1 - t)
    row0 = pl.multiple_of(chunk * CHUNK, CHUNK)

    @pl.when(step == 0)
    def _():
        if zero_init:
            state_scr[...] = jnp.zeros_like(state_scr)
        else:
            state_scr[...] = s0_ref[...]

    q = q_ref[...]
    k = k_ref[...]
    if rope:
        cos = jnp.concatenate([cos_ref[...]] * RET_HEADS, axis=1)
        sin = jnp.concatenate([sin_ref[...]] * RET_HEADS, axis=1)
        lane = lax.broadcasted_iota(jnp.int32, (CHUNK, D_RET), 1)
        first = (lane // (RET_DK // 4)) % 2 == 0

        def rot(x):
            partner = jnp.where(first, pltpu.roll(x, D_RET - RET_DK // 4, axis=1), pltpu.roll(x, RET_DK // 4, axis=1))
            return x * cos + partner * sin

        q = rot(q)
        k = rot(k)
    k = k * (RET_DK ** -0.5)
    qb = q.astype(BF16)
    vb = v_ref[...].astype(BF16)
    rowdec = rowdec_ref[...]
    y_parts = []
    for h in range(RET_HEADS):
        sl = slice(h * RET_DK, (h + 1) * RET_DK)
        kh = k[:, sl]
        sc = _dot_nt(qb[:, sl], kh.astype(BF16)) * dmat_ref[h]
        yh = _dot(sc.astype(BF16), vb[:, sl])
        yh = yh + _dot(qb[:, sl], state_scr[h].astype(BF16)) * rowdec[:, sl]
        y_parts.append(yh)
        kt = (kh.T * wend_ref[h:h + 1, :]).astype(BF16)
        state_scr[h] = state_scr[h] * cdec_ref[h:h + 1, :] + _dot(kt, vb[:, sl])
    y = jnp.concatenate(y_parts, axis=1)

    @pl.when(jnp.logical_not(fwd))
    def _():
        stash_scr[pl.ds(row0, CHUNK), :] = y

    @pl.when(fwd)
    def _():
        yt = y + stash_scr[pl.ds(row0, CHUNK), :]
        outs = []
        for h in range(RET_HEADS):
            yh = yt[:, h * RET_DV:(h + 1) * RET_DV]
            yc = yh - jnp.mean(yh, axis=-1, keepdims=True)
            outs.append(yc * lax.rsqrt(jnp.mean(yc * yc, axis=-1, keepdims=True) + EPS))
        yn = jnp.concatenate(outs, axis=1)
        y_ref[...] = (yn * gn_ref[...] * _silu(g_ref[...])).astype(y_ref.dtype)

    if emit_state:
        @pl.when(step == nc - 1)
        def _():
            st_ref[...] = state_scr[...]


def _ret(proj, gn_g, n_seq, nc, row_off, s0, rope):
    zero_init = s0 is None
    emit_state = zero_init
    direction, chunk, rows, out_rows = _seq_specs(nc, row_off)
    dmat, rowdec, wend, cdec = _ret_consts()

    def col(c):
        return pl.BlockSpec((CHUNK, D_RET), lambda s, t: (rows(s, t), c // D_RET))

    in_specs = [col(COL_Q), col(COL_K), col(COL_V), col(COL_G)]
    args = [proj, proj, proj, proj]
    if rope:
        cos, sin = _rope_tables()
        tab = pl.BlockSpec((CHUNK, RET_DK), lambda s, t: (chunk(t), 0))
        in_specs += [tab, tab]
        args += [cos, sin]
    in_specs += [
        pl.BlockSpec((None, RET_HEADS, CHUNK, CHUNK), lambda s, t: (direction(t), 0, 0, 0)),
        pl.BlockSpec((None, CHUNK, D_RET), lambda s, t: (direction(t), 0, 0)),
        pl.BlockSpec((None, 8, CHUNK), lambda s, t: (direction(t), 0, 0)),
        pl.BlockSpec((None, 8, CHUNK), lambda s, t: (direction(t), 0, 0)),
        pl.BlockSpec((1, D_RET), lambda s, t: (0, 0)),
    ]
    args += [dmat, rowdec, wend, cdec, gn_g.reshape(1, D_RET)]
    st_spec = pl.BlockSpec((None, None, RET_HEADS, RET_DK, RET_DV), lambda s, t: (s, direction(t), 0, 0, 0))
    if not zero_init:
        in_specs.append(st_spec)
        args.append(s0)
    out_specs = [pl.BlockSpec((CHUNK, D_RET), lambda s, t: (out_rows(s, t) - row_off, 0))]
    out_shape = [jax.ShapeDtypeStruct((n_seq * nc * CHUNK, D_RET), BF16)]
    if emit_state:
        out_specs.append(st_spec)
        out_shape.append(jax.ShapeDtypeStruct((n_seq, 2, RET_HEADS, RET_DK, RET_DV), F32))
    res = pl.pallas_call(
        functools.partial(_ret_kernel, nc, zero_init, emit_state, rope),
        grid=(n_seq, 2 * nc),
        in_specs=in_specs,
        out_specs=out_specs,
        out_shape=out_shape,
        scratch_shapes=[pltpu.VMEM((RET_HEADS, RET_DK, RET_DV), F32), pltpu.VMEM((nc * CHUNK, D_RET), F32)],
        compiler_params=_cparams(("arbitrary", "arbitrary")),
        name="ret_ctx" if zero_init else "ret_dec",
    )(*args)
    return res if emit_state else (res[0], None)


OUTPROJ_TM = 512


def _outproj_kernel(x_ref, ya_ref, yb_ref, yc_ref, w_ref, g_ref, o_ref):
    y = _dot(ya_ref[...], w_ref[:D_SSD, :])
    y = y + _dot(yb_ref[...], w_ref[D_SSD:D_SSD + D_S5, :])
    y = y + _dot(yc_ref[...], w_ref[D_SSD + D_S5:, :])
    o_ref[...] = x_ref[...] + g_ref[...] * y


def _outproj(x, y_ssd, y_s5, y_ret, w_out, mod3, layer):
    tm = OUTPROJ_TM
    return pl.pallas_call(
        _outproj_kernel,
        grid=(N_TOK // tm,),
        in_specs=[
            pl.BlockSpec((tm, D_MODEL), lambda i: (i, 0)),
            pl.BlockSpec((tm, D_SSD), lambda i: (i, 0)),
            pl.BlockSpec((tm, D_S5), lambda i: (i, 0)),
            pl.BlockSpec((tm, D_RET), lambda i: (i, 0)),
            pl.BlockSpec((D_MODEL, D_MODEL), lambda i: (0, 0)),
            _mod_spec(layer, 2, tm),
        ],
        out_specs=pl.BlockSpec((tm, D_MODEL), lambda i: (i, 0)),
        out_shape=jax.ShapeDtypeStruct((N_TOK, D_MODEL), F32),
        compiler_params=_cparams(("arbitrary",)),
        name="outproj",
    )(x, y_ssd, y_s5, y_ret, w_out, mod3)


PEER_SC_TM = 256
PEER_NCAND = PEER_TOPK + 1


def _cand_pairs():
    return [(i, j) for i in range(PEER_NCAND) for j in range(PEER_NCAND) if (i + 1) * (j + 1) <= PEER_NCAND]


def _top_rows(work, n):
    rows = []
    for r in range(n):
        m = jnp.max(work, axis=0, keepdims=True)
        rows.append(m)
        if r < n - 1:
            work = jnp.where(work >= m, NEG_INF, work)
    return rows


def _peer_scores_kernel(x_ref, sh_ref, sc_ref, g_ref, wq_ref, keys_ref,
                        h_ref, thr_ref, g1_ref, s2_ref, e2_ref, cand_scr):
    tm = x_ref.shape[0]
    hb = _rms_modulate(x_ref[...], g_ref[...], sc_ref[...], sh_ref[...]).astype(BF16)
    h_ref[...] = hb
    q = _dot(hb, wq_ref[...]).astype(BF16)
    half = PEER_DQ // 2
    pairs = _cand_pairs()
    n_rows = cand_scr.shape[0]
    cand_scr[len(pairs):, :] = jnp.full((n_rows - len(pairs), tm), NEG_INF, F32)
    for h in range(PEER_HEADS):
        q1 = q[:, h * PEER_DQ:h * PEER_DQ + half]
        q2 = q[:, h * PEER_DQ + half:(h + 1) * PEER_DQ]
        s1 = _dot_nt(keys_ref[h, 0], q1)
        s2 = _dot_nt(keys_ref[h, 1], q2)
        a = _top_rows(s1, PEER_NCAND)
        b = _top_rows(s2, PEER_NCAND)
        for r, (i, j) in enumerate(pairs):
            cand_scr[r:r + 1, :] = a[i] + b[j]
        c = _top_rows(cand_scr[...], PEER_NCAND)
        top = a[0] + b[0]
        zsum = jnp.zeros_like(top)
        for r in range(PEER_TOPK):
            zsum = zsum + jnp.exp(c[r] - top)
        tau = 0.5 * (c[PEER_TOPK - 1] + c[PEER_TOPK])
        thr_ref[h] = tau - s1
        g1_ref[h] = jnp.exp(s1 - a[0]) / zsum
        s2_ref[h] = s2
        e2_ref[h] = jnp.exp(s2 - b[0])


def _peer_scores(x, mod3, layer, norm_g, wq, keys):
    tm = PEER_SC_TM
    n_tok = x.shape[0]
    sc_spec = pl.BlockSpec((PEER_HEADS, PEER_NKEYS, tm), lambda i: (0, 0, i))
    sc_shape = jax.ShapeDtypeStruct((PEER_HEADS, PEER_NKEYS, n_tok), F32)
    n_cand_rows = -(-len(_cand_pairs()) // 8) * 8
    return pl.pallas_call(
        _peer_scores_kernel,
        grid=(n_tok // tm,),
        in_specs=[
            pl.BlockSpec((tm, D_MODEL), lambda i: (i, 0)),
            _mod_spec(layer, 3, tm),
            _mod_spec(layer, 4, tm),
            pl.BlockSpec((1, D_MODEL), lambda i: (0, 0)),
            pl.BlockSpec((D_MODEL, PEER_HEADS * PEER_DQ), lambda i: (0, 0)),
            pl.BlockSpec((PEER_HEADS, 2, PEER_NKEYS, PEER_DQ // 2), lambda i: (0, 0, 0, 0)),
        ],
        out_specs=[pl.BlockSpec((tm, D_MODEL), lambda i: (i, 0)), sc_spec, sc_spec, sc_spec, sc_spec],
        out_shape=[jax.ShapeDtypeStruct((n_tok, D_MODEL), BF16), sc_shape, sc_shape, sc_shape, sc_shape],
        scratch_shapes=[pltpu.VMEM((n_cand_rows, tm), F32)],
        compiler_params=_cparams(("arbitrary",)),
        name="peer_scores",
    )(x, mod3, mod3, norm_g.reshape(1, D_MODEL), wq, keys)


PEER_TM = 256
PEER_EBLK = 1024


def _peer_experts_kernel(x_ref, g2_ref, h_ref, thr_ref, g1_ref, s2_ref, e2_ref, u_ref, vt_ref,
                         o_ref, acc_scr, w_scr):
    tm = x_ref.shape[0]
    eblk = u_ref.shape[0]
    j = pl.program_id(1)

    @pl.when(j == 0)
    def _():
        acc_scr[...] = jnp.zeros_like(acc_scr)

    nb = eblk // PEER_NKEYS
    for a in range(nb):
        for lg in range(tm // 128):
            sl = slice(lg * 128, (lg + 1) * 128)
            acc = jnp.zeros((PEER_NKEYS, 128), F32)
            for h in range(PEER_HEADS):
                thr = thr_ref[h, a:a + 1, sl]
                g1 = g1_ref[h, a:a + 1, sl]
                acc = acc + jnp.where(s2_ref[h, :, sl] >= thr, e2_ref[h, :, sl], 0.0) * g1
            w_scr[a * PEER_NKEYS:(a + 1) * PEER_NKEYS, sl] = acc

    act = _dot_nt(u_ref[...], h_ref[...])
    gl = (jax.nn.gelu(act) * w_scr[...]).astype(BF16)
    acc_scr[...] += _dot(vt_ref[...], gl)

    @pl.when(j == pl.num_programs(1) - 1)
    def _():
        o_ref[...] = x_ref[...] + g2_ref[...] * acc_scr[...].T


def _peer_experts(x, mod3, layer, hb, thr, g1, s2, e2, u_bf, vt_bf):
    tm, eblk = PEER_TM, PEER_EBLK
    n_tok = x.shape[0]
    sc_spec = pl.BlockSpec((PEER_HEADS, PEER_NKEYS, tm), lambda i, j: (0, 0, i))
    k1_spec = pl.BlockSpec((PEER_HEADS, eblk // PEER_NKEYS, tm), lambda i, j: (0, j, i))
    return pl.pallas_call(
        _peer_experts_kernel,
        grid=(n_tok // tm, PEER_EXPERTS // eblk),
        in_specs=[
            pl.BlockSpec((tm, D_MODEL), lambda i, j: (i, 0)),
            _mod_spec(layer, 5, tm),
            pl.BlockSpec((tm, D_MODEL), lambda i, j: (i, 0)),
            k1_spec, k1_spec, sc_spec, sc_spec,
            pl.BlockSpec((eblk, D_MODEL), lambda i, j: (j, 0)),
            pl.BlockSpec((D_MODEL, eblk), lambda i, j: (0, j)),
        ],
        out_specs=pl.BlockSpec((tm, D_MODEL), lambda i, j: (i, 0)),
        out_shape=jax.ShapeDtypeStruct((n_tok, D_MODEL), F32),
        scratch_shapes=[pltpu.VMEM((D_MODEL, tm), F32), pltpu.VMEM((eblk, tm), F32)],
        compiler_params=_cparams(("arbitrary", "arbitrary")),
        name="peer_experts",
    )(x, mod3, hb, thr, g1, s2, e2, u_bf, vt_bf)


FINAL_TM = 512


def _final_norm_kernel(x_ref, g_ref, o_ref):
    x = x_ref[...]
    var = jnp.mean(x * x, axis=-1, keepdims=True)
    o_ref[...] = x * lax.rsqrt(var + EPS) * g_ref[...]


def _final_norm(x, g):
    tm = FINAL_TM
    return pl.pallas_call(
        _final_norm_kernel,
        grid=(N_TOK // tm,),
        in_specs=[pl.BlockSpec((tm, D_MODEL), lambda i: (i, 0)), pl.BlockSpec((1, D_MODEL), lambda i: (0, 0))],
        out_specs=pl.BlockSpec((tm, D_MODEL), lambda i: (i, 0)),
        out_shape=jax.ShapeDtypeStruct((N_TOK, D_MODEL), F32),
        compiler_params=_cparams(("arbitrary",)),
        name="final_norm",
    )(x, g.reshape(1, D_MODEL))


def _permute_w_in(w):
    cuts = np.cumsum([D_SSD, SSD_CONV_CH, 2 * SSD_HEADS, D_S5, D_RET, D_RET, D_RET])
    z, xbc, dt, u, rq, rk, rv, rg = jnp.split(w, [int(c) for c in cuts], axis=1)
    pad = jnp.zeros((D_MODEL, CHUNK - SSD_HEADS), w.dtype)
    tail = jnp.zeros((D_MODEL, PROJ_W - COL_DT - 2 * CHUNK), w.dtype)
    out = jnp.concatenate([z, rq, rk, rv, rg, xbc, u, dt[:, :SSD_HEADS], pad, dt[:, SSD_HEADS:], pad, tail], axis=1)
    return out.astype(BF16)


def _pad_lanes(a, width):
    return jnp.pad(a, [(0, 0)] * (a.ndim - 1) + [(0, width - a.shape[-1])])


def _ssd_params(conv_w, conv_b, dt_bias, a_log, d_skip, norm_g):
    return {
        "conv_w": jnp.pad(conv_w, ((0, 8 - SSD_CONV_K), (0, 0))),
        "conv_b": conv_b.reshape(1, SSD_CONV_CH),
        "dt_bias": _pad_lanes(dt_bias, CHUNK).reshape(2, 1, CHUNK),
        "a_exp": _pad_lanes(jnp.exp(a_log), CHUNK).reshape(2, 1, CHUNK),
        "d_skip": jnp.repeat(d_skip, SSD_P).reshape(1, D_SSD),
        "norm_g": norm_g.reshape(1, D_SSD),
    }


def _s5_params(layer, k_re, k_im, p_re, p_im, b_re, b_im, c_re, c_im, d_skip, glu_w, glu_b):
    kr = k_re.reshape(2, S5_GROUPS, S5_STATE, 1)
    ki = k_im.reshape(2, S5_GROUPS, S5_STATE, 1)
    bb_re = kr * b_re - ki * b_im
    bb_im = kr * b_im + ki * b_re
    eye = jnp.eye(S5_GROUPS, dtype=F32)

    def blockdiag_in(bb):
        return jnp.einsum("gh,dgnc->dgchn", eye, bb).reshape(2, D_S5, S5_LANES)

    def blockdiag_out(cc):
        return jnp.einsum("gh,dgcn->dgnhc", eye, cc).reshape(2, S5_LANES, D_S5)

    wb = jnp.concatenate([blockdiag_in(bb_re), blockdiag_in(bb_im)], axis=2).astype(BF16)
    wc = jnp.concatenate([blockdiag_out(c_re), blockdiag_out(-c_im)], axis=1).astype(BF16)
    return {"layer": layer, "wb": wb, "wc": wc, "p_re": p_re, "p_im": p_im,
            "d_skip": d_skip.reshape(1, D_S5), "glu_w": glu_w.astype(BF16), "glu_b": glu_b.reshape(1, D_S5)}


def _ssd_state_in(s):
    b = s.shape[0]
    s = s.reshape(b, 2, SSD_GROUPS, SSD_HPG, SSD_P, SSD_N)
    return jnp.transpose(s, (0, 1, 2, 5, 3, 4)).reshape(b, 2, SSD_GROUPS, SSD_N, SSD_HPG * SSD_P)


def _ssd_state_out(s):
    b = s.shape[0]
    s = s.reshape(b, 2, SSD_GROUPS, SSD_N, SSD_HPG, SSD_P)
    return jnp.transpose(s, (0, 1, 2, 4, 5, 3)).reshape(b, 2, SSD_HEADS, SSD_P, SSD_N)


def kernel(x_prompt, x_sample, c, state_ssd, state_s5_re, state_s5_im, state_ret, c_ctx, ada_w, ada_b, norm1_g, norm2_g, w_in, w_out, ssd_conv_w, ssd_conv_b, ssd_dt_bias, ssd_a_log, ssd_d, ssd_norm_g, s5_lambda_re, s5_lambda_im, s5_log_dt, s5_b_re, s5_b_im, s5_c_re, s5_c_im, s5_d, s5_glu_w, s5_glu_b, ret_gn_g, peer_wq, peer_keys, peer_u, peer_v, final_norm_g):
    nc_ctx = CTX_LEN // CHUNK
    nc_dec = DEC_LEN // CHUNK
    dec_row_off = N_CTX_TOK // CHUNK

    cond = jnp.concatenate([c_ctx[None, :], c, jnp.zeros((N_COND - 1 - N_DEC_SEQ, D_MODEL), F32)], axis=0)
    mod3 = _adaln(cond, ada_w, ada_b).reshape(DEPTH * N_COND * N_MOD, 1, D_MODEL)
    p_re, p_im, k_re, k_im = _s5_prep(s5_lambda_re, s5_lambda_im, s5_log_dt)

    x = jnp.concatenate([x_prompt.reshape(N_CTX_TOK, D_MODEL), x_sample.reshape(N_TOK - N_CTX_TOK, D_MODEL)], axis=0)
    new_ssd, new_re, new_im, new_ret = [], [], [], []
    for l in range(DEPTH):
        proj = _inproj(x, mod3, l, norm1_g[l], _permute_w_in(w_in[l]))

        sp = _ssd_params(ssd_conv_w[l], ssd_conv_b[l], ssd_dt_bias[l], ssd_a_log[l], ssd_d[l], ssd_norm_g[l])
        y_ssd_c, st_ssd = _ssd(proj, sp, N_CTX_SEQ, nc_ctx, 0, None)
        y_ssd_d, _ = _ssd(proj, sp, N_DEC_SEQ, nc_dec, dec_row_off, _ssd_state_in(state_ssd[:, l]))

        s5p = _s5_params(l, k_re[2 * l:2 * l + 2], k_im[2 * l:2 * l + 2], p_re, p_im, s5_b_re[l], s5_b_im[l],
                         s5_c_re[l], s5_c_im[l], s5_d[l], s5_glu_w[l], s5_glu_b[l])
        y_s5_c, st_re, st_im = _s5(proj, s5p, N_CTX_SEQ, nc_ctx, 0, None)
        x0 = (state_s5_re[:, l].reshape(N_DEC_SEQ, 2, 1, S5_LANES), state_s5_im[:, l].reshape(N_DEC_SEQ, 2, 1, S5_LANES))
        y_s5_d, _, _ = _s5(proj, s5p, N_DEC_SEQ, nc_dec, dec_row_off, x0)

        y_ret_c, st_ret = _ret(proj, ret_gn_g[l], N_CTX_SEQ, nc_ctx, 0, None, rope=False)
        y_ret_d, _ = _ret(proj, ret_gn_g[l], N_DEC_SEQ, nc_dec, dec_row_off,
                          jnp.swapaxes(state_ret[:, l], -1, -2), rope=True)

        x = _outproj(x, jnp.concatenate([y_ssd_c, y_ssd_d]), jnp.concatenate([y_s5_c, y_s5_d]),
                     jnp.concatenate([y_ret_c, y_ret_d]), w_out[l].astype(BF16), mod3, l)

        hb, thr, g1, s2, e2 = _peer_scores(x, mod3, l, norm2_g[l], peer_wq[l].astype(BF16), peer_keys[l].astype(BF16))
        x = _peer_experts(x, mod3, l, hb, thr, g1, s2, e2, peer_u[l].astype(BF16), peer_v[l].T.astype(BF16))

        new_ssd.append(_ssd_state_out(st_ssd))
        new_re.append(st_re.reshape(N_CTX_SEQ, 2, S5_GROUPS, S5_STATE))
        new_im.append(st_im.reshape(N_CTX_SEQ, 2, S5_GROUPS, S5_STATE))
        new_ret.append(jnp.swapaxes(st_ret, -1, -2))

    y = _final_norm(x, final_norm_g)
    y_prompt = y[:N_CTX_TOK].reshape(N_CTX_SEQ, CTX_LEN, D_MODEL)
    y_sample = y[N_CTX_TOK:].reshape(N_DEC_SEQ, DEC_LEN, D_MODEL)
    return (y_prompt, y_sample, jnp.stack(new_ssd, axis=1), jnp.stack(new_re, axis=1),
            jnp.stack(new_im, axis=1), jnp.stack(new_ret, axis=1))
```

```python
import functools
import math

import jax
import jax.numpy as jnp
import numpy as np
from jax import lax
from jax.experimental import pallas as pl
from jax.experimental.pallas import tpu as pltpu

F32 = jnp.float32
BF16 = jnp.bfloat16

D_MODEL = 2048
N_CTX_SEQ = 16
CTX_LEN = 256
N_DEC_SEQ = 2
DEC_LEN = 2048
N_CTX_TOK = N_CTX_SEQ * CTX_LEN
N_TOK = N_CTX_TOK + N_DEC_SEQ * DEC_LEN
DEPTH = 2
GRID_W = 64
CHUNK = 128

SSD_HEADS = 12
SSD_P = 64
D_SSD = SSD_HEADS * SSD_P
SSD_N = 128
SSD_GROUPS = 2
SSD_HPG = SSD_HEADS // SSD_GROUPS
SSD_CONV_K = 5
SSD_CONV_CH = D_SSD + 2 * SSD_GROUPS * SSD_N
S5_CH = 16
S5_GROUPS = 32
D_S5 = S5_CH * S5_GROUPS
S5_STATE = 64
S5_LANES = S5_GROUPS * S5_STATE
S5_TILE = 8
RET_HEADS = 6
RET_DK = 128
RET_DV = 128
D_RET = RET_HEADS * RET_DV
ROPE_BASE = 10000.0
PEER_HEADS = 8
PEER_DQ = 256
PEER_NKEYS = 128
PEER_EXPERTS = PEER_NKEYS * PEER_NKEYS
PEER_TOPK = 16
N_MOD = 6
N_COND = 8
EPS = 1e-6

COL_Z, COL_Q, COL_K, COL_V, COL_G = 0, 768, 1536, 2304, 3072
COL_XBC = 3840
COL_U = 5120
COL_DT = 5632
PROJ_W = 6144

VMEM_LIMIT = 56 * 1024 * 1024

NEG_INF = float("-inf")


def _cparams(sem, flags=None):
    return pltpu.CompilerParams(dimension_semantics=sem, vmem_limit_bytes=VMEM_LIMIT, flags=flags)


def _split3(a):
    hi = a.astype(BF16)
    r1 = a - hi.astype(F32)
    mid = r1.astype(BF16)
    lo = (r1 - mid.astype(F32)).astype(BF16)
    return hi, mid, lo


def _dot(a, b):
    return jnp.dot(a, b, preferred_element_type=F32)


def _dot_split_lhs(a, b_exact):
    hi, mid, lo = _split3(a)
    return _dot(hi, b_exact) + _dot(mid, b_exact) + _dot(lo, b_exact)


def _dot_split_rhs(a_exact, b):
    hi, mid, lo = _split3(b)
    return _dot(a_exact, hi) + _dot(a_exact, mid) + _dot(a_exact, lo)


def _dot_nt(a, b):
    return lax.dot_general(a, b, (((1,), (1,)), ((), ())), preferred_element_type=F32)


def _silu(x):
    return x * jax.nn.sigmoid(x)


def _softplus(x):
    return jnp.maximum(x, 0.0) + jnp.log1p(jnp.exp(-jnp.abs(x)))


ADA_TN = 1536


def _adaln_kernel(c_ref, w_ref, b_ref, o_ref):
    s = _silu(c_ref[...]).astype(BF16)
    o_ref[...] = _dot(s, w_ref[...].astype(BF16)) + b_ref[...]


def _adaln(cond, ada_w, ada_b):
    n_out = N_MOD * D_MODEL
    return pl.pallas_call(
        _adaln_kernel,
        grid=(DEPTH, n_out // ADA_TN),
        in_specs=[
            pl.BlockSpec((N_COND, D_MODEL), lambda l, j: (0, 0)),
            pl.BlockSpec((None, D_MODEL, ADA_TN), lambda l, j: (l, 0, j)),
            pl.BlockSpec((None, 1, ADA_TN), lambda l, j: (l, 0, j)),
        ],
        out_specs=pl.BlockSpec((None, N_COND, ADA_TN), lambda l, j: (l, 0, j)),
        out_shape=jax.ShapeDtypeStruct((DEPTH, N_COND, n_out), F32),
        compiler_params=_cparams(("arbitrary", "arbitrary")),
        name="adaln",
    )(cond, ada_w, ada_b.reshape(DEPTH, 1, n_out))


def _mod_spec(layer, which, tm):
    n_ctx_tiles = N_CTX_TOK // tm
    tiles_per_dec = DEC_LEN // tm

    def index(i, *_):
        cond = jnp.where(i < n_ctx_tiles, 0, 1 + (i - n_ctx_tiles) // tiles_per_dec)
        return ((layer * N_COND + cond) * N_MOD + which, 0, 0)

    return pl.BlockSpec((None, 1, D_MODEL), index)


def _rms_modulate(x, g, sc, sh):
    var = jnp.mean(x * x, axis=-1, keepdims=True)
    y = x * lax.rsqrt(var + EPS) * g
    return y * (1.0 + sc) + sh


INPROJ_TM = 512
INPROJ_TN = 2048


def _inproj_kernel(x_ref, sh_ref, sc_ref, g_ref, w_ref, o_ref, h_scr):
    @pl.when(pl.program_id(1) == 0)
    def _():
        h_scr[...] = _rms_modulate(x_ref[...], g_ref[...], sc_ref[...], sh_ref[...]).astype(BF16)

    o_ref[...] = _dot(h_scr[...], w_ref[...])


def _inproj(x, mod3, layer, norm_g, w_perm):
    tm, tn = INPROJ_TM, INPROJ_TN
    return pl.pallas_call(
        _inproj_kernel,
        grid=(N_TOK // tm, PROJ_W // tn),
        in_specs=[
            pl.BlockSpec((tm, D_MODEL), lambda i, j: (i, 0)),
            _mod_spec(layer, 0, tm),
            _mod_spec(layer, 1, tm),
            pl.BlockSpec((1, D_MODEL), lambda i, j: (0, 0)),
            pl.BlockSpec((D_MODEL, tn), lambda i, j: (0, j)),
        ],
        out_specs=pl.BlockSpec((tm, tn), lambda i, j: (i, j)),
        out_shape=jax.ShapeDtypeStruct((N_TOK, PROJ_W), F32),
        scratch_shapes=[pltpu.VMEM((tm, D_MODEL), BF16)],
        compiler_params=_cparams(("arbitrary", "arbitrary")),
        name="inproj",
    )(x, mod3, mod3, norm_g.reshape(1, D_MODEL), w_perm)


def _walk(nc):
    def direction(t):
        return jnp.where(t < nc, 1, 0)

    def chunk(t):
        return jnp.where(t < nc, nc - 1 - t, t - nc)

    return direction, chunk


def _seq_specs(nc, row_off):
    direction, chunk = _walk(nc)

    def rows(s, t):
        return row_off + s * nc + chunk(t)

    def out_rows(s, t):
        return row_off + s * nc + jnp.where(t < nc, 0, t - nc)

    return direction, chunk, rows, out_rows


def _ssd_consts():
    idx = np.arange(CHUNK)
    tri = np.stack([(idx[None, :] <= idx[:, None]), (idx[None, :] >= idx[:, None])]).astype(np.float32)
    e_p = np.zeros((CHUNK, D_SSD), np.float32)
    e_n = np.zeros((CHUNK, SSD_HEADS * CHUNK), np.float32)
    for h in range(SSD_HEADS):
        e_p[h, h * SSD_P:(h + 1) * SSD_P] = 1.0
        e_n[h, h * CHUNK:(h + 1) * CHUNK] = 1.0
    lane = np.arange(CHUNK)
    pair = np.concatenate([np.broadcast_to(lane < SSD_P, (CHUNK, CHUNK)),
                           np.broadcast_to(lane >= SSD_P, (CHUNK, CHUNK))]).astype(np.float32)
    return tri, e_p, e_n, pair


def _ssd_kernel(nc, zero_init, emit_state, *refs):
    (z_ref, xc_ref, xp_ref, xn_ref, dt_ref, cw_ref, cb_ref, dtb_ref, aexp_ref, dsk_ref, ng_ref,
     tri_ref, ep_ref, en_ref, pair_ref) = refs[:15]
    pos = 15
    s0_ref = None
    if not zero_init:
        s0_ref = refs[pos]
        pos += 1
    y_ref = refs[pos]
    pos += 1
    st_ref = None
    if emit_state:
        st_ref = refs[pos]
        pos += 1
    state_scr, stash_scr = refs[pos:]

    t = pl.program_id(1)
    fwd = t >= nc
    step = jnp.where(fwd, t - nc, t)
    chunk = jnp.where(fwd, t - nc, nc - 1 - t)
    row0 = pl.multiple_of(chunk * CHUNK, CHUNK)

    @pl.when(step == 0)
    def _():
        if zero_init:
            state_scr[...] = jnp.zeros_like(state_scr)
        else:
            state_scr[...] = s0_ref[...]

    prev = jnp.where(chunk > 0, xp_ref[...], 0.0)
    nxt = jnp.where(chunk < nc - 1, xn_ref[...], 0.0)
    ext = jnp.concatenate([prev, xc_ref[...], nxt], axis=0)
    conv = cb_ref[...]
    for k in range(SSD_CONV_K):
        off = 8 + k - SSD_CONV_K // 2
        conv = conv + cw_ref[k:k + 1, :] * ext[off:off + CHUNK, :]
    xbc = _silu(conv)
    x = xbc[:, :D_SSD]
    bm = xbc[:, D_SSD:D_SSD + SSD_GROUPS * SSD_N]
    cm = xbc[:, D_SSD + SSD_GROUPS * SSD_N:]

    dt = _softplus(dt_ref[...] + dtb_ref[...])
    la = -dt * aexp_ref[...]
    tri = tri_ref[...]
    cum = _dot_split_rhs(tri.astype(BF16), la)
    ep = ep_ref[...]
    cum_p = _dot_split_lhs(cum, ep)
    dt_p = _dot_split_lhs(dt, ep)
    cum_col = _dot_split_lhs(cum, en_ref[...])
    cum_t = cum.T
    tot_p = jnp.where(fwd, cum_p[CHUNK - 1:CHUNK, :], cum_p[0:1, :])

    v = x * dt_p
    vb = v.astype(BF16)
    mask = tri > 0.5
    pair = pair_ref[...]
    y_parts = []
    for g in range(SSD_GROUPS):
        cg = cm[:, g * SSD_N:(g + 1) * SSD_N].astype(BF16)
        bg = bm[:, g * SSD_N:(g + 1) * SSD_N]
        gmat = _dot_nt(cg, bg.astype(BF16))
        for hp in range(SSD_HPG // 2):
            scs = []
            for h in (g * SSD_HPG + 2 * hp, g * SSD_HPG + 2 * hp + 1):
                ci = cum_col[:, h * CHUNK:(h + 1) * CHUNK]
                cj = cum_t[h:h + 1, :]
                dec = jnp.exp(jnp.where(mask, ci - cj, NEG_INF))
                scs.append((gmat * dec).astype(BF16))
            c0 = (g * SSD_HPG + 2 * hp) * SSD_P
            v2 = vb[:, c0:c0 + 2 * SSD_P]
            vv = jnp.concatenate([v2, v2], axis=0) * pair
            y_parts.append(_dot(jnp.concatenate(scs, axis=1), vv))
    y = jnp.concatenate(y_parts, axis=1)

    w_p = SSD_HPG * SSD_P
    y_off = jnp.concatenate(
        [_dot(cm[:, g * SSD_N:(g + 1) * SSD_N].astype(BF16), state_scr[g].astype(BF16))
         for g in range(SSD_GROUPS)], axis=1)
    y = y + y_off * jnp.exp(cum_p)

    vw = (v * jnp.exp(tot_p - cum_p)).astype(BF16)
    cdec = jnp.exp(tot_p)
    for g in range(SSD_GROUPS):
        bt = bm[:, g * SSD_N:(g + 1) * SSD_N].T.astype(BF16)
        state_scr[g] = state_scr[g] * cdec[:, g * w_p:(g + 1) * w_p] + _dot(bt, vw[:, g * w_p:(g + 1) * w_p])

    @pl.when(jnp.logical_not(fwd))
    def _():
        stash_scr[pl.ds(row0, CHUNK), :] = y

    @pl.when(fwd)
    def _():
        ytot = y + stash_scr[pl.ds(row0, CHUNK), :] + x * dsk_ref[...]
        gated = ytot * _silu(z_ref[...])
        var = jnp.mean(gated * gated, axis=-1, keepdims=True)
        y_ref[...] = (gated * lax.rsqrt(var + EPS) * ng_ref[...]).astype(y_ref.dtype)

    if emit_state:
        @pl.when(step == nc - 1)
        def _():
            st_ref[...] = state_scr[...]


def _ssd(proj, p, n_seq, nc, row_off, s0):
    zero_init = s0 is None
    emit_state = zero_init
    direction, chunk, rows, out_rows = _seq_specs(nc, row_off)
    tri, e_p, e_n, pair = _ssd_consts()
    n8 = N_TOK // 8
    w_st = SSD_HPG * SSD_P
    in_specs = [
        pl.BlockSpec((CHUNK, D_SSD), lambda s, t: (rows(s, t), COL_Z // D_SSD)),
        pl.BlockSpec((CHUNK, SSD_CONV_CH), lambda s, t: (rows(s, t), COL_XBC // SSD_CONV_CH)),
        pl.BlockSpec((8, SSD_CONV_CH),
                     lambda s, t: (jnp.maximum(rows(s, t) * (CHUNK // 8) - 1, 0), COL_XBC // SSD_CONV_CH)),
        pl.BlockSpec((8, SSD_CONV_CH),
                     lambda s, t: (jnp.minimum((rows(s, t) + 1) * (CHUNK // 8), n8 - 1), COL_XBC // SSD_CONV_CH)),
        pl.BlockSpec((CHUNK, CHUNK), lambda s, t: (rows(s, t), COL_DT // CHUNK + direction(t))),
        pl.BlockSpec((8, SSD_CONV_CH), lambda s, t: (0, 0)),
        pl.BlockSpec((1, SSD_CONV_CH), lambda s, t: (0, 0)),
        pl.BlockSpec((None, 1, CHUNK), lambda s, t: (direction(t), 0, 0)),
        pl.BlockSpec((None, 1, CHUNK), lambda s, t: (direction(t), 0, 0)),
        pl.BlockSpec((1, D_SSD), lambda s, t: (0, 0)),
        pl.BlockSpec((1, D_SSD), lambda s, t: (0, 0)),
        pl.BlockSpec((None, CHUNK, CHUNK), lambda s, t: (direction(t), 0, 0)),
        pl.BlockSpec((CHUNK, D_SSD), lambda s, t: (0, 0)),
        pl.BlockSpec((CHUNK, SSD_HEADS * CHUNK), lambda s, t: (0, 0)),
        pl.BlockSpec((2 * CHUNK, CHUNK), lambda s, t: (0, 0)),
    ]
    args = [proj, proj, proj, proj, proj, p["conv_w"], p["conv_b"], p["dt_bias"], p["a_exp"], p["d_skip"],
            p["norm_g"], jnp.asarray(tri), jnp.asarray(e_p, BF16), jnp.asarray(e_n, BF16), jnp.asarray(pair, BF16)]
    st_spec = pl.BlockSpec((None, None, SSD_GROUPS, SSD_N, w_st), lambda s, t: (s, direction(t), 0, 0, 0))
    if not zero_init:
        in_specs.append(st_spec)
        args.append(s0)
    out_specs = [pl.BlockSpec((CHUNK, D_SSD), lambda s, t: (out_rows(s, t) - row_off, 0))]
    out_shape = [jax.ShapeDtypeStruct((n_seq * nc * CHUNK, D_SSD), BF16)]
    if emit_state:
        out_specs.append(st_spec)
        out_shape.append(jax.ShapeDtypeStruct((n_seq, 2, SSD_GROUPS, SSD_N, w_st), F32))
    res = pl.pallas_call(
        functools.partial(_ssd_kernel, nc, zero_init, emit_state),
        grid=(n_seq, 2 * nc),
        in_specs=in_specs,
        out_specs=out_specs,
        out_shape=out_shape,
        scratch_shapes=[pltpu.VMEM((SSD_GROUPS, SSD_N, w_st), F32), pltpu.VMEM((nc * CHUNK, D_SSD), F32)],
        compiler_params=_cparams(("arbitrary", "arbitrary")),
        name="ssd_ctx" if zero_init else "ssd_dec",
    )(*args)
    return res if emit_state else (res[0], None)


def _s5_prep_kernel(lr_ref, li_ref, ldt_ref, pre_ref, pim_ref, kre_ref, kim_ref):
    lr = lr_ref[...]
    li = li_ref[...]
    dt = jnp.exp(ldt_ref[...])
    kk = (lax.broadcasted_iota(jnp.int32, (S5_TILE, S5_LANES), 0) + 1).astype(F32)
    mag = jnp.exp(kk * (lr * dt))
    ang = kk * (li * dt)
    p_re = mag * jnp.cos(ang)
    p_im = mag * jnp.sin(ang)
    pre_ref[...] = p_re
    pim_ref[...] = p_im
    a_re = p_re[0:1, :]
    a_im = p_im[0:1, :]
    den = lr * lr + li * li
    num_re = a_re - 1.0
    kre_ref[...] = (num_re * lr + a_im * li) / den
    kim_ref[...] = (a_im * lr - num_re * li) / den


def _s5_prep(lam_re, lam_im, log_dt):
    n = DEPTH * 2
    lr = lam_re.reshape(n, 1, S5_LANES)
    li = lam_im.reshape(n, 1, S5_LANES)
    ldt = jnp.broadcast_to(log_dt[..., None], (DEPTH, 2, S5_GROUPS, S5_STATE)).reshape(n, 1, S5_LANES)
    row = pl.BlockSpec((None, 1, S5_LANES), lambda i: (i, 0, 0))
    tab = pl.BlockSpec((None, S5_TILE, S5_LANES), lambda i: (i, 0, 0))
    return pl.pallas_call(
        _s5_prep_kernel,
        grid=(n,),
        in_specs=[row, row, row],
        out_specs=[tab, tab, row, row],
        out_shape=[jax.ShapeDtypeStruct((n, S5_TILE, S5_LANES), F32)] * 2
        + [jax.ShapeDtypeStruct((n, 1, S5_LANES), F32)] * 2,
        compiler_params=_cparams(("arbitrary",)),
        name="s5_prep",
    )(lr, li, ldt)


def _s5_kernel(nc, zero_init, emit_state, *refs):
    (u_ref, jm_ref, wb_ref, pre_ref, pim_ref, wc_ref, dsk_ref, gw_ref, gb_ref) = refs[:9]
    pos = 9
    x0re_ref = x0im_ref = None
    if not zero_init:
        x0re_ref, x0im_ref = refs[pos:pos + 2]
        pos += 2
    y_ref = refs[pos]
    pos += 1
    sre_ref = sim_ref = None
    if emit_state:
        sre_ref, sim_ref = refs[pos:pos + 2]
        pos += 2
    cre_scr, cim_scr, stash_scr = refs[pos:]

    t = pl.program_id(1)
    fwd = t >= nc
    step = jnp.where(fwd, t - nc, t)
    chunk = jnp.where(fwd, t - nc, nc - 1 - t)
    row0 = pl.multiple_of(chunk * CHUNK, CHUNK)

    @pl.when(step == 0)
    def _():
        if zero_init:
            cre_scr[...] = jnp.zeros_like(cre_scr)
            cim_scr[...] = jnp.zeros_like(cim_scr)
        else:
            cre_scr[...] = x0re_ref[...]
            cim_scr[...] = x0im_ref[...]

    u = u_ref[...]
    jm = jm_ref[...]
    us = _dot(jm, u.astype(BF16)).astype(BF16)
    bu = _dot(us, wb_ref[...])
    xr = bu[:, :S5_LANES]
    xi = bu[:, S5_LANES:]
    sub = lax.broadcasted_iota(jnp.int32, (CHUNK, S5_LANES), 0) % S5_TILE
    d = 1
    while d < S5_TILE:
        ar = pre_ref[d - 1:d, :]
        ai = pim_ref[d - 1:d, :]
        keep = sub >= d
        sr = jnp.where(keep, pltpu.roll(xr, d, axis=0), 0.0)
        si = jnp.where(keep, pltpu.roll(xi, d, axis=0), 0.0)
        xr, xi = xr + (ar * sr - ai * si), xi + (ar * si + ai * sr)
        d *= 2
    cr = cre_scr[...]
    ci = cim_scr[...]
    pr = pre_ref[...]
    pi = pim_ref[...]
    tiles_r, tiles_i = [], []
    for b in range(CHUNK // S5_TILE):
        br = xr[b * S5_TILE:(b + 1) * S5_TILE, :]
        bi = xi[b * S5_TILE:(b + 1) * S5_TILE, :]
        br, bi = br + (pr * cr - pi * ci), bi + (pr * ci + pi * cr)
        cr = br[S5_TILE - 1:S5_TILE, :]
        ci = bi[S5_TILE - 1:S5_TILE, :]
        tiles_r.append(br)
        tiles_i.append(bi)
    xr = jnp.concatenate(tiles_r, axis=0)
    xi = jnp.concatenate(tiles_i, axis=0)
    cre_scr[...] = cr
    cim_scr[...] = ci

    ys = _dot(xr.astype(BF16), wc_ref[:S5_LANES, :]) + _dot(xi.astype(BF16), wc_ref[S5_LANES:, :])
    yh = ys.astype(BF16)
    yl = (ys - yh.astype(F32)).astype(BF16)
    y = _dot(jm, yh) + _dot(jm, yl)

    @pl.when(jnp.logical_not(fwd))
    def _():
        stash_scr[pl.ds(row0, CHUNK), :] = y

    @pl.when(fwd)
    def _():
        yt = y + stash_scr[pl.ds(row0, CHUNK), :] + u * dsk_ref[...]
        yt = jax.nn.gelu(yt)
        gate = jax.nn.sigmoid(_dot(yt.astype(BF16), gw_ref[...]) + gb_ref[...])
        y_ref[...] = (yt * gate).astype(y_ref.dtype)

    if emit_state:
        @pl.when(step == nc - 1)
        def _():
            sre_ref[...] = cre_scr[...]
            sim_ref[...] = cim_scr[...]


def _s5(proj, p, n_seq, nc, row_off, x0):
    zero_init = x0 is None
    emit_state = zero_init
    direction, chunk, rows, out_rows = _seq_specs(nc, row_off)
    eye = np.eye(CHUNK, dtype=np.float32)
    jm = jnp.asarray(np.stack([eye, eye[::-1]]), BF16)
    lyr = p["layer"]

    def dsel(t):
        return lyr * 2 + direction(t)

    in_specs = [
        pl.BlockSpec((CHUNK, D_S5), lambda s, t: (rows(s, t), COL_U // D_S5)),
        pl.BlockSpec((None, CHUNK, CHUNK), lambda s, t: (direction(t), 0, 0)),
        pl.BlockSpec((None, D_S5, 2 * S5_LANES), lambda s, t: (direction(t), 0, 0)),
        pl.BlockSpec((None, S5_TILE, S5_LANES), lambda s, t: (dsel(t), 0, 0)),
        pl.BlockSpec((None, S5_TILE, S5_LANES), lambda s, t: (dsel(t), 0, 0)),
        pl.BlockSpec((None, 2 * S5_LANES, D_S5), lambda s, t: (direction(t), 0, 0)),
        pl.BlockSpec((1, D_S5), lambda s, t: (0, 0)),
        pl.BlockSpec((D_S5, D_S5), lambda s, t: (0, 0)),
        pl.BlockSpec((1, D_S5), lambda s, t: (0, 0)),
    ]
    args = [proj, jm, p["wb"], p["p_re"], p["p_im"], p["wc"], p["d_skip"], p["glu_w"], p["glu_b"]]
    st_spec = pl.BlockSpec((None, None, 1, S5_LANES), lambda s, t: (s, direction(t), 0, 0))
    if not zero_init:
        in_specs += [st_spec, st_spec]
        args += [x0[0], x0[1]]
    out_specs = [pl.BlockSpec((CHUNK, D_S5), lambda s, t: (out_rows(s, t) - row_off, 0))]
    out_shape = [jax.ShapeDtypeStruct((n_seq * nc * CHUNK, D_S5), BF16)]
    if emit_state:
        out_specs += [st_spec, st_spec]
        out_shape += [jax.ShapeDtypeStruct((n_seq, 2, 1, S5_LANES), F32)] * 2
    res = pl.pallas_call(
        functools.partial(_s5_kernel, nc, zero_init, emit_state),
        grid=(n_seq, 2 * nc),
        in_specs=in_specs,
        out_specs=out_specs,
        out_shape=out_shape,
        scratch_shapes=[pltpu.VMEM((1, S5_LANES), F32), pltpu.VMEM((1, S5_LANES), F32),
                        pltpu.VMEM((nc * CHUNK, D_S5), F32)],
        compiler_params=_cparams(("arbitrary", "arbitrary")),
        name="s5_ctx" if zero_init else "s5_dec",
    )(*args)
    return res if emit_state else (res[0], None, None)


def _ret_consts():
    heads = np.arange(RET_HEADS, dtype=np.float64)
    lg = np.stack([np.log1p(-np.exp2(-5.0 - heads)), np.log1p(-np.exp2(-5.5 - heads))])
    i = np.arange(CHUNK, dtype=np.float64)
    diff = i[:, None] - i[None, :]
    dmat = np.zeros((2, RET_HEADS, CHUNK, CHUNK))
    rowdec = np.zeros((2, CHUNK, D_RET))
    wend = np.zeros((2, 8, CHUNK))
    cdec = np.zeros((2, 8, CHUNK))
    for h in range(RET_HEADS):
        dmat[0, h] = np.where(diff >= 0, np.exp(lg[0, h] * diff), 0.0)
        dmat[1, h] = np.where(diff <= 0, np.exp(-lg[1, h] * diff), 0.0)
        rowdec[0, :, h * RET_DV:(h + 1) * RET_DV] = np.exp(lg[0, h] * (i + 1))[:, None]
        rowdec[1, :, h * RET_DV:(h + 1) * RET_DV] = np.exp(lg[1, h] * (CHUNK - i))[:, None]
        wend[0, h] = np.exp(lg[0, h] * (CHUNK - 1 - i))
        wend[1, h] = np.exp(lg[1, h] * i)
        cdec[:, h] = np.exp(lg[:, h] * CHUNK)[:, None]
    return [jnp.asarray(a, F32) for a in (dmat, rowdec, wend, cdec)]


def _rope_tables():
    t = np.arange(DEC_LEN)
    row = (t // GRID_W).astype(np.float32)
    col = (t % GRID_W).astype(np.float32)
    quarter = RET_DK // 4
    freqs = (ROPE_BASE ** (-np.arange(quarter, dtype=np.float32) / quarter)).astype(np.float32)
    ar = (row[:, None] * freqs[None, :]).astype(np.float64)
    ac = (col[:, None] * freqs[None, :]).astype(np.float64)
    cos = np.concatenate([np.cos(ar), np.cos(ar), np.cos(ac), np.cos(ac)], axis=1)
    sin = np.concatenate([-np.sin(ar), np.sin(ar), -np.sin(ac), np.sin(ac)], axis=1)
    return jnp.asarray(cos, F32), jnp.asarray(sin, F32)


def _ret_kernel(nc, zero_init, emit_state, rope, *refs):
    q_ref, k_ref, v_ref, g_ref = refs[:4]
    pos = 4
    cos_ref = sin_ref = None
    if rope:
        cos_ref, sin_ref = refs[pos:pos + 2]
        pos += 2
    dmat_ref, rowdec_ref, wend_ref, cdec_ref, gn_ref = refs[pos:pos + 5]
    pos += 5
    s0_ref = None
    if not zero_init:
        s0_ref = refs[pos]
        pos += 1
    y_ref = refs[pos]
    pos += 1
    st_ref = None
    if emit_state:
        st_ref = refs[pos]
        pos += 1
    state_scr, stash_scr = refs[pos:]

    t = pl.program_id(1)
    fwd = t >= nc
    step = jnp.where(fwd, t - nc, t)
    chunk = jnp.where(fwd, t - nc, nc - 1 - t)
    row0 = pl.multiple_of(chunk * CHUNK, CHUNK)

    @pl.when(step == 0)
    def _():
        if zero_init:
            state_scr[...] = jnp.zeros_like(state_scr)
        else:
            state_scr[...] = s0_ref[...]

    q = q_ref[...]
    k = k_ref[...]
    if rope:
        cos = jnp.concatenate([cos_ref[...]] * RET_HEADS, axis=1)
        sin = jnp.concatenate([sin_ref[...]] * RET_HEADS, axis=1)
        lane = lax.broadcasted_iota(jnp.int32, (CHUNK, D_RET), 1)
        first = (lane // (RET_DK // 4)) % 2 == 0

        def rot(x):
            partner = jnp.where(first, pltpu.roll(x, D_RET - RET_DK // 4, axis=1), pltpu.roll(x, RET_DK // 4, axis=1))
            return x * cos + partner * sin

        q = rot(q)
        k = rot(k)
    k = k * (RET_DK ** -0.5)
    qb = q.astype(BF16)
    vb = v_ref[...].astype(BF16)
    rowdec = rowdec_ref[...]
    y_parts = []
    for h in range(RET_HEADS):
        sl = slice(h * RET_DK, (h + 1) * RET_DK)
        kh = k[:, sl]
        sc = _dot_nt(qb[:, sl], kh.astype(BF16)) * dmat_ref[h]
        yh = _dot(sc.astype(BF16), vb[:, sl])
        yh = yh + _dot(qb[:, sl], state_scr[h].astype(BF16)) * rowdec[:, sl]
        y_parts.append(yh)
        kt = (kh.T * wend_ref[h:h + 1, :]).astype(BF16)
        state_scr[h] = state_scr[h] * cdec_ref[h:h + 1, :] + _dot(kt, vb[:, sl])
    y = jnp.concatenate(y_parts, axis=1)

    @pl.when(jnp.logical_not(fwd))
    def _():
        stash_scr[pl.ds(row0, CHUNK), :] = y

    @pl.when(fwd)
    def _():
        yt = y + stash_scr[pl.ds(row0, CHUNK), :]
        outs = []
        for h in range(RET_HEADS):
            yh = yt[:, h * RET_DV:(h + 1) * RET_DV]
            yc = yh - jnp.mean(yh, axis=-1, keepdims=True)
            outs.append(yc * lax.rsqrt(jnp.mean(yc * yc, axis=-1, keepdims=True) + EPS))
        yn = jnp.concatenate(outs, axis=1)
        y_ref[...] = (yn * gn_ref[...] * _silu(g_ref[...])).astype(y_ref.dtype)

    if emit_state:
        @pl.when(step == nc - 1)
        def _():
            st_ref[...] = state_scr[...]


def _ret(proj, gn_g, n_seq, nc, row_off, s0, rope):
    zero_init = s0 is None
    emit_state = zero_init
    direction, chunk, rows, out_rows = _seq_specs(nc, row_off)
    dmat, rowdec, wend, cdec = _ret_consts()

    def col(c):
        return pl.BlockSpec((CHUNK, D_RET), lambda s, t: (rows(s, t), c // D_RET))

    in_specs = [col(COL_Q), col(COL_K), col(COL_V), col(COL_G)]
    args = [proj, proj, proj, proj]
    if rope:
        cos, sin = _rope_tables()
        tab = pl.BlockSpec((CHUNK, RET_DK), lambda s, t: (chunk(t), 0))
        in_specs += [tab, tab]
        args += [cos, sin]
    in_specs += [
        pl.BlockSpec((None, RET_HEADS, CHUNK, CHUNK), lambda s, t: (direction(t), 0, 0, 0)),
        pl.BlockSpec((None, CHUNK, D_RET), lambda s, t: (direction(t), 0, 0)),
        pl.BlockSpec((None, 8, CHUNK), lambda s, t: (direction(t), 0, 0)),
        pl.BlockSpec((None, 8, CHUNK), lambda s, t: (direction(t), 0, 0)),
        pl.BlockSpec((1, D_RET), lambda s, t: (0, 0)),
    ]
    args += [dmat, rowdec, wend, cdec, gn_g.reshape(1, D_RET)]
    st_spec = pl.BlockSpec((None, None, RET_HEADS, RET_DK, RET_DV), lambda s, t: (s, direction(t), 0, 0, 0))
    if not zero_init:
        in_specs.append(st_spec)
        args.append(s0)
    out_specs = [pl.BlockSpec((CHUNK, D_RET), lambda s, t: (out_rows(s, t) - row_off, 0))]
    out_shape = [jax.ShapeDtypeStruct((n_seq * nc * CHUNK, D_RET), BF16)]
    if emit_state:
        out_specs.append(st_spec)
        out_shape.append(jax.ShapeDtypeStruct((n_seq, 2, RET_HEADS, RET_DK, RET_DV), F32))
    res = pl.pallas_call(
        functools.partial(_ret_kernel, nc, zero_init, emit_state, rope),
        grid=(n_seq, 2 * nc),
        in_specs=in_specs,
        out_specs=out_specs,
        out_shape=out_shape,
        scratch_shapes=[pltpu.VMEM((RET_HEADS, RET_DK, RET_DV), F32), pltpu.VMEM((nc * CHUNK, D_RET), F32)],
        compiler_params=_cparams(("arbitrary", "arbitrary")),
        name="ret_ctx" if zero_init else "ret_dec",
    )(*args)
    return res if emit_state else (res[0], None)


OUTPROJ_TM = 512


def _outproj_kernel(x_ref, ya_ref, yb_ref, yc_ref, w_ref, g_ref, o_ref):
    y = _dot(ya_ref[...], w_ref[:D_SSD, :])
    y = y + _dot(yb_ref[...], w_ref[D_SSD:D_SSD + D_S5, :])
    y = y + _dot(yc_ref[...], w_ref[D_SSD + D_S5:, :])
    o_ref[...] = x_ref[...] + g_ref[...] * y


def _outproj(x, y_ssd, y_s5, y_ret, w_out, mod3, layer):
    tm = OUTPROJ_TM
    return pl.pallas_call(
        _outproj_kernel,
        grid=(N_TOK // tm,),
        in_specs=[
            pl.BlockSpec((tm, D_MODEL), lambda i: (i, 0)),
            pl.BlockSpec((tm, D_SSD), lambda i: (i, 0)),
            pl.BlockSpec((tm, D_S5), lambda i: (i, 0)),
            pl.BlockSpec((tm, D_RET), lambda i: (i, 0)),
            pl.BlockSpec((D_MODEL, D_MODEL), lambda i: (0, 0)),
            _mod_spec(layer, 2, tm),
        ],
        out_specs=pl.BlockSpec((tm, D_MODEL), lambda i: (i, 0)),
        out_shape=jax.ShapeDtypeStruct((N_TOK, D_MODEL), F32),
        compiler_params=_cparams(("arbitrary",)),
        name="outproj",
    )(x, y_ssd, y_s5, y_ret, w_out, mod3)


PEER_SC_TM = 256
PEER_NCAND = PEER_TOPK + 1


def _cand_pairs():
    return [(i, j) for i in range(PEER_NCAND) for j in range(PEER_NCAND) if (i + 1) * (j + 1) <= PEER_NCAND]


def _top_rows(work, n):
    rows = []
    for r in range(n):
        m = jnp.max(work, axis=0, keepdims=True)
        rows.append(m)
        if r < n - 1:
            work = jnp.where(work >= m, NEG_INF, work)
    return rows


def _peer_scores_kernel(x_ref, sh_ref, sc_ref, g_ref, wq_ref, keys_ref,
                        h_ref, thr_ref, g1_ref, s2_ref, e2_ref, cand_scr):
    tm = x_ref.shape[0]
    hb = _rms_modulate(x_ref[...], g_ref[...], sc_ref[...], sh_ref[...]).astype(BF16)
    h_ref[...] = hb
    q = _dot(hb, wq_ref[...]).astype(BF16)
    half = PEER_DQ // 2
    pairs = _cand_pairs()
    n_rows = cand_scr.shape[0]
    cand_scr[len(pairs):, :] = jnp.full((n_rows - len(pairs), tm), NEG_INF, F32)
    for h in range(PEER_HEADS):
        q1 = q[:, h * PEER_DQ:h * PEER_DQ + half]
        q2 = q[:, h * PEER_DQ + half:(h + 1) * PEER_DQ]
        s1 = _dot_nt(keys_ref[h, 0], q1)
        s2 = _dot_nt(keys_ref[h, 1], q2)
        a = _top_rows(s1, PEER_NCAND)
        b = _top_rows(s2, PEER_NCAND)
        for r, (i, j) in enumerate(pairs):
            cand_scr[r:r + 1, :] = a[i] + b[j]
        c = _top_rows(cand_scr[...], PEER_NCAND)
        top = a[0] + b[0]
        zsum = jnp.zeros_like(top)
        for r in range(PEER_TOPK):
            zsum = zsum + jnp.exp(c[r] - top)
        tau = 0.5 * (c[PEER_TOPK - 1] + c[PEER_TOPK])
        thr_ref[h] = tau - s1
        g1_ref[h] = jnp.exp(s1 - a[0]) * (0.5 / zsum)
        s2_ref[h] = s2
        e2_ref[h] = jnp.exp(s2 - b[0])


def _peer_scores(x, mod3, layer, norm_g, wq, keys):
    tm = PEER_SC_TM
    n_tok = x.shape[0]
    sc_spec = pl.BlockSpec((PEER_HEADS, PEER_NKEYS, tm), lambda i: (0, 0, i))
    sc_shape = jax.ShapeDtypeStruct((PEER_HEADS, PEER_NKEYS, n_tok), F32)
    n_cand_rows = -(-len(_cand_pairs()) // 8) * 8
    return pl.pallas_call(
        _peer_scores_kernel,
        grid=(n_tok // tm,),
        in_specs=[
            pl.BlockSpec((tm, D_MODEL), lambda i: (i, 0)),
            _mod_spec(layer, 3, tm),
            _mod_spec(layer, 4, tm),
            pl.BlockSpec((1, D_MODEL), lambda i: (0, 0)),
            pl.BlockSpec((D_MODEL, PEER_HEADS * PEER_DQ), lambda i: (0, 0)),
            pl.BlockSpec((PEER_HEADS, 2, PEER_NKEYS, PEER_DQ // 2), lambda i: (0, 0, 0, 0)),
        ],
        out_specs=[pl.BlockSpec((tm, D_MODEL), lambda i: (i, 0)), sc_spec, sc_spec, sc_spec, sc_spec],
        out_shape=[jax.ShapeDtypeStruct((n_tok, D_MODEL), BF16), sc_shape, sc_shape, sc_shape, sc_shape],
        scratch_shapes=[pltpu.VMEM((n_cand_rows, tm), F32)],
        compiler_params=_cparams(("arbitrary",)),
        name="peer_scores",
    )(x, mod3, mod3, norm_g.reshape(1, D_MODEL), wq, keys)


PEER_TM = 512
PEER_EBLK = 1024
GELU_C = math.sqrt(2.0 / math.pi)


def _peer_experts_kernel(x_ref, g2_ref, h_ref, thr_ref, g1_ref, s2_ref, e2_ref, u_ref, vt_ref,
                         o_ref, acc_scr, act_scr, w_scr, gl_scr):
    tm = x_ref.shape[0]
    eblk = u_ref.shape[0]
    j = pl.program_id(1)

    @pl.when(j == 0)
    def _():
        acc_scr[...] = jnp.zeros_like(acc_scr)

    nb = eblk // PEER_NKEYS
    rsub = 16
    na = 2

    def region(c):
        rows_c = slice(c * na * PEER_NKEYS, (c + 1) * na * PEER_NKEYS)
        act_scr[rows_c, :] = _dot_nt(u_ref[rows_c, :], h_ref[...])
        for lg in range(tm // 128):
            sl = slice(lg * 128, (lg + 1) * 128)
            for r0 in range(0, PEER_NKEYS, rsub):
                w = [jnp.zeros((rsub, 128), F32) for _ in range(na)]
                for h in range(PEER_HEADS):
                    s2 = s2_ref[h, r0:r0 + rsub, sl]
                    e2 = e2_ref[h, r0:r0 + rsub, sl]
                    for k in range(na):
                        a = c * na + k
                        w[k] = w[k] + jnp.where(s2 >= thr_ref[h, a:a + 1, sl], e2, 0.0) * g1_ref[h, a:a + 1, sl]
                for k in range(na):
                    r = (c * na + k) * PEER_NKEYS + r0
                    w_scr[r:r + rsub, sl] = w[k]

    for c in range(nb // na):
        pl.when(j + c < pl.num_programs(1) + c)(functools.partial(region, c))

    @pl.when(j + nb < pl.num_programs(1) + nb)
    def _():
        act = act_scr[...]
        inner = act * (GELU_C + (GELU_C * 0.044715) * (act * act))
        gl_scr[...] = ((act * w_scr[...]) * (1.0 + jnp.tanh(inner))).astype(BF16)

    acc_scr[...] += _dot(vt_ref[...], gl_scr[...])

    @pl.when(j == pl.num_programs(1) - 1)
    def _():
        o_ref[...] = x_ref[...] + g2_ref[...] * acc_scr[...].T


def _peer_experts(x, mod3, layer, hb, thr, g1, s2, e2, u_bf, vt_bf):
    tm, eblk = PEER_TM, PEER_EBLK
    n_tok = x.shape[0]
    once = pl.Buffered(1)
    sc_spec = pl.BlockSpec((PEER_HEADS, PEER_NKEYS, tm), lambda i, j: (0, 0, i), pipeline_mode=once)
    k1_spec = pl.BlockSpec((PEER_HEADS, eblk // PEER_NKEYS, tm), lambda i, j: (0, j, i))
    return pl.pallas_call(
        _peer_experts_kernel,
        grid=(n_tok // tm, PEER_EXPERTS // eblk),
        in_specs=[
            pl.BlockSpec((tm, D_MODEL), lambda i, j: (i, 0), pipeline_mode=once),
            _mod_spec(layer, 5, tm),
            pl.BlockSpec((tm, D_MODEL), lambda i, j: (i, 0), pipeline_mode=once),
            k1_spec, k1_spec, sc_spec, sc_spec,
            pl.BlockSpec((eblk, D_MODEL), lambda i, j: (j, 0)),
            pl.BlockSpec((D_MODEL, eblk), lambda i, j: (0, j)),
        ],
        out_specs=pl.BlockSpec((tm, D_MODEL), lambda i, j: (i, 0)),
        out_shape=jax.ShapeDtypeStruct((n_tok, D_MODEL), F32),
        scratch_shapes=[pltpu.VMEM((D_MODEL, tm), F32), pltpu.VMEM((eblk, tm), F32), pltpu.VMEM((eblk, tm), F32),
                        pltpu.VMEM((eblk, tm), BF16)],
        compiler_params=_cparams(("arbitrary", "arbitrary")),
        name="peer_experts",
    )(x, mod3, hb, thr, g1, s2, e2, u_bf, vt_bf)


FINAL_TM = 512


def _final_norm_kernel(x_ref, g_ref, o_ref):
    x = x_ref[...]
    var = jnp.mean(x * x, axis=-1, keepdims=True)
    o_ref[...] = x * lax.rsqrt(var + EPS) * g_ref[...]


def _final_norm(x, g):
    tm = FINAL_TM
    return pl.pallas_call(
        _final_norm_kernel,
        grid=(N_TOK // tm,),
        in_specs=[pl.BlockSpec((tm, D_MODEL), lambda i: (i, 0)), pl.BlockSpec((1, D_MODEL), lambda i: (0, 0))],
        out_specs=pl.BlockSpec((tm, D_MODEL), lambda i: (i, 0)),
        out_shape=jax.ShapeDtypeStruct((N_TOK, D_MODEL), F32),
        compiler_params=_cparams(("arbitrary",)),
        name="final_norm",
    )(x, g.reshape(1, D_MODEL))


def _permute_w_in(w):
    cuts = np.cumsum([D_SSD, SSD_CONV_CH, 2 * SSD_HEADS, D_S5, D_RET, D_RET, D_RET])
    z, xbc, dt, u, rq, rk, rv, rg = jnp.split(w, [int(c) for c in cuts], axis=1)
    pad = jnp.zeros((D_MODEL, CHUNK - SSD_HEADS), w.dtype)
    tail = jnp.zeros((D_MODEL, PROJ_W - COL_DT - 2 * CHUNK), w.dtype)
    out = jnp.concatenate([z, rq, rk, rv, rg, xbc, u, dt[:, :SSD_HEADS], pad, dt[:, SSD_HEADS:], pad, tail], axis=1)
    return out.astype(BF16)


def _pad_lanes(a, width):
    return jnp.pad(a, [(0, 0)] * (a.ndim - 1) + [(0, width - a.shape[-1])])


def _ssd_params(conv_w, conv_b, dt_bias, a_log, d_skip, norm_g):
    return {
        "conv_w": jnp.pad(conv_w, ((0, 8 - SSD_CONV_K), (0, 0))),
        "conv_b": conv_b.reshape(1, SSD_CONV_CH),
        "dt_bias": _pad_lanes(dt_bias, CHUNK).reshape(2, 1, CHUNK),
        "a_exp": _pad_lanes(jnp.exp(a_log), CHUNK).reshape(2, 1, CHUNK),
        "d_skip": jnp.repeat(d_skip, SSD_P).reshape(1, D_SSD),
        "norm_g": norm_g.reshape(1, D_SSD),
    }


def _s5_params(layer, k_re, k_im, p_re, p_im, b_re, b_im, c_re, c_im, d_skip, glu_w, glu_b):
    kr = k_re.reshape(2, S5_GROUPS, S5_STATE, 1)
    ki = k_im.reshape(2, S5_GROUPS, S5_STATE, 1)
    bb_re = kr * b_re - ki * b_im
    bb_im = kr * b_im + ki * b_re
    eye = jnp.eye(S5_GROUPS, dtype=F32)

    def blockdiag_in(bb):
        return jnp.einsum("gh,dgnc->dgchn", eye, bb).reshape(2, D_S5, S5_LANES)

    def blockdiag_out(cc):
        return jnp.einsum("gh,dgcn->dgnhc", eye, cc).reshape(2, S5_LANES, D_S5)

    wb = jnp.concatenate([blockdiag_in(bb_re), blockdiag_in(bb_im)], axis=2).astype(BF16)
    wc = jnp.concatenate([blockdiag_out(c_re), blockdiag_out(-c_im)], axis=1).astype(BF16)
    return {"layer": layer, "wb": wb, "wc": wc, "p_re": p_re, "p_im": p_im,
            "d_skip": d_skip.reshape(1, D_S5), "glu_w": glu_w.astype(BF16), "glu_b": glu_b.reshape(1, D_S5)}


def _ssd_state_in(s):
    b = s.shape[0]
    s = s.reshape(b, 2, SSD_GROUPS, SSD_HPG, SSD_P, SSD_N)
    return jnp.transpose(s, (0, 1, 2, 5, 3, 4)).reshape(b, 2, SSD_GROUPS, SSD_N, SSD_HPG * SSD_P)


def _ssd_state_out(s):
    b = s.shape[0]
    s = s.reshape(b, 2, SSD_GROUPS, SSD_N, SSD_HPG, SSD_P)
    return jnp.transpose(s, (0, 1, 2, 4, 5, 3)).reshape(b, 2, SSD_HEADS, SSD_P, SSD_N)


def kernel(x_prompt, x_sample, c, state_ssd, state_s5_re, state_s5_im, state_ret, c_ctx, ada_w, ada_b, norm1_g, norm2_g, w_in, w_out, ssd_conv_w, ssd_conv_b, ssd_dt_bias, ssd_a_log, ssd_d, ssd_norm_g, s5_lambda_re, s5_lambda_im, s5_log_dt, s5_b_re, s5_b_im, s5_c_re, s5_c_im, s5_d, s5_glu_w, s5_glu_b, ret_gn_g, peer_wq, peer_keys, peer_u, peer_v, final_norm_g):
    nc_ctx = CTX_LEN // CHUNK
    nc_dec = DEC_LEN // CHUNK
    dec_row_off = N_CTX_TOK // CHUNK

    cond = jnp.concatenate([c_ctx[None, :], c, jnp.zeros((N_COND - 1 - N_DEC_SEQ, D_MODEL), F32)], axis=0)
    mod3 = _adaln(cond, ada_w, ada_b).reshape(DEPTH * N_COND * N_MOD, 1, D_MODEL)
    p_re, p_im, k_re, k_im = _s5_prep(s5_lambda_re, s5_lambda_im, s5_log_dt)

    x = jnp.concatenate([x_prompt.reshape(N_CTX_TOK, D_MODEL), x_sample.reshape(N_TOK - N_CTX_TOK, D_MODEL)], axis=0)
    new_ssd, new_re, new_im, new_ret = [], [], [], []
    for l in range(DEPTH):
        proj = _inproj(x, mod3, l, norm1_g[l], _permute_w_in(w_in[l]))

        sp = _ssd_params(ssd_conv_w[l], ssd_conv_b[l], ssd_dt_bias[l], ssd_a_log[l], ssd_d[l], ssd_norm_g[l])
        y_ssd_c, st_ssd = _ssd(proj, sp, N_CTX_SEQ, nc_ctx, 0, None)
        y_ssd_d, _ = _ssd(proj, sp, N_DEC_SEQ, nc_dec, dec_row_off, _ssd_state_in(state_ssd[:, l]))

        s5p = _s5_params(l, k_re[2 * l:2 * l + 2], k_im[2 * l:2 * l + 2], p_re, p_im, s5_b_re[l], s5_b_im[l],
                         s5_c_re[l], s5_c_im[l], s5_d[l], s5_glu_w[l], s5_glu_b[l])
        y_s5_c, st_re, st_im = _s5(proj, s5p, N_CTX_SEQ, nc_ctx, 0, None)
        x0 = (state_s5_re[:, l].reshape(N_DEC_SEQ, 2, 1, S5_LANES), state_s5_im[:, l].reshape(N_DEC_SEQ, 2, 1, S5_LANES))
        y_s5_d, _, _ = _s5(proj, s5p, N_DEC_SEQ, nc_dec, dec_row_off, x0)

        y_ret_c, st_ret = _ret(proj, ret_gn_g[l], N_CTX_SEQ, nc_ctx, 0, None, rope=False)
        y_ret_d, _ = _ret(proj, ret_gn_g[l], N_DEC_SEQ, nc_dec, dec_row_off,
                          jnp.swapaxes(state_ret[:, l], -1, -2), rope=True)

        x = _outproj(x, jnp.concatenate([y_ssd_c, y_ssd_d]), jnp.concatenate([y_s5_c, y_s5_d]),
                     jnp.concatenate([y_ret_c, y_ret_d]), w_out[l].astype(BF16), mod3, l)

        hb, thr, g1, s2, e2 = _peer_scores(x, mod3, l, norm2_g[l], peer_wq[l].astype(BF16), peer_keys[l].astype(BF16))
        x = _peer_experts(x, mod3, l, hb, thr, g1, s2, e2, peer_u[l].astype(BF16), peer_v[l].T.astype(BF16))

        new_ssd.append(_ssd_state_out(st_ssd))
        new_re.append(st_re.reshape(N_CTX_SEQ, 2, S5_GROUPS, S5_STATE))
        new_im.append(st_im.reshape(N_CTX_SEQ, 2, S5_GROUPS, S5_STATE))
        new_ret.append(jnp.swapaxes(st_ret, -1, -2))

    y = _final_norm(x, final_norm_g)
    y_prompt = y[:N_CTX_TOK].reshape(N_CTX_SEQ, CTX_LEN, D_MODEL)
    y_sample = y[N_CTX_TOK:].reshape(N_DEC_SEQ, DEC_LEN, D_MODEL)
    return (y_prompt, y_sample, jnp.stack(new_ssd, axis=1), jnp.stack(new_re, axis=1),
            jnp.stack(new_im, axis=1), jnp.stack(new_ret, axis=1))
```

```python
import functools
import math

import jax
import jax.numpy as jnp
import numpy as np
from jax import lax
from jax.experimental import pallas as pl
from jax.experimental.pallas import tpu as pltpu

F32 = jnp.float32
BF16 = jnp.bfloat16

D_MODEL = 2048
N_CTX_SEQ = 16
CTX_LEN = 256
N_DEC_SEQ = 2
DEC_LEN = 2048
N_CTX_TOK = N_CTX_SEQ * CTX_LEN
N_TOK = N_CTX_TOK + N_DEC_SEQ * DEC_LEN
DEPTH = 2
GRID_W = 64
CHUNK = 128

SSD_HEADS = 12
SSD_P = 64
D_SSD = SSD_HEADS * SSD_P
SSD_N = 128
SSD_GROUPS = 2
SSD_HPG = SSD_HEADS // SSD_GROUPS
SSD_CONV_K = 5
SSD_CONV_CH = D_SSD + 2 * SSD_GROUPS * SSD_N
S5_CH = 16
S5_GROUPS = 32
D_S5 = S5_CH * S5_GROUPS
S5_STATE = 64
S5_LANES = S5_GROUPS * S5_STATE
S5_TILE = 8
RET_HEADS = 6
RET_DK = 128
RET_DV = 128
D_RET = RET_HEADS * RET_DV
ROPE_BASE = 10000.0
PEER_HEADS = 8
PEER_DQ = 256
PEER_NKEYS = 128
PEER_EXPERTS = PEER_NKEYS * PEER_NKEYS
PEER_TOPK = 16
N_MOD = 6
N_COND = 8
EPS = 1e-6

COL_Z, COL_Q, COL_K, COL_V, COL_G = 0, 768, 1536, 2304, 3072
COL_XBC = 3840
COL_U = 5120
COL_DT = 5632
PROJ_W = 6144

VMEM_LIMIT = 56 * 1024 * 1024

NEG_INF = float("-inf")


def _cparams(sem, vmem_limit=VMEM_LIMIT):
    return pltpu.CompilerParams(dimension_semantics=sem, vmem_limit_bytes=vmem_limit)


def _split3(a):
    hi = a.astype(BF16)
    r1 = a - hi.astype(F32)
    mid = r1.astype(BF16)
    lo = (r1 - mid.astype(F32)).astype(BF16)
    return hi, mid, lo


def _dot(a, b):
    return jnp.dot(a, b, preferred_element_type=F32)


def _dot_split_lhs(a, b_exact):
    hi, mid, lo = _split3(a)
    return _dot(hi, b_exact) + _dot(mid, b_exact) + _dot(lo, b_exact)


def _dot_split_rhs(a_exact, b):
    hi, mid, lo = _split3(b)
    return _dot(a_exact, hi) + _dot(a_exact, mid) + _dot(a_exact, lo)


def _dot_nt(a, b):
    return lax.dot_general(a, b, (((1,), (1,)), ((), ())), preferred_element_type=F32)


def _silu(x):
    return x * jax.nn.sigmoid(x)


def _softplus(x):
    return jnp.maximum(x, 0.0) + jnp.log1p(jnp.exp(-jnp.abs(x)))


ADA_TN = 1536


def _adaln_kernel(c_ref, w_ref, b_ref, o_ref):
    s = _silu(c_ref[...]).astype(BF16)
    o_ref[...] = _dot(s, w_ref[...].astype(BF16)) + b_ref[...]


def _adaln(cond, ada_w, ada_b):
    n_out = N_MOD * D_MODEL
    return pl.pallas_call(
        _adaln_kernel,
        grid=(DEPTH, n_out // ADA_TN),
        in_specs=[
            pl.BlockSpec((N_COND, D_MODEL), lambda l, j: (0, 0)),
            pl.BlockSpec((None, D_MODEL, ADA_TN), lambda l, j: (l, 0, j)),
            pl.BlockSpec((None, 1, ADA_TN), lambda l, j: (l, 0, j)),
        ],
        out_specs=pl.BlockSpec((None, N_COND, ADA_TN), lambda l, j: (l, 0, j)),
        out_shape=jax.ShapeDtypeStruct((DEPTH, N_COND, n_out), F32),
        compiler_params=_cparams(("arbitrary", "arbitrary")),
        name="adaln",
    )(cond, ada_w, ada_b.reshape(DEPTH, 1, n_out))


def _mod_spec(layer, which, tm):
    n_ctx_tiles = N_CTX_TOK // tm
    tiles_per_dec = DEC_LEN // tm

    def index(i, *_):
        cond = jnp.where(i < n_ctx_tiles, 0, 1 + (i - n_ctx_tiles) // tiles_per_dec)
        return ((layer * N_COND + cond) * N_MOD + which, 0, 0)

    return pl.BlockSpec((None, 1, D_MODEL), index)


def _rms_modulate(x, g, sc, sh):
    var = jnp.mean(x * x, axis=-1, keepdims=True)
    y = x * lax.rsqrt(var + EPS) * g
    return y * (1.0 + sc) + sh


INPROJ_TM = 512
INPROJ_TN = 2048


def _inproj_kernel(x_ref, sh_ref, sc_ref, g_ref, w_ref, o_ref, h_scr):
    @pl.when(pl.program_id(1) == 0)
    def _():
        h_scr[...] = _rms_modulate(x_ref[...], g_ref[...], sc_ref[...], sh_ref[...]).astype(BF16)

    o_ref[...] = _dot(h_scr[...], w_ref[...])


def _inproj(x, mod3, layer, norm_g, w_perm):
    tm, tn = INPROJ_TM, INPROJ_TN
    return pl.pallas_call(
        _inproj_kernel,
        grid=(N_TOK // tm, PROJ_W // tn),
        in_specs=[
            pl.BlockSpec((tm, D_MODEL), lambda i, j: (i, 0)),
            _mod_spec(layer, 0, tm),
            _mod_spec(layer, 1, tm),
            pl.BlockSpec((1, D_MODEL), lambda i, j: (0, 0)),
            pl.BlockSpec((D_MODEL, tn), lambda i, j: (0, j)),
        ],
        out_specs=pl.BlockSpec((tm, tn), lambda i, j: (i, j)),
        out_shape=jax.ShapeDtypeStruct((N_TOK, PROJ_W), F32),
        scratch_shapes=[pltpu.VMEM((tm, D_MODEL), BF16)],
        compiler_params=_cparams(("arbitrary", "arbitrary")),
        name="inproj",
    )(x, mod3, mod3, norm_g.reshape(1, D_MODEL), w_perm)


def _walk(nc):
    def direction(t):
        return jnp.where(t < nc, 1, 0)

    def chunk(t):
        return jnp.where(t < nc, nc - 1 - t, t - nc)

    return direction, chunk


def _seq_specs(nc, row_off):
    direction, chunk = _walk(nc)

    def rows(s, t):
        return row_off + s * nc + chunk(t)

    def out_rows(s, t):
        return row_off + s * nc + jnp.where(t < nc, 0, t - nc)

    return direction, chunk, rows, out_rows


def _ssd_consts():
    idx = np.arange(CHUNK)
    tri = np.stack([(idx[None, :] <= idx[:, None]), (idx[None, :] >= idx[:, None])]).astype(np.float32)
    e_p = np.zeros((CHUNK, D_SSD), np.float32)
    e_n = np.zeros((CHUNK, SSD_HEADS * CHUNK), np.float32)
    for h in range(SSD_HEADS):
        e_p[h, h * SSD_P:(h + 1) * SSD_P] = 1.0
        e_n[h, h * CHUNK:(h + 1) * CHUNK] = 1.0
    lane = np.arange(CHUNK)
    pair = np.concatenate([np.broadcast_to(lane < SSD_P, (CHUNK, CHUNK)),
                           np.broadcast_to(lane >= SSD_P, (CHUNK, CHUNK))]).astype(np.float32)
    return tri, e_p, e_n, pair


def _ssd_kernel(nc, zero_init, emit_state, *refs):
    (z_ref, xc_ref, xp_ref, xn_ref, dt_ref, cw_ref, cb_ref, dtb_ref, aexp_ref, dsk_ref, ng_ref,
     tri_ref, ep_ref, en_ref, pair_ref) = refs[:15]
    pos = 15
    s0_ref = None
    if not zero_init:
        s0_ref = refs[pos]
        pos += 1
    y_ref = refs[pos]
    pos += 1
    st_ref = None
    if emit_state:
        st_ref = refs[pos]
        pos += 1
    state_scr, stash_scr = refs[pos:]

    t = pl.program_id(1)
    fwd = t >= nc
    step = jnp.where(fwd, t - nc, t)
    chunk = jnp.where(fwd, t - nc, nc - 1 - t)
    row0 = pl.multiple_of(chunk * CHUNK, CHUNK)

    @pl.when(step == 0)
    def _():
        if zero_init:
            state_scr[...] = jnp.zeros_like(state_scr)
        else:
            state_scr[...] = s0_ref[...]

    prev = jnp.where(chunk > 0, xp_ref[...], 0.0)
    nxt = jnp.where(chunk < nc - 1, xn_ref[...], 0.0)
    ext = jnp.concatenate([prev, xc_ref[...], nxt], axis=0)
    conv = cb_ref[...]
    for k in range(SSD_CONV_K):
        off = 8 + k - SSD_CONV_K // 2
        conv = conv + cw_ref[k:k + 1, :] * ext[off:off + CHUNK, :]
    xbc = _silu(conv)
    x = xbc[:, :D_SSD]
    bm = xbc[:, D_SSD:D_SSD + SSD_GROUPS * SSD_N]
    cm = xbc[:, D_SSD + SSD_GROUPS * SSD_N:]

    dt = _softplus(dt_ref[...] + dtb_ref[...])
    la = -dt * aexp_ref[...]
    tri = tri_ref[...]
    cum = _dot_split_rhs(tri.astype(BF16), la)
    ep = ep_ref[...]
    cum_p = _dot_split_lhs(cum, ep)
    dt_p = _dot_split_lhs(dt, ep)
    cum_col = _dot_split_lhs(cum, en_ref[...])
    cum_t = cum.T
    tot_p = jnp.where(fwd, cum_p[CHUNK - 1:CHUNK, :], cum_p[0:1, :])

    v = x * dt_p
    vb = v.astype(BF16)
    mask = tri > 0.5
    pair = pair_ref[...]
    y_parts = []
    for g in range(SSD_GROUPS):
        cg = cm[:, g * SSD_N:(g + 1) * SSD_N].astype(BF16)
        bg = bm[:, g * SSD_N:(g + 1) * SSD_N]
        gmat = _dot_nt(cg, bg.astype(BF16))
        for hp in range(SSD_HPG // 2):
            scs = []
            for h in (g * SSD_HPG + 2 * hp, g * SSD_HPG + 2 * hp + 1):
                ci = cum_col[:, h * CHUNK:(h + 1) * CHUNK]
                cj = cum_t[h:h + 1, :]
                dec = jnp.exp(jnp.where(mask, ci - cj, NEG_INF))
                scs.append((gmat * dec).astype(BF16))
            c0 = (g * SSD_HPG + 2 * hp) * SSD_P
            v2 = vb[:, c0:c0 + 2 * SSD_P]
            vv = jnp.concatenate([v2, v2], axis=0) * pair
            y_parts.append(_dot(jnp.concatenate(scs, axis=1), vv))
    y = jnp.concatenate(y_parts, axis=1)

    w_p = SSD_HPG * SSD_P
    y_off = jnp.concatenate(
        [_dot(cm[:, g * SSD_N:(g + 1) * SSD_N].astype(BF16), state_scr[g].astype(BF16))
         for g in range(SSD_GROUPS)], axis=1)
    y = y + y_off * jnp.exp(cum_p)

    vw = (v * jnp.exp(tot_p - cum_p)).astype(BF16)
    cdec = jnp.exp(tot_p)
    for g in range(SSD_GROUPS):
        bt = bm[:, g * SSD_N:(g + 1) * SSD_N].T.astype(BF16)
        state_scr[g] = state_scr[g] * cdec[:, g * w_p:(g + 1) * w_p] + _dot(bt, vw[:, g * w_p:(g + 1) * w_p])

    @pl.when(jnp.logical_not(fwd))
    def _():
        stash_scr[pl.ds(row0, CHUNK), :] = y

    @pl.when(fwd)
    def _():
        ytot = y + stash_scr[pl.ds(row0, CHUNK), :] + x * dsk_ref[...]
        gated = ytot * _silu(z_ref[...])
        var = jnp.mean(gated * gated, axis=-1, keepdims=True)
        y_ref[...] = (gated * lax.rsqrt(var + EPS) * ng_ref[...]).astype(y_ref.dtype)

    if emit_state:
        @pl.when(step == nc - 1)
        def _():
            for g in range(SSD_GROUPS):
                st_t = state_scr[g].T
                for k in range(SSD_HPG):
                    st_ref[g * SSD_HPG + k] = st_t[k * SSD_P:(k + 1) * SSD_P, :]


def _ssd(proj, p, n_seq, nc, row_off, s0):
    zero_init = s0 is None
    emit_state = zero_init
    direction, chunk, rows, out_rows = _seq_specs(nc, row_off)
    tri, e_p, e_n, pair = _ssd_consts()
    n8 = N_TOK // 8
    w_st = SSD_HPG * SSD_P
    in_specs = [
        pl.BlockSpec((CHUNK, D_SSD), lambda s, t: (rows(s, t), COL_Z // D_SSD)),
        pl.BlockSpec((CHUNK, SSD_CONV_CH), lambda s, t: (rows(s, t), COL_XBC // SSD_CONV_CH)),
        pl.BlockSpec((8, SSD_CONV_CH),
                     lambda s, t: (jnp.maximum(rows(s, t) * (CHUNK // 8) - 1, 0), COL_XBC // SSD_CONV_CH)),
        pl.BlockSpec((8, SSD_CONV_CH),
                     lambda s, t: (jnp.minimum((rows(s, t) + 1) * (CHUNK // 8), n8 - 1), COL_XBC // SSD_CONV_CH)),
        pl.BlockSpec((CHUNK, CHUNK), lambda s, t: (rows(s, t), COL_DT // CHUNK + direction(t))),
        pl.BlockSpec((8, SSD_CONV_CH), lambda s, t: (0, 0)),
        pl.BlockSpec((1, SSD_CONV_CH), lambda s, t: (0, 0)),
        pl.BlockSpec((None, 1, CHUNK), lambda s, t: (direction(t), 0, 0)),
        pl.BlockSpec((None, 1, CHUNK), lambda s, t: (direction(t), 0, 0)),
        pl.BlockSpec((1, D_SSD), lambda s, t: (0, 0)),
        pl.BlockSpec((1, D_SSD), lambda s, t: (0, 0)),
        pl.BlockSpec((None, CHUNK, CHUNK), lambda s, t: (direction(t), 0, 0)),
        pl.BlockSpec((CHUNK, D_SSD), lambda s, t: (0, 0)),
        pl.BlockSpec((CHUNK, SSD_HEADS * CHUNK), lambda s, t: (0, 0)),
        pl.BlockSpec((2 * CHUNK, CHUNK), lambda s, t: (0, 0)),
    ]
    args = [proj, proj, proj, proj, proj, p["conv_w"], p["conv_b"], p["dt_bias"], p["a_exp"], p["d_skip"],
            p["norm_g"], jnp.asarray(tri), jnp.asarray(e_p, BF16), jnp.asarray(e_n, BF16), jnp.asarray(pair, BF16)]
    st_spec = pl.BlockSpec((None, None, SSD_GROUPS, SSD_N, w_st), lambda s, t: (s, direction(t), 0, 0, 0))
    if not zero_init:
        in_specs.append(st_spec)
        args.append(s0)
    out_specs = [pl.BlockSpec((CHUNK, D_SSD), lambda s, t: (out_rows(s, t) - row_off, 0))]
    out_shape = [jax.ShapeDtypeStruct((n_seq * nc * CHUNK, D_SSD), BF16)]
    if emit_state:
        out_specs.append(pl.BlockSpec((None, None, SSD_HEADS, SSD_P, SSD_N), lambda s, t: (s, direction(t), 0, 0, 0)))
        out_shape.append(jax.ShapeDtypeStruct((n_seq, 2, SSD_HEADS, SSD_P, SSD_N), F32))
    res = pl.pallas_call(
        functools.partial(_ssd_kernel, nc, zero_init, emit_state),
        grid=(n_seq, 2 * nc),
        in_specs=in_specs,
        out_specs=out_specs,
        out_shape=out_shape,
        scratch_shapes=[pltpu.VMEM((SSD_GROUPS, SSD_N, w_st), F32), pltpu.VMEM((nc * CHUNK, D_SSD), F32)],
        compiler_params=_cparams(("arbitrary", "arbitrary")),
        name="ssd_ctx" if zero_init else "ssd_dec",
    )(*args)
    return res if emit_state else (res[0], None)


def _s5_prep_kernel(lr_ref, li_ref, ldt_ref, pre_ref, pim_ref, kre_ref, kim_ref):
    lr = lr_ref[...]
    li = li_ref[...]
    dt = jnp.exp(ldt_ref[...])
    kk = (lax.broadcasted_iota(jnp.int32, (S5_TILE, S5_LANES), 0) + 1).astype(F32)
    mag = jnp.exp(kk * (lr * dt))
    ang = kk * (li * dt)
    p_re = mag * jnp.cos(ang)
    p_im = mag * jnp.sin(ang)
    pre_ref[...] = p_re
    pim_ref[...] = p_im
    a_re = p_re[0:1, :]
    a_im = p_im[0:1, :]
    den = lr * lr + li * li
    num_re = a_re - 1.0
    kre_ref[...] = (num_re * lr + a_im * li) / den
    kim_ref[...] = (a_im * lr - num_re * li) / den


def _s5_prep(lam_re, lam_im, log_dt):
    n = DEPTH * 2
    lr = lam_re.reshape(n, 1, S5_LANES)
    li = lam_im.reshape(n, 1, S5_LANES)
    ldt = jnp.broadcast_to(log_dt[..., None], (DEPTH, 2, S5_GROUPS, S5_STATE)).reshape(n, 1, S5_LANES)
    row = pl.BlockSpec((None, 1, S5_LANES), lambda i: (i, 0, 0))
    tab = pl.BlockSpec((None, S5_TILE, S5_LANES), lambda i: (i, 0, 0))
    return pl.pallas_call(
        _s5_prep_kernel,
        grid=(n,),
        in_specs=[row, row, row],
        out_specs=[tab, tab, row, row],
        out_shape=[jax.ShapeDtypeStruct((n, S5_TILE, S5_LANES), F32)] * 2
        + [jax.ShapeDtypeStruct((n, 1, S5_LANES), F32)] * 2,
        compiler_params=_cparams(("arbitrary",)),
        name="s5_prep",
    )(lr, li, ldt)


def _s5_kernel(nc, zero_init, emit_state, *refs):
    (u_ref, jm_ref, wb_ref, pre_ref, pim_ref, wc_ref, dsk_ref, gw_ref, gb_ref) = refs[:9]
    pos = 9
    x0re_ref = x0im_ref = None
    if not zero_init:
        x0re_ref, x0im_ref = refs[pos:pos + 2]
        pos += 2
    y_ref = refs[pos]
    pos += 1
    sre_ref = sim_ref = None
    if emit_state:
        sre_ref, sim_ref = refs[pos:pos + 2]
        pos += 2
    cre_scr, cim_scr, stash_scr = refs[pos:]

    t = pl.program_id(1)
    fwd = t >= nc
    step = jnp.where(fwd, t - nc, t)
    chunk = jnp.where(fwd, t - nc, nc - 1 - t)
    row0 = pl.multiple_of(chunk * CHUNK, CHUNK)

    @pl.when(step == 0)
    def _():
        if zero_init:
            cre_scr[...] = jnp.zeros_like(cre_scr)
            cim_scr[...] = jnp.zeros_like(cim_scr)
        else:
            cre_scr[...] = x0re_ref[...]
            cim_scr[...] = x0im_ref[...]

    u = u_ref[...]
    jm = jm_ref[...]
    us = _dot(jm, u.astype(BF16)).astype(BF16)
    bu = _dot(us, wb_ref[...])
    xr = bu[:, :S5_LANES]
    xi = bu[:, S5_LANES:]
    cr = cre_scr[...]
    ci = cim_scr[...]
    pr = pre_ref[...]
    pi = pim_ref[...]
    sub = lax.broadcasted_iota(jnp.int32, (S5_TILE, S5_LANES), 0)
    steps = []
    d = 1
    while d < S5_TILE:
        keep = sub >= d
        steps.append((d, jnp.where(keep, pre_ref[d - 1:d, :], 0.0), jnp.where(keep, pim_ref[d - 1:d, :], 0.0)))
        d *= 2
    tiles_r, tiles_i = [], []
    for b in range(CHUNK // S5_TILE):
        br = xr[b * S5_TILE:(b + 1) * S5_TILE, :]
        bi = xi[b * S5_TILE:(b + 1) * S5_TILE, :]
        for d, ar, ai in steps:
            sr = pltpu.roll(br, d, axis=0)
            si = pltpu.roll(bi, d, axis=0)
            br, bi = br + (ar * sr - ai * si), bi + (ar * si + ai * sr)
        br, bi = br + (pr * cr - pi * ci), bi + (pr * ci + pi * cr)
        cr = br[S5_TILE - 1:S5_TILE, :]
        ci = bi[S5_TILE - 1:S5_TILE, :]
        tiles_r.append(br)
        tiles_i.append(bi)
    xr = jnp.concatenate(tiles_r, axis=0)
    xi = jnp.concatenate(tiles_i, axis=0)
    cre_scr[...] = cr
    cim_scr[...] = ci

    ys = _dot(xr.astype(BF16), wc_ref[:S5_LANES, :]) + _dot(xi.astype(BF16), wc_ref[S5_LANES:, :])
    yh = ys.astype(BF16)
    yl = (ys - yh.astype(F32)).astype(BF16)
    y = _dot(jm, yh) + _dot(jm, yl)

    @pl.when(jnp.logical_not(fwd))
    def _():
        stash_scr[pl.ds(row0, CHUNK), :] = y

    @pl.when(fwd)
    def _():
        yt = y + stash_scr[pl.ds(row0, CHUNK), :] + u * dsk_ref[...]
        yt = jax.nn.gelu(yt)
        gate = jax.nn.sigmoid(_dot(yt.astype(BF16), gw_ref[...]) + gb_ref[...])
        y_ref[...] = (yt * gate).astype(y_ref.dtype)

    if emit_state:
        @pl.when(step == nc - 1)
        def _():
            sre_ref[...] = cre_scr[...]
            sim_ref[...] = cim_scr[...]


def _s5(proj, p, n_seq, nc, row_off, x0):
    zero_init = x0 is None
    emit_state = zero_init
    direction, chunk, rows, out_rows = _seq_specs(nc, row_off)
    eye = np.eye(CHUNK, dtype=np.float32)
    jm = jnp.asarray(np.stack([eye, eye[::-1]]), BF16)
    lyr = p["layer"]

    def dsel(t):
        return lyr * 2 + direction(t)

    in_specs = [
        pl.BlockSpec((CHUNK, D_S5), lambda s, t: (rows(s, t), COL_U // D_S5)),
        pl.BlockSpec((None, CHUNK, CHUNK), lambda s, t: (direction(t), 0, 0)),
        pl.BlockSpec((None, D_S5, 2 * S5_LANES), lambda s, t: (direction(t), 0, 0)),
        pl.BlockSpec((None, S5_TILE, S5_LANES), lambda s, t: (dsel(t), 0, 0)),
        pl.BlockSpec((None, S5_TILE, S5_LANES), lambda s, t: (dsel(t), 0, 0)),
        pl.BlockSpec((None, 2 * S5_LANES, D_S5), lambda s, t: (direction(t), 0, 0)),
        pl.BlockSpec((1, D_S5), lambda s, t: (0, 0)),
        pl.BlockSpec((D_S5, D_S5), lambda s, t: (0, 0)),
        pl.BlockSpec((1, D_S5), lambda s, t: (0, 0)),
    ]
    args = [proj, jm, p["wb"], p["p_re"], p["p_im"], p["wc"], p["d_skip"], p["glu_w"], p["glu_b"]]
    st_spec = pl.BlockSpec((None, None, 1, S5_LANES), lambda s, t: (s, direction(t), 0, 0))
    if not zero_init:
        in_specs += [st_spec, st_spec]
        args += [x0[0], x0[1]]
    out_specs = [pl.BlockSpec((CHUNK, D_S5), lambda s, t: (out_rows(s, t) - row_off, 0))]
    out_shape = [jax.ShapeDtypeStruct((n_seq * nc * CHUNK, D_S5), BF16)]
    if emit_state:
        out_specs += [st_spec, st_spec]
        out_shape += [jax.ShapeDtypeStruct((n_seq, 2, 1, S5_LANES), F32)] * 2
    res = pl.pallas_call(
        functools.partial(_s5_kernel, nc, zero_init, emit_state),
        grid=(n_seq, 2 * nc),
        in_specs=in_specs,
        out_specs=out_specs,
        out_shape=out_shape,
        scratch_shapes=[pltpu.VMEM((1, S5_LANES), F32), pltpu.VMEM((1, S5_LANES), F32),
                        pltpu.VMEM((nc * CHUNK, D_S5), F32)],
        compiler_params=_cparams(("arbitrary", "arbitrary")),
        name="s5_ctx" if zero_init else "s5_dec",
    )(*args)
    return res if emit_state else (res[0], None, None)


def _ret_consts():
    heads = np.arange(RET_HEADS, dtype=np.float64)
    lg = np.stack([np.log1p(-np.exp2(-5.0 - heads)), np.log1p(-np.exp2(-5.5 - heads))])
    i = np.arange(CHUNK, dtype=np.float64)
    diff = i[:, None] - i[None, :]
    dmat = np.zeros((2, RET_HEADS, CHUNK, CHUNK))
    rowdec = np.zeros((2, CHUNK, D_RET))
    wend = np.zeros((2, 8, CHUNK))
    cdec = np.zeros((2, 8, CHUNK))
    for h in range(RET_HEADS):
        dmat[0, h] = np.where(diff >= 0, np.exp(lg[0, h] * diff), 0.0)
        dmat[1, h] = np.where(diff <= 0, np.exp(-lg[1, h] * diff), 0.0)
        rowdec[0, :, h * RET_DV:(h + 1) * RET_DV] = np.exp(lg[0, h] * (i + 1))[:, None]
        rowdec[1, :, h * RET_DV:(h + 1) * RET_DV] = np.exp(lg[1, h] * (CHUNK - i))[:, None]
        wend[0, h] = np.exp(lg[0, h] * (CHUNK - 1 - i))
        wend[1, h] = np.exp(lg[1, h] * i)
        cdec[:, h] = np.exp(lg[:, h] * CHUNK)[:, None]
    return [jnp.asarray(a, F32) for a in (dmat, rowdec, wend, cdec)]


def _rope_tables():
    t = np.arange(DEC_LEN)
    row = (t // GRID_W).astype(np.float32)
    col = (t % GRID_W).astype(np.float32)
    quarter = RET_DK // 4
    freqs = (ROPE_BASE ** (-np.arange(quarter, dtype=np.float32) / quarter)).astype(np.float32)
    ar = (row[:, None] * freqs[None, :]).astype(np.float64)
    ac = (col[:, None] * freqs[None, :]).astype(np.float64)
    cos = np.concatenate([np.cos(ar), np.cos(ar), np.cos(ac), np.cos(ac)], axis=1)
    sin = np.concatenate([-np.sin(ar), np.sin(ar), -np.sin(ac), np.sin(ac)], axis=1)
    return jnp.asarray(cos, F32), jnp.asarray(sin, F32)


def _ret_kernel(nc, zero_init, emit_state, rope, *refs):
    q_ref, k_ref, v_ref, g_ref = refs[:4]
    pos = 4
    cos_ref = sin_ref = None
    if rope:
        cos_ref, sin_ref = refs[pos:pos + 2]
        pos += 2
    dmat_ref, rowdec_ref, wend_ref, cdec_ref, gn_ref = refs[pos:pos + 5]
    pos += 5
    s0_ref = None
    if not zero_init:
        s0_ref = refs[pos]
        pos += 1
    y_ref = refs[pos]
    pos += 1
    st_ref = None
    if emit_state:
        st_ref = refs[pos]
        pos += 1
    state_scr, stash_scr = refs[pos:]

    t = pl.program_id(1)
    fwd = t >= nc
    step = jnp.where(fwd, t - nc, t)
    chunk = jnp.where(fwd, t - nc, nc - 1 - t)
    row0 = pl.multiple_of(chunk * CHUNK, CHUNK)

    @pl.when(step == 0)
    def _():
        if zero_init:
            state_scr[...] = jnp.zeros_like(state_scr)
        else:
            state_scr[...] = s0_ref[...]

    q = q_ref[...]
    k = k_ref[...]
    if rope:
        cos = jnp.concatenate([cos_ref[...]] * RET_HEADS, axis=1)
        sin = jnp.concatenate([sin_ref[...]] * RET_HEADS, axis=1)
        lane = lax.broadcasted_iota(jnp.int32, (CHUNK, D_RET), 1)
        first = (lane // (RET_DK // 4)) % 2 == 0

        def rot(x):
            partner = jnp.where(first, pltpu.roll(x, D_RET - RET_DK // 4, axis=1), pltpu.roll(x, RET_DK // 4, axis=1))
            return x * cos + partner * sin

        q = rot(q)
        k = rot(k)
    k = k * (RET_DK ** -0.5)
    qb = q.astype(BF16)
    vb = v_ref[...].astype(BF16)
    rowdec = rowdec_ref[...]
    y_parts = []
    for h in range(RET_HEADS):
        sl = slice(h * RET_DK, (h + 1) * RET_DK)
        kh = k[:, sl]
        sc = _dot_nt(qb[:, sl], kh.astype(BF16)) * dmat_ref[h]
        yh = _dot(sc.astype(BF16), vb[:, sl])
        yh = yh + _dot(qb[:, sl], state_scr[h].astype(BF16)) * rowdec[:, sl]
        y_parts.append(yh)
        kt = (kh.T * wend_ref[h:h + 1, :]).astype(BF16)
        state_scr[h] = state_scr[h] * cdec_ref[h:h + 1, :] + _dot(kt, vb[:, sl])
    y = jnp.concatenate(y_parts, axis=1)

    @pl.when(jnp.logical_not(fwd))
    def _():
        stash_scr[pl.ds(row0, CHUNK), :] = y

    @pl.when(fwd)
    def _():
        yt = y + stash_scr[pl.ds(row0, CHUNK), :]
        outs = []
        for h in range(RET_HEADS):
            yh = yt[:, h * RET_DV:(h + 1) * RET_DV]
            yc = yh - jnp.mean(yh, axis=-1, keepdims=True)
            outs.append(yc * lax.rsqrt(jnp.mean(yc * yc, axis=-1, keepdims=True) + EPS))
        yn = jnp.concatenate(outs, axis=1)
        y_ref[...] = (yn * gn_ref[...] * _silu(g_ref[...])).astype(y_ref.dtype)

    if emit_state:
        @pl.when(step == nc - 1)
        def _():
            for h in range(RET_HEADS):
                st_ref[h] = state_scr[h].T


def _ret(proj, gn_g, n_seq, nc, row_off, s0, rope):
    zero_init = s0 is None
    emit_state = zero_init
    direction, chunk, rows, out_rows = _seq_specs(nc, row_off)
    dmat, rowdec, wend, cdec = _ret_consts()

    def col(c):
        return pl.BlockSpec((CHUNK, D_RET), lambda s, t: (rows(s, t), c // D_RET))

    in_specs = [col(COL_Q), col(COL_K), col(COL_V), col(COL_G)]
    args = [proj, proj, proj, proj]
    if rope:
        cos, sin = _rope_tables()
        tab = pl.BlockSpec((CHUNK, RET_DK), lambda s, t: (chunk(t), 0))
        in_specs += [tab, tab]
        args += [cos, sin]
    in_specs += [
        pl.BlockSpec((None, RET_HEADS, CHUNK, CHUNK), lambda s, t: (direction(t), 0, 0, 0)),
        pl.BlockSpec((None, CHUNK, D_RET), lambda s, t: (direction(t), 0, 0)),
        pl.BlockSpec((None, 8, CHUNK), lambda s, t: (direction(t), 0, 0)),
        pl.BlockSpec((None, 8, CHUNK), lambda s, t: (direction(t), 0, 0)),
        pl.BlockSpec((1, D_RET), lambda s, t: (0, 0)),
    ]
    args += [dmat, rowdec, wend, cdec, gn_g.reshape(1, D_RET)]
    st_spec = pl.BlockSpec((None, None, RET_HEADS, RET_DK, RET_DV), lambda s, t: (s, direction(t), 0, 0, 0))
    if not zero_init:
        in_specs.append(st_spec)
        args.append(s0)
    out_specs = [pl.BlockSpec((CHUNK, D_RET), lambda s, t: (out_rows(s, t) - row_off, 0))]
    out_shape = [jax.ShapeDtypeStruct((n_seq * nc * CHUNK, D_RET), BF16)]
    if emit_state:
        out_specs.append(st_spec)
        out_shape.append(jax.ShapeDtypeStruct((n_seq, 2, RET_HEADS, RET_DK, RET_DV), F32))
    res = pl.pallas_call(
        functools.partial(_ret_kernel, nc, zero_init, emit_state, rope),
        grid=(n_seq, 2 * nc),
        in_specs=in_specs,
        out_specs=out_specs,
        out_shape=out_shape,
        scratch_shapes=[pltpu.VMEM((RET_HEADS, RET_DK, RET_DV), F32), pltpu.VMEM((nc * CHUNK, D_RET), F32)],
        compiler_params=_cparams(("arbitrary", "arbitrary")),
        name="ret_ctx" if zero_init else "ret_dec",
    )(*args)
    return res if emit_state else (res[0], None)


OUTPROJ_TM = 512


def _outproj_kernel(x_ref, ya_ref, yb_ref, yc_ref, w_ref, g_ref, o_ref):
    y = _dot(ya_ref[...], w_ref[:D_SSD, :])
    y = y + _dot(yb_ref[...], w_ref[D_SSD:D_SSD + D_S5, :])
    y = y + _dot(yc_ref[...], w_ref[D_SSD + D_S5:, :])
    o_ref[...] = x_ref[...] + g_ref[...] * y


def _outproj(x, y_ssd, y_s5, y_ret, w_out, mod3, layer):
    tm = OUTPROJ_TM
    return pl.pallas_call(
        _outproj_kernel,
        grid=(N_TOK // tm,),
        in_specs=[
            pl.BlockSpec((tm, D_MODEL), lambda i: (i, 0)),
            pl.BlockSpec((tm, D_SSD), lambda i: (i, 0)),
            pl.BlockSpec((tm, D_S5), lambda i: (i, 0)),
            pl.BlockSpec((tm, D_RET), lambda i: (i, 0)),
            pl.BlockSpec((D_MODEL, D_MODEL), lambda i: (0, 0)),
            _mod_spec(layer, 2, tm),
        ],
        out_specs=pl.BlockSpec((tm, D_MODEL), lambda i: (i, 0)),
        out_shape=jax.ShapeDtypeStruct((N_TOK, D_MODEL), F32),
        compiler_params=_cparams(("arbitrary",)),
        name="outproj",
    )(x, y_ssd, y_s5, y_ret, w_out, mod3)


PEER_SC_TM = 256
PEER_NCAND = PEER_TOPK + 1


def _cand_pairs():
    return [(i, j) for i in range(PEER_NCAND) for j in range(PEER_NCAND) if (i + 1) * (j + 1) <= PEER_NCAND]


def _top_rows(work, n):
    rows = []
    for r in range(n):
        m = jnp.max(work, axis=0, keepdims=True)
        rows.append(m)
        if r < n - 1:
            work = jnp.where(work >= m, NEG_INF, work)
    return rows


def _peer_scores_kernel(x_ref, sh_ref, sc_ref, g_ref, wq_ref, keys_ref,
                        h_ref, thr_ref, g1_ref, s2_ref, e2_ref, cand_scr):
    tm = x_ref.shape[0]
    hb = _rms_modulate(x_ref[...], g_ref[...], sc_ref[...], sh_ref[...]).astype(BF16)
    h_ref[...] = hb
    q = _dot(hb, wq_ref[...]).astype(BF16)
    half = PEER_DQ // 2
    pairs = _cand_pairs()
    n_rows = cand_scr.shape[0]
    cand_scr[len(pairs):, :] = jnp.full((n_rows - len(pairs), tm), NEG_INF, F32)
    for h in range(PEER_HEADS):
        q1 = q[:, h * PEER_DQ:h * PEER_DQ + half]
        q2 = q[:, h * PEER_DQ + half:(h + 1) * PEER_DQ]
        s1 = _dot_nt(keys_ref[h, 0], q1)
        s2 = _dot_nt(keys_ref[h, 1], q2)
        a = _top_rows(s1, PEER_NCAND)
        b = _top_rows(s2, PEER_NCAND)
        for r, (i, j) in enumerate(pairs):
            cand_scr[r:r + 1, :] = a[i] + b[j]
        c = _top_rows(cand_scr[...], PEER_NCAND)
        top = a[0] + b[0]
        zsum = jnp.zeros_like(top)
        for r in range(PEER_TOPK):
            zsum = zsum + jnp.exp(c[r] - top)
        tau = 0.5 * (c[PEER_TOPK - 1] + c[PEER_TOPK])
        thr_ref[h] = tau - s1
        g1_ref[h] = jnp.exp(s1 - a[0]) * (0.5 / zsum)
        s2_ref[h] = s2
        e2_ref[h] = jnp.exp(s2 - b[0])


def _peer_scores(x, mod3, layer, norm_g, wq, keys):
    tm = PEER_SC_TM
    n_tok = x.shape[0]
    sc_spec = pl.BlockSpec((PEER_HEADS, PEER_NKEYS, tm), lambda i: (0, 0, i))
    sc_shape = jax.ShapeDtypeStruct((PEER_HEADS, PEER_NKEYS, n_tok), F32)
    n_cand_rows = -(-len(_cand_pairs()) // 8) * 8
    return pl.pallas_call(
        _peer_scores_kernel,
        grid=(n_tok // tm,),
        in_specs=[
            pl.BlockSpec((tm, D_MODEL), lambda i: (i, 0)),
            _mod_spec(layer, 3, tm),
            _mod_spec(layer, 4, tm),
            pl.BlockSpec((1, D_MODEL), lambda i: (0, 0)),
            pl.BlockSpec((D_MODEL, PEER_HEADS * PEER_DQ), lambda i: (0, 0)),
            pl.BlockSpec((PEER_HEADS, 2, PEER_NKEYS, PEER_DQ // 2), lambda i: (0, 0, 0, 0)),
        ],
        out_specs=[pl.BlockSpec((tm, D_MODEL), lambda i: (i, 0)), sc_spec, sc_spec, sc_spec, sc_spec],
        out_shape=[jax.ShapeDtypeStruct((n_tok, D_MODEL), BF16), sc_shape, sc_shape, sc_shape, sc_shape],
        scratch_shapes=[pltpu.VMEM((n_cand_rows, tm), F32)],
        compiler_params=_cparams(("arbitrary",)),
        name="peer_scores",
    )(x, mod3, mod3, norm_g.reshape(1, D_MODEL), wq, keys)


PEER_TM = 512
PEER_EBLK = 1024
GELU_C = math.sqrt(2.0 / math.pi)


def _peer_experts_kernel(x_ref, g2_ref, h_ref, thr_ref, g1_ref, s2_ref, e2_ref, u_ref, vt_ref,
                         o_ref, acc_scr, act_scr, w_scr, gl_scr):
    tm = x_ref.shape[0]
    eblk = u_ref.shape[0]
    j = pl.program_id(1)

    @pl.when(j == 0)
    def _():
        acc_scr[...] = jnp.zeros_like(acc_scr)

    nb = eblk // PEER_NKEYS
    rsub = 16
    na = 2

    def region(c):
        rows_c = slice(c * na * PEER_NKEYS, (c + 1) * na * PEER_NKEYS)
        act_scr[rows_c, :] = _dot_nt(u_ref[rows_c, :], h_ref[...])
        for lg in range(tm // 128):
            sl = slice(lg * 128, (lg + 1) * 128)
            for r0 in range(0, PEER_NKEYS, rsub):
                w = [jnp.zeros((rsub, 128), F32) for _ in range(na)]
                for h in range(PEER_HEADS):
                    s2 = s2_ref[h, r0:r0 + rsub, sl]
                    e2 = e2_ref[h, r0:r0 + rsub, sl]
                    for k in range(na):
                        a = c * na + k
                        w[k] = w[k] + jnp.where(s2 >= thr_ref[h, a:a + 1, sl], e2, 0.0) * g1_ref[h, a:a + 1, sl]
                for k in range(na):
                    r = (c * na + k) * PEER_NKEYS + r0
                    w_scr[r:r + rsub, sl] = w[k]

    for c in range(nb // na):
        pl.when(j + c < pl.num_programs(1) + c)(functools.partial(region, c))

    @pl.when(j + nb < pl.num_programs(1) + nb)
    def _():
        act = act_scr[...]
        inner = act * (GELU_C + (GELU_C * 0.044715) * (act * act))
        gl_scr[...] = ((act * w_scr[...]) * (1.0 + jnp.tanh(inner))).astype(BF16)

    acc_scr[...] += _dot(vt_ref[...], gl_scr[...])

    @pl.when(j == pl.num_programs(1) - 1)
    def _():
        o_ref[...] = x_ref[...] + g2_ref[...] * acc_scr[...].T


def _peer_experts(x, mod3, layer, hb, thr, g1, s2, e2, u_bf, vt_bf):
    tm, eblk = PEER_TM, PEER_EBLK
    n_tok = x.shape[0]
    once = pl.Buffered(1)
    sc_spec = pl.BlockSpec((PEER_HEADS, PEER_NKEYS, tm), lambda i, j: (0, 0, i))
    k1_spec = pl.BlockSpec((PEER_HEADS, eblk // PEER_NKEYS, tm), lambda i, j: (0, j, i))
    return pl.pallas_call(
        _peer_experts_kernel,
        grid=(n_tok // tm, PEER_EXPERTS // eblk),
        in_specs=[
            pl.BlockSpec((tm, D_MODEL), lambda i, j: (i, 0), pipeline_mode=once),
            _mod_spec(layer, 5, tm),
            pl.BlockSpec((tm, D_MODEL), lambda i, j: (i, 0)),
            k1_spec, k1_spec, sc_spec, sc_spec,
            pl.BlockSpec((eblk, D_MODEL), lambda i, j: (j, 0)),
            pl.BlockSpec((D_MODEL, eblk), lambda i, j: (0, j)),
        ],
        out_specs=pl.BlockSpec((tm, D_MODEL), lambda i, j: (i, 0)),
        out_shape=jax.ShapeDtypeStruct((n_tok, D_MODEL), F32),
        scratch_shapes=[pltpu.VMEM((D_MODEL, tm), F32), pltpu.VMEM((eblk, tm), F32), pltpu.VMEM((eblk, tm), F32),
                        pltpu.VMEM((eblk, tm), BF16)],
        compiler_params=_cparams(("arbitrary", "arbitrary")),
        name="peer_experts",
    )(x, mod3, hb, thr, g1, s2, e2, u_bf, vt_bf)


FINAL_TM = 512


def _final_norm_kernel(n_ctx_tiles, x_ref, g_ref, oc_ref, od_ref):
    x = x_ref[...]
    var = jnp.mean(x * x, axis=-1, keepdims=True)
    y = x * lax.rsqrt(var + EPS) * g_ref[...]
    i = pl.program_id(0)

    @pl.when(i < n_ctx_tiles)
    def _():
        oc_ref[...] = y

    @pl.when(i >= n_ctx_tiles)
    def _():
        od_ref[...] = y


def _final_norm(x, g):
    tm = FINAL_TM
    n_ctx_tiles = N_CTX_TOK // tm
    return pl.pallas_call(
        functools.partial(_final_norm_kernel, n_ctx_tiles),
        grid=(N_TOK // tm,),
        in_specs=[pl.BlockSpec((tm, D_MODEL), lambda i: (i, 0)), pl.BlockSpec((1, D_MODEL), lambda i: (0, 0))],
        out_specs=[pl.BlockSpec((tm, D_MODEL), lambda i: (jnp.minimum(i, n_ctx_tiles - 1), 0)),
                   pl.BlockSpec((tm, D_MODEL), lambda i: (jnp.maximum(i - n_ctx_tiles, 0), 0))],
        out_shape=[jax.ShapeDtypeStruct((N_CTX_TOK, D_MODEL), F32),
                   jax.ShapeDtypeStruct((N_TOK - N_CTX_TOK, D_MODEL), F32)],
        compiler_params=_cparams(("arbitrary",)),
        name="final_norm",
    )(x, g.reshape(1, D_MODEL))


def _permute_w_in(w):
    cuts = np.cumsum([D_SSD, SSD_CONV_CH, 2 * SSD_HEADS, D_S5, D_RET, D_RET, D_RET])
    z, xbc, dt, u, rq, rk, rv, rg = jnp.split(w, [int(c) for c in cuts], axis=1)
    pad = jnp.zeros((D_MODEL, CHUNK - SSD_HEADS), w.dtype)
    tail = jnp.zeros((D_MODEL, PROJ_W - COL_DT - 2 * CHUNK), w.dtype)
    out = jnp.concatenate([z, rq, rk, rv, rg, xbc, u, dt[:, :SSD_HEADS], pad, dt[:, SSD_HEADS:], pad, tail], axis=1)
    return out.astype(BF16)


def _pad_lanes(a, width):
    return jnp.pad(a, [(0, 0)] * (a.ndim - 1) + [(0, width - a.shape[-1])])


def _ssd_params(conv_w, conv_b, dt_bias, a_log, d_skip, norm_g):
    return {
        "conv_w": jnp.pad(conv_w, ((0, 8 - SSD_CONV_K), (0, 0))),
        "conv_b": conv_b.reshape(1, SSD_CONV_CH),
        "dt_bias": _pad_lanes(dt_bias, CHUNK).reshape(2, 1, CHUNK),
        "a_exp": _pad_lanes(jnp.exp(a_log), CHUNK).reshape(2, 1, CHUNK),
        "d_skip": jnp.repeat(d_skip, SSD_P).reshape(1, D_SSD),
        "norm_g": norm_g.reshape(1, D_SSD),
    }


def _s5_params(layer, k_re, k_im, p_re, p_im, b_re, b_im, c_re, c_im, d_skip, glu_w, glu_b):
    kr = k_re.reshape(2, S5_GROUPS, S5_STATE, 1)
    ki = k_im.reshape(2, S5_GROUPS, S5_STATE, 1)
    bb_re = kr * b_re - ki * b_im
    bb_im = kr * b_im + ki * b_re
    eye = jnp.eye(S5_GROUPS, dtype=F32)

    def blockdiag_in(bb):
        return jnp.einsum("gh,dgnc->dgchn", eye, bb).reshape(2, D_S5, S5_LANES)

    def blockdiag_out(cc):
        return jnp.einsum("gh,dgcn->dgnhc", eye, cc).reshape(2, S5_LANES, D_S5)

    wb = jnp.concatenate([blockdiag_in(bb_re), blockdiag_in(bb_im)], axis=2).astype(BF16)
    wc = jnp.concatenate([blockdiag_out(c_re), blockdiag_out(-c_im)], axis=1).astype(BF16)
    return {"layer": layer, "wb": wb, "wc": wc, "p_re": p_re, "p_im": p_im,
            "d_skip": d_skip.reshape(1, D_S5), "glu_w": glu_w.astype(BF16), "glu_b": glu_b.reshape(1, D_S5)}


def _ssd_state_in(s):
    b = s.shape[0]
    s = s.reshape(b, 2, SSD_GROUPS, SSD_HPG, SSD_P, SSD_N)
    return jnp.transpose(s, (0, 1, 2, 5, 3, 4)).reshape(b, 2, SSD_GROUPS, SSD_N, SSD_HPG * SSD_P)


def kernel(x_prompt, x_sample, c, state_ssd, state_s5_re, state_s5_im, state_ret, c_ctx, ada_w, ada_b, norm1_g, norm2_g, w_in, w_out, ssd_conv_w, ssd_conv_b, ssd_dt_bias, ssd_a_log, ssd_d, ssd_norm_g, s5_lambda_re, s5_lambda_im, s5_log_dt, s5_b_re, s5_b_im, s5_c_re, s5_c_im, s5_d, s5_glu_w, s5_glu_b, ret_gn_g, peer_wq, peer_keys, peer_u, peer_v, final_norm_g):
    nc_ctx = CTX_LEN // CHUNK
    nc_dec = DEC_LEN // CHUNK
    dec_row_off = N_CTX_TOK // CHUNK

    cond = jnp.concatenate([c_ctx[None, :], c, jnp.zeros((N_COND - 1 - N_DEC_SEQ, D_MODEL), F32)], axis=0)
    mod3 = _adaln(cond, ada_w, ada_b).reshape(DEPTH * N_COND * N_MOD, 1, D_MODEL)
    p_re, p_im, k_re, k_im = _s5_prep(s5_lambda_re, s5_lambda_im, s5_log_dt)

    x = jnp.concatenate([x_prompt.reshape(N_CTX_TOK, D_MODEL), x_sample.reshape(N_TOK - N_CTX_TOK, D_MODEL)], axis=0)
    new_ssd, new_re, new_im, new_ret = [], [], [], []
    for l in range(DEPTH):
        proj = _inproj(x, mod3, l, norm1_g[l], _permute_w_in(w_in[l]))

        sp = _ssd_params(ssd_conv_w[l], ssd_conv_b[l], ssd_dt_bias[l], ssd_a_log[l], ssd_d[l], ssd_norm_g[l])
        y_ssd_c, st_ssd = _ssd(proj, sp, N_CTX_SEQ, nc_ctx, 0, None)
        y_ssd_d, _ = _ssd(proj, sp, N_DEC_SEQ, nc_dec, dec_row_off, _ssd_state_in(state_ssd[:, l]))

        s5p = _s5_params(l, k_re[2 * l:2 * l + 2], k_im[2 * l:2 * l + 2], p_re, p_im, s5_b_re[l], s5_b_im[l],
                         s5_c_re[l], s5_c_im[l], s5_d[l], s5_glu_w[l], s5_glu_b[l])
        y_s5_c, st_re, st_im = _s5(proj, s5p, N_CTX_SEQ, nc_ctx, 0, None)
        x0 = (state_s5_re[:, l].reshape(N_DEC_SEQ, 2, 1, S5_LANES), state_s5_im[:, l].reshape(N_DEC_SEQ, 2, 1, S5_LANES))
        y_s5_d, _, _ = _s5(proj, s5p, N_DEC_SEQ, nc_dec, dec_row_off, x0)

        y_ret_c, st_ret = _ret(proj, ret_gn_g[l], N_CTX_SEQ, nc_ctx, 0, None, rope=False)
        y_ret_d, _ = _ret(proj, ret_gn_g[l], N_DEC_SEQ, nc_dec, dec_row_off,
                          jnp.swapaxes(state_ret[:, l], -1, -2), rope=True)

        x = _outproj(x, jnp.concatenate([y_ssd_c, y_ssd_d]), jnp.concatenate([y_s5_c, y_s5_d]),
                     jnp.concatenate([y_ret_c, y_ret_d]), w_out[l].astype(BF16), mod3, l)

        hb, thr, g1, s2, e2 = _peer_scores(x, mod3, l, norm2_g[l], peer_wq[l].astype(BF16), peer_keys[l].astype(BF16))
        x = _peer_experts(x, mod3, l, hb, thr, g1, s2, e2, peer_u[l].astype(BF16), peer_v[l].T.astype(BF16))

        new_ssd.append(st_ssd)
        new_re.append(st_re.reshape(N_CTX_SEQ, 2, S5_GROUPS, S5_STATE))
        new_im.append(st_im.reshape(N_CTX_SEQ, 2, S5_GROUPS, S5_STATE))
        new_ret.append(st_ret)

    y_ctx, y_dec = _final_norm(x, final_norm_g)
    y_prompt = y_ctx.reshape(N_CTX_SEQ, CTX_LEN, D_MODEL)
    y_sample = y_dec.reshape(N_DEC_SEQ, DEC_LEN, D_MODEL)
    return (y_prompt, y_sample, jnp.stack(new_ssd, axis=1), jnp.stack(new_re, axis=1),
            jnp.stack(new_im, axis=1), jnp.stack(new_ret, axis=1))
```

```python
import functools
import math

import jax
import jax.numpy as jnp
import numpy as np
from jax import lax
from jax.experimental import pallas as pl
from jax.experimental.pallas import tpu as pltpu

F32 = jnp.float32
BF16 = jnp.bfloat16

D_MODEL = 2048
N_CTX_SEQ = 16
CTX_LEN = 256
N_DEC_SEQ = 2
DEC_LEN = 2048
N_CTX_TOK = N_CTX_SEQ * CTX_LEN
N_TOK = N_CTX_TOK + N_DEC_SEQ * DEC_LEN
DEPTH = 2
GRID_W = 64
CHUNK = 128

SSD_HEADS = 12
SSD_P = 64
D_SSD = SSD_HEADS * SSD_P
SSD_N = 128
SSD_GROUPS = 2
SSD_HPG = SSD_HEADS // SSD_GROUPS
SSD_CONV_K = 5
SSD_CONV_CH = D_SSD + 2 * SSD_GROUPS * SSD_N
S5_CH = 16
S5_GROUPS = 32
D_S5 = S5_CH * S5_GROUPS
S5_STATE = 64
S5_LANES = S5_GROUPS * S5_STATE
S5_TILE = 8
RET_HEADS = 6
RET_DK = 128
RET_DV = 128
D_RET = RET_HEADS * RET_DV
ROPE_BASE = 10000.0
PEER_HEADS = 8
PEER_DQ = 256
PEER_NKEYS = 128
PEER_EXPERTS = PEER_NKEYS * PEER_NKEYS
PEER_TOPK = 16
N_MOD = 6
N_COND = 8
EPS = 1e-6

COL_Z, COL_Q, COL_K, COL_V, COL_G = 0, 768, 1536, 2304, 3072
COL_XBC = 3840
COL_U = 5120
COL_DT = 5632
PROJ_W = 6144

VMEM_LIMIT = 56 * 1024 * 1024

NEG_INF = float("-inf")


def _cparams(sem, vmem_limit=VMEM_LIMIT):
    return pltpu.CompilerParams(dimension_semantics=sem, vmem_limit_bytes=vmem_limit)


def _split3(a):
    hi = a.astype(BF16)
    r1 = a - hi.astype(F32)
    mid = r1.astype(BF16)
    lo = (r1 - mid.astype(F32)).astype(BF16)
    return hi, mid, lo


def _dot(a, b):
    return jnp.dot(a, b, preferred_element_type=F32)


def _dot_split_lhs(a, b_exact):
    hi, mid, lo = _split3(a)
    return _dot(hi, b_exact) + _dot(mid, b_exact) + _dot(lo, b_exact)


def _dot_split_rhs(a_exact, b):
    hi, mid, lo = _split3(b)
    return _dot(a_exact, hi) + _dot(a_exact, mid) + _dot(a_exact, lo)


def _dot_nt(a, b):
    return lax.dot_general(a, b, (((1,), (1,)), ((), ())), preferred_element_type=F32)


def _silu(x):
    return x * jax.nn.sigmoid(x)


def _softplus(x):
    return jnp.maximum(x, 0.0) + jnp.log1p(jnp.exp(-jnp.abs(x)))


ADA_TN = 1536


def _adaln_kernel(c_ref, w_ref, b_ref, o_ref):
    s = _silu(c_ref[...]).astype(BF16)
    o_ref[...] = _dot(s, w_ref[...].astype(BF16)) + b_ref[...]


def _adaln(cond, ada_w, ada_b):
    n_out = N_MOD * D_MODEL
    return pl.pallas_call(
        _adaln_kernel,
        grid=(DEPTH, n_out // ADA_TN),
        in_specs=[
            pl.BlockSpec((N_COND, D_MODEL), lambda l, j: (0, 0)),
            pl.BlockSpec((None, D_MODEL, ADA_TN), lambda l, j: (l, 0, j)),
            pl.BlockSpec((None, 1, ADA_TN), lambda l, j: (l, 0, j)),
        ],
        out_specs=pl.BlockSpec((None, N_COND, ADA_TN), lambda l, j: (l, 0, j)),
        out_shape=jax.ShapeDtypeStruct((DEPTH, N_COND, n_out), F32),
        compiler_params=_cparams(("arbitrary", "arbitrary")),
        name="adaln",
    )(cond, ada_w, ada_b.reshape(DEPTH, 1, n_out))


def _mod_spec(layer, which, tm):
    n_ctx_tiles = N_CTX_TOK // tm
    tiles_per_dec = DEC_LEN // tm

    def index(i, *_):
        cond = jnp.where(i < n_ctx_tiles, 0, 1 + (i - n_ctx_tiles) // tiles_per_dec)
        return ((layer * N_COND + cond) * N_MOD + which, 0, 0)

    return pl.BlockSpec((None, 1, D_MODEL), index)


def _rms_modulate(x, g, sc, sh):
    var = jnp.mean(x * x, axis=-1, keepdims=True)
    y = x * lax.rsqrt(var + EPS) * g
    return y * (1.0 + sc) + sh


INPROJ_TM = 256


def _inproj_kernel(x_ref, sh_ref, sc_ref, g_ref, w_ref, o_ref):
    h = _rms_modulate(x_ref[...], g_ref[...], sc_ref[...], sh_ref[...]).astype(BF16)
    o_ref[...] = _dot(h, w_ref[...])


def _inproj(x, mod3, layer, norm_g, w_perm):
    tm = INPROJ_TM
    return pl.pallas_call(
        _inproj_kernel,
        grid=(N_TOK // tm,),
        in_specs=[
            pl.BlockSpec((tm, D_MODEL), lambda i: (i, 0)),
            _mod_spec(layer, 0, tm),
            _mod_spec(layer, 1, tm),
            pl.BlockSpec((1, D_MODEL), lambda i: (0, 0)),
            pl.BlockSpec((None, D_MODEL, PROJ_W), lambda i: (layer, 0, 0), pipeline_mode=pl.Buffered(1)),
        ],
        out_specs=pl.BlockSpec((tm, PROJ_W), lambda i: (i, 0)),
        out_shape=jax.ShapeDtypeStruct((N_TOK, PROJ_W), F32),
        compiler_params=_cparams(("arbitrary",)),
        name="inproj",
    )(x, mod3, mod3, norm_g.reshape(1, D_MODEL), w_perm)


def _walk(nc):
    def direction(t):
        return jnp.where(t < nc, 1, 0)

    def chunk(t):
        return jnp.where(t < nc, nc - 1 - t, t - nc)

    return direction, chunk


def _seq_specs(nc, row_off):
    direction, chunk = _walk(nc)

    def rows(s, t):
        return row_off + s * nc + chunk(t)

    def out_rows(s, t):
        return row_off + s * nc + jnp.where(t < nc, 0, t - nc)

    return direction, chunk, rows, out_rows


def _ssd_consts():
    idx = np.arange(CHUNK)
    tri = np.stack([(idx[None, :] <= idx[:, None]), (idx[None, :] >= idx[:, None])]).astype(np.float32)
    e_p = np.zeros((CHUNK, D_SSD), np.float32)
    e_n = np.zeros((CHUNK, SSD_HEADS * CHUNK), np.float32)
    for h in range(SSD_HEADS):
        e_p[h, h * SSD_P:(h + 1) * SSD_P] = 1.0
        e_n[h, h * CHUNK:(h + 1) * CHUNK] = 1.0
    lane = np.arange(CHUNK)
    pair = np.concatenate([np.broadcast_to(lane < SSD_P, (CHUNK, CHUNK)),
                           np.broadcast_to(lane >= SSD_P, (CHUNK, CHUNK))]).astype(np.float32)
    return tri, e_p, e_n, pair


def _ssd_kernel(nc, zero_init, emit_state, *refs):
    (z_ref, xc_ref, xp_ref, xn_ref, dt_ref, cw_ref, cb_ref, dtb_ref, aexp_ref, dsk_ref, ng_ref,
     tri_ref, ep_ref, en_ref, pair_ref) = refs[:15]
    pos = 15
    s0_ref = None
    if not zero_init:
        s0_ref = refs[pos]
        pos += 1
    y_ref = refs[pos]
    pos += 1
    st_ref = None
    if emit_state:
        st_ref = refs[pos]
        pos += 1
    state_scr, stash_scr = refs[pos:]

    t = pl.program_id(1)
    fwd = t >= nc
    step = jnp.where(fwd, t - nc, t)
    chunk = jnp.where(fwd, t - nc, nc - 1 - t)
    row0 = pl.multiple_of(chunk * CHUNK, CHUNK)

    @pl.when(step == 0)
    def _():
        if zero_init:
            state_scr[...] = jnp.zeros_like(state_scr)
        else:
            state_scr[...] = s0_ref[...]

    prev = jnp.where(chunk > 0, xp_ref[...], 0.0)
    nxt = jnp.where(chunk < nc - 1, xn_ref[...], 0.0)
    ext = jnp.concatenate([prev, xc_ref[...], nxt], axis=0)
    conv = cb_ref[...]
    for k in range(SSD_CONV_K):
        off = 8 + k - SSD_CONV_K // 2
        conv = conv + cw_ref[k:k + 1, :] * ext[off:off + CHUNK, :]
    xbc = _silu(conv)
    x = xbc[:, :D_SSD]
    bm = xbc[:, D_SSD:D_SSD + SSD_GROUPS * SSD_N]
    cm = xbc[:, D_SSD + SSD_GROUPS * SSD_N:]

    dt = _softplus(dt_ref[...] + dtb_ref[...])
    la = -dt * aexp_ref[...]
    tri = tri_ref[...]
    cum = _dot_split_rhs(tri.astype(BF16), la)
    ep = ep_ref[...]
    cum_p = _dot_split_lhs(cum, ep)
    dt_p = _dot_split_lhs(dt, ep)
    cum_col = _dot_split_lhs(cum, en_ref[...])
    cum_t = cum.T
    tot_p = jnp.where(fwd, cum_p[CHUNK - 1:CHUNK, :], cum_p[0:1, :])

    v = x * dt_p
    vb = v.astype(BF16)
    mask = tri > 0.5
    pair = pair_ref[...]
    y_parts = []
    for g in range(SSD_GROUPS):
        cg = cm[:, g * SSD_N:(g + 1) * SSD_N].astype(BF16)
        bg = bm[:, g * SSD_N:(g + 1) * SSD_N]
        gmat = _dot_nt(cg, bg.astype(BF16))
        for hp in range(SSD_HPG // 2):
            scs = []
            for h in (g * SSD_HPG + 2 * hp, g * SSD_HPG + 2 * hp + 1):
                ci = cum_col[:, h * CHUNK:(h + 1) * CHUNK]
                cj = cum_t[h:h + 1, :]
                dec = jnp.exp(jnp.where(mask, ci - cj, NEG_INF))
                scs.append((gmat * dec).astype(BF16))
            c0 = (g * SSD_HPG + 2 * hp) * SSD_P
            v2 = vb[:, c0:c0 + 2 * SSD_P]
            vv = jnp.concatenate([v2, v2], axis=0) * pair
            y_parts.append(_dot(jnp.concatenate(scs, axis=1), vv))
    y = jnp.concatenate(y_parts, axis=1)

    w_p = SSD_HPG * SSD_P
    y_off = jnp.concatenate(
        [_dot(cm[:, g * SSD_N:(g + 1) * SSD_N].astype(BF16), state_scr[g].astype(BF16))
         for g in range(SSD_GROUPS)], axis=1)
    y = y + y_off * jnp.exp(cum_p)

    vw = (v * jnp.exp(tot_p - cum_p)).astype(BF16)
    cdec = jnp.exp(tot_p)
    for g in range(SSD_GROUPS):
        bt = bm[:, g * SSD_N:(g + 1) * SSD_N].T.astype(BF16)
        state_scr[g] = state_scr[g] * cdec[:, g * w_p:(g + 1) * w_p] + _dot(bt, vw[:, g * w_p:(g + 1) * w_p])

    @pl.when(jnp.logical_not(fwd))
    def _():
        stash_scr[pl.ds(row0, CHUNK), :] = y

    @pl.when(fwd)
    def _():
        ytot = y + stash_scr[pl.ds(row0, CHUNK), :] + x * dsk_ref[...]
        gated = ytot * _silu(z_ref[...])
        var = jnp.mean(gated * gated, axis=-1, keepdims=True)
        y_ref[...] = (gated * lax.rsqrt(var + EPS) * ng_ref[...]).astype(y_ref.dtype)

    if emit_state:
        @pl.when(step == nc - 1)
        def _():
            for g in range(SSD_GROUPS):
                st_t = state_scr[g].T
                for k in range(SSD_HPG):
                    st_ref[g * SSD_HPG + k] = st_t[k * SSD_P:(k + 1) * SSD_P, :]


def _ssd(proj, p, n_seq, nc, row_off, s0):
    zero_init = s0 is None
    emit_state = zero_init
    direction, chunk, rows, out_rows = _seq_specs(nc, row_off)
    tri, e_p, e_n, pair = _ssd_consts()
    n8 = N_TOK // 8
    w_st = SSD_HPG * SSD_P
    in_specs = [
        pl.BlockSpec((CHUNK, D_SSD), lambda s, t: (rows(s, t), COL_Z // D_SSD)),
        pl.BlockSpec((CHUNK, SSD_CONV_CH), lambda s, t: (rows(s, t), COL_XBC // SSD_CONV_CH)),
        pl.BlockSpec((8, SSD_CONV_CH),
                     lambda s, t: (jnp.maximum(rows(s, t) * (CHUNK // 8) - 1, 0), COL_XBC // SSD_CONV_CH)),
        pl.BlockSpec((8, SSD_CONV_CH),
                     lambda s, t: (jnp.minimum((rows(s, t) + 1) * (CHUNK // 8), n8 - 1), COL_XBC // SSD_CONV_CH)),
        pl.BlockSpec((CHUNK, CHUNK), lambda s, t: (rows(s, t), COL_DT // CHUNK + direction(t))),
        pl.BlockSpec((8, SSD_CONV_CH), lambda s, t: (0, 0)),
        pl.BlockSpec((1, SSD_CONV_CH), lambda s, t: (0, 0)),
        pl.BlockSpec((None, 1, CHUNK), lambda s, t: (direction(t), 0, 0)),
        pl.BlockSpec((None, 1, CHUNK), lambda s, t: (direction(t), 0, 0)),
        pl.BlockSpec((1, D_SSD), lambda s, t: (0, 0)),
        pl.BlockSpec((1, D_SSD), lambda s, t: (0, 0)),
        pl.BlockSpec((None, CHUNK, CHUNK), lambda s, t: (direction(t), 0, 0)),
        pl.BlockSpec((CHUNK, D_SSD), lambda s, t: (0, 0)),
        pl.BlockSpec((CHUNK, SSD_HEADS * CHUNK), lambda s, t: (0, 0)),
        pl.BlockSpec((2 * CHUNK, CHUNK), lambda s, t: (0, 0)),
    ]
    args = [proj, proj, proj, proj, proj, p["conv_w"], p["conv_b"], p["dt_bias"], p["a_exp"], p["d_skip"],
            p["norm_g"], jnp.asarray(tri), jnp.asarray(e_p, BF16), jnp.asarray(e_n, BF16), jnp.asarray(pair, BF16)]
    st_spec = pl.BlockSpec((None, None, SSD_GROUPS, SSD_N, w_st), lambda s, t: (s, direction(t), 0, 0, 0))
    if not zero_init:
        in_specs.append(st_spec)
        args.append(s0)
    out_specs = [pl.BlockSpec((CHUNK, D_SSD), lambda s, t: (out_rows(s, t) - row_off, 0))]
    out_shape = [jax.ShapeDtypeStruct((n_seq * nc * CHUNK, D_SSD), BF16)]
    if emit_state:
        out_specs.append(pl.BlockSpec((None, None, SSD_HEADS, SSD_P, SSD_N), lambda s, t: (s, direction(t), 0, 0, 0)))
        out_shape.append(jax.ShapeDtypeStruct((n_seq, 2, SSD_HEADS, SSD_P, SSD_N), F32))
    res = pl.pallas_call(
        functools.partial(_ssd_kernel, nc, zero_init, emit_state),
        grid=(n_seq, 2 * nc),
        in_specs=in_specs,
        out_specs=out_specs,
        out_shape=out_shape,
        scratch_shapes=[pltpu.VMEM((SSD_GROUPS, SSD_N, w_st), F32), pltpu.VMEM((nc * CHUNK, D_SSD), F32)],
        compiler_params=_cparams(("arbitrary", "arbitrary")),
        name="ssd_ctx" if zero_init else "ssd_dec",
    )(*args)
    return res if emit_state else (res[0], None)


def _s5_prep_kernel(lr_ref, li_ref, ldt_ref, pre_ref, pim_ref, kre_ref, kim_ref):
    lr = lr_ref[...]
    li = li_ref[...]
    dt = jnp.exp(ldt_ref[...])
    kk = (lax.broadcasted_iota(jnp.int32, (S5_TILE, S5_LANES), 0) + 1).astype(F32)
    mag = jnp.exp(kk * (lr * dt))
    ang = kk * (li * dt)
    p_re = mag * jnp.cos(ang)
    p_im = mag * jnp.sin(ang)
    pre_ref[...] = p_re
    pim_ref[...] = p_im
    a_re = p_re[0:1, :]
    a_im = p_im[0:1, :]
    den = lr * lr + li * li
    num_re = a_re - 1.0
    kre_ref[...] = (num_re * lr + a_im * li) / den
    kim_ref[...] = (a_im * lr - num_re * li) / den


def _s5_prep(lam_re, lam_im, log_dt):
    n = DEPTH * 2
    lr = lam_re.reshape(n, 1, S5_LANES)
    li = lam_im.reshape(n, 1, S5_LANES)
    ldt = jnp.broadcast_to(log_dt[..., None], (DEPTH, 2, S5_GROUPS, S5_STATE)).reshape(n, 1, S5_LANES)
    row = pl.BlockSpec((None, 1, S5_LANES), lambda i: (i, 0, 0))
    tab = pl.BlockSpec((None, S5_TILE, S5_LANES), lambda i: (i, 0, 0))
    return pl.pallas_call(
        _s5_prep_kernel,
        grid=(n,),
        in_specs=[row, row, row],
        out_specs=[tab, tab, row, row],
        out_shape=[jax.ShapeDtypeStruct((n, S5_TILE, S5_LANES), F32)] * 2
        + [jax.ShapeDtypeStruct((n, 1, S5_LANES), F32)] * 2,
        compiler_params=_cparams(("arbitrary",)),
        name="s5_prep",
    )(lr, li, ldt)


def _s5_kernel(nc, zero_init, emit_state, *refs):
    (u_ref, jm_ref, wb_ref, pre_ref, pim_ref, wc_ref, dsk_ref, gw_ref, gb_ref) = refs[:9]
    pos = 9
    x0re_ref = x0im_ref = None
    if not zero_init:
        x0re_ref, x0im_ref = refs[pos:pos + 2]
        pos += 2
    y_ref = refs[pos]
    pos += 1
    sre_ref = sim_ref = None
    if emit_state:
        sre_ref, sim_ref = refs[pos:pos + 2]
        pos += 2
    cre_scr, cim_scr, stash_scr = refs[pos:]

    t = pl.program_id(1)
    fwd = t >= nc
    step = jnp.where(fwd, t - nc, t)
    chunk = jnp.where(fwd, t - nc, nc - 1 - t)
    row0 = pl.multiple_of(chunk * CHUNK, CHUNK)

    @pl.when(step == 0)
    def _():
        if zero_init:
            cre_scr[...] = jnp.zeros_like(cre_scr)
            cim_scr[...] = jnp.zeros_like(cim_scr)
        else:
            cre_scr[...] = x0re_ref[...]
            cim_scr[...] = x0im_ref[...]

    u = u_ref[...]
    jm = jm_ref[...]
    us = _dot(jm, u.astype(BF16)).astype(BF16)
    bu = _dot(us, wb_ref[...])
    xr = bu[:, :S5_LANES]
    xi = bu[:, S5_LANES:]
    cr = cre_scr[...]
    ci = cim_scr[...]
    pr = pre_ref[...]
    pi = pim_ref[...]
    sub = lax.broadcasted_iota(jnp.int32, (S5_TILE, S5_LANES), 0)
    steps = []
    d = 1
    while d < S5_TILE:
        keep = sub >= d
        steps.append((d, jnp.where(keep, pre_ref[d - 1:d, :], 0.0), jnp.where(keep, pim_ref[d - 1:d, :], 0.0)))
        d *= 2
    tiles_r, tiles_i = [], []
    for b in range(CHUNK // S5_TILE):
        br = xr[b * S5_TILE:(b + 1) * S5_TILE, :]
        bi = xi[b * S5_TILE:(b + 1) * S5_TILE, :]
        for d, ar, ai in steps:
            sr = pltpu.roll(br, d, axis=0)
            si = pltpu.roll(bi, d, axis=0)
            br, bi = br + (ar * sr - ai * si), bi + (ar * si + ai * sr)
        br, bi = br + (pr * cr - pi * ci), bi + (pr * ci + pi * cr)
        cr = br[S5_TILE - 1:S5_TILE, :]
        ci = bi[S5_TILE - 1:S5_TILE, :]
        tiles_r.append(br)
        tiles_i.append(bi)
    xr = jnp.concatenate(tiles_r, axis=0)
    xi = jnp.concatenate(tiles_i, axis=0)
    cre_scr[...] = cr
    cim_scr[...] = ci

    ys = _dot(xr.astype(BF16), wc_ref[:S5_LANES, :]) + _dot(xi.astype(BF16), wc_ref[S5_LANES:, :])
    yh = ys.astype(BF16)
    yl = (ys - yh.astype(F32)).astype(BF16)
    y = _dot(jm, yh) + _dot(jm, yl)

    @pl.when(jnp.logical_not(fwd))
    def _():
        stash_scr[pl.ds(row0, CHUNK), :] = y

    @pl.when(fwd)
    def _():
        yt = y + stash_scr[pl.ds(row0, CHUNK), :] + u * dsk_ref[...]
        yt = jax.nn.gelu(yt)
        gate = jax.nn.sigmoid(_dot(yt.astype(BF16), gw_ref[...]) + gb_ref[...])
        y_ref[...] = (yt * gate).astype(y_ref.dtype)

    if emit_state:
        @pl.when(step == nc - 1)
        def _():
            sre_ref[...] = cre_scr[...]
            sim_ref[...] = cim_scr[...]


def _s5(proj, p, n_seq, nc, row_off, x0):
    zero_init = x0 is None
    emit_state = zero_init
    direction, chunk, rows, out_rows = _seq_specs(nc, row_off)
    eye = np.eye(CHUNK, dtype=np.float32)
    jm = jnp.asarray(np.stack([eye, eye[::-1]]), BF16)
    lyr = p["layer"]

    def dsel(t):
        return lyr * 2 + direction(t)

    in_specs = [
        pl.BlockSpec((CHUNK, D_S5), lambda s, t: (rows(s, t), COL_U // D_S5)),
        pl.BlockSpec((None, CHUNK, CHUNK), lambda s, t: (direction(t), 0, 0)),
        pl.BlockSpec((None, D_S5, 2 * S5_LANES), lambda s, t: (direction(t), 0, 0)),
        pl.BlockSpec((None, S5_TILE, S5_LANES), lambda s, t: (dsel(t), 0, 0)),
        pl.BlockSpec((None, S5_TILE, S5_LANES), lambda s, t: (dsel(t), 0, 0)),
        pl.BlockSpec((None, 2 * S5_LANES, D_S5), lambda s, t: (direction(t), 0, 0)),
        pl.BlockSpec((1, D_S5), lambda s, t: (0, 0)),
        pl.BlockSpec((D_S5, D_S5), lambda s, t: (0, 0)),
        pl.BlockSpec((1, D_S5), lambda s, t: (0, 0)),
    ]
    args = [proj, jm, p["wb"], p["p_re"], p["p_im"], p["wc"], p["d_skip"], p["glu_w"], p["glu_b"]]
    st_spec = pl.BlockSpec((None, None, 1, S5_LANES), lambda s, t: (s, direction(t), 0, 0))
    if not zero_init:
        in_specs += [st_spec, st_spec]
        args += [x0[0], x0[1]]
    out_specs = [pl.BlockSpec((CHUNK, D_S5), lambda s, t: (out_rows(s, t) - row_off, 0))]
    out_shape = [jax.ShapeDtypeStruct((n_seq * nc * CHUNK, D_S5), BF16)]
    if emit_state:
        out_specs += [st_spec, st_spec]
        out_shape += [jax.ShapeDtypeStruct((n_seq, 2, 1, S5_LANES), F32)] * 2
    res = pl.pallas_call(
        functools.partial(_s5_kernel, nc, zero_init, emit_state),
        grid=(n_seq, 2 * nc),
        in_specs=in_specs,
        out_specs=out_specs,
        out_shape=out_shape,
        scratch_shapes=[pltpu.VMEM((1, S5_LANES), F32), pltpu.VMEM((1, S5_LANES), F32),
                        pltpu.VMEM((nc * CHUNK, D_S5), F32)],
        compiler_params=_cparams(("arbitrary", "arbitrary")),
        name="s5_ctx" if zero_init else "s5_dec",
    )(*args)
    return res if emit_state else (res[0], None, None)


def _ret_consts():
    heads = np.arange(RET_HEADS, dtype=np.float64)
    lg = np.stack([np.log1p(-np.exp2(-5.0 - heads)), np.log1p(-np.exp2(-5.5 - heads))])
    i = np.arange(CHUNK, dtype=np.float64)
    diff = i[:, None] - i[None, :]
    dmat = np.zeros((2, RET_HEADS, CHUNK, CHUNK))
    rowdec = np.zeros((2, CHUNK, D_RET))
    wend = np.zeros((2, 8, CHUNK))
    cdec = np.zeros((2, 8, CHUNK))
    for h in range(RET_HEADS):
        dmat[0, h] = np.where(diff >= 0, np.exp(lg[0, h] * diff), 0.0)
        dmat[1, h] = np.where(diff <= 0, np.exp(-lg[1, h] * diff), 0.0)
        rowdec[0, :, h * RET_DV:(h + 1) * RET_DV] = np.exp(lg[0, h] * (i + 1))[:, None]
        rowdec[1, :, h * RET_DV:(h + 1) * RET_DV] = np.exp(lg[1, h] * (CHUNK - i))[:, None]
        wend[0, h] = np.exp(lg[0, h] * (CHUNK - 1 - i))
        wend[1, h] = np.exp(lg[1, h] * i)
        cdec[:, h] = np.exp(lg[:, h] * CHUNK)[:, None]
    return [jnp.asarray(a, F32) for a in (dmat, rowdec, wend, cdec)]


def _rope_tables():
    t = np.arange(DEC_LEN)
    row = (t // GRID_W).astype(np.float32)
    col = (t % GRID_W).astype(np.float32)
    quarter = RET_DK // 4
    freqs = (ROPE_BASE ** (-np.arange(quarter, dtype=np.float32) / quarter)).astype(np.float32)
    ar = (row[:, None] * freqs[None, :]).astype(np.float64)
    ac = (col[:, None] * freqs[None, :]).astype(np.float64)
    cos = np.concatenate([np.cos(ar), np.cos(ar), np.cos(ac), np.cos(ac)], axis=1)
    sin = np.concatenate([-np.sin(ar), np.sin(ar), -np.sin(ac), np.sin(ac)], axis=1)
    return jnp.asarray(cos, F32), jnp.asarray(sin, F32)


def _ret_kernel(nc, zero_init, emit_state, rope, *refs):
    q_ref, k_ref, v_ref, g_ref = refs[:4]
    pos = 4
    cos_ref = sin_ref = None
    if rope:
        cos_ref, sin_ref = refs[pos:pos + 2]
        pos += 2
    dmat_ref, rowdec_ref, wend_ref, cdec_ref, gn_ref = refs[pos:pos + 5]
    pos += 5
    s0_ref = None
    if not zero_init:
        s0_ref = refs[pos]
        pos += 1
    y_ref = refs[pos]
    pos += 1
    st_ref = None
    if emit_state:
        st_ref = refs[pos]
        pos += 1
    state_scr, stash_scr = refs[pos:]

    t = pl.program_id(1)
    fwd = t >= nc
    step = jnp.where(fwd, t - nc, t)
    chunk = jnp.where(fwd, t - nc, nc - 1 - t)
    row0 = pl.multiple_of(chunk * CHUNK, CHUNK)

    @pl.when(step == 0)
    def _():
        if zero_init:
            state_scr[...] = jnp.zeros_like(state_scr)
        else:
            state_scr[...] = s0_ref[...]

    q = q_ref[...]
    k = k_ref[...]
    if rope:
        cos = jnp.concatenate([cos_ref[...]] * RET_HEADS, axis=1)
        sin = jnp.concatenate([sin_ref[...]] * RET_HEADS, axis=1)
        lane = lax.broadcasted_iota(jnp.int32, (CHUNK, D_RET), 1)
        first = (lane // (RET_DK // 4)) % 2 == 0

        def rot(x):
            partner = jnp.where(first, pltpu.roll(x, D_RET - RET_DK // 4, axis=1), pltpu.roll(x, RET_DK // 4, axis=1))
            return x * cos + partner * sin

        q = rot(q)
        k = rot(k)
    k = k * (RET_DK ** -0.5)
    qb = q.astype(BF16)
    vb = v_ref[...].astype(BF16)
    rowdec = rowdec_ref[...]
    y_parts = []
    for h in range(RET_HEADS):
        sl = slice(h * RET_DK, (h + 1) * RET_DK)
        kh = k[:, sl]
        sc = _dot_nt(qb[:, sl], kh.astype(BF16)) * dmat_ref[h]
        yh = _dot(sc.astype(BF16), vb[:, sl])
        yh = yh + _dot(qb[:, sl], state_scr[h].astype(BF16)) * rowdec[:, sl]
        y_parts.append(yh)
        kt = (kh.T * wend_ref[h:h + 1, :]).astype(BF16)
        state_scr[h] = state_scr[h] * cdec_ref[h:h + 1, :] + _dot(kt, vb[:, sl])
    y = jnp.concatenate(y_parts, axis=1)

    @pl.when(jnp.logical_not(fwd))
    def _():
        stash_scr[pl.ds(row0, CHUNK), :] = y

    @pl.when(fwd)
    def _():
        yt = y + stash_scr[pl.ds(row0, CHUNK), :]
        outs = []
        for h in range(RET_HEADS):
            yh = yt[:, h * RET_DV:(h + 1) * RET_DV]
            yc = yh - jnp.mean(yh, axis=-1, keepdims=True)
            outs.append(yc * lax.rsqrt(jnp.mean(yc * yc, axis=-1, keepdims=True) + EPS))
        yn = jnp.concatenate(outs, axis=1)
        y_ref[...] = (yn * gn_ref[...] * _silu(g_ref[...])).astype(y_ref.dtype)

    if emit_state:
        @pl.when(step == nc - 1)
        def _():
            for h in range(RET_HEADS):
                st_ref[h] = state_scr[h].T


def _ret(proj, gn_g, n_seq, nc, row_off, s0, rope):
    zero_init = s0 is None
    emit_state = zero_init
    direction, chunk, rows, out_rows = _seq_specs(nc, row_off)
    dmat, rowdec, wend, cdec = _ret_consts()

    def col(c):
        return pl.BlockSpec((CHUNK, D_RET), lambda s, t: (rows(s, t), c // D_RET))

    in_specs = [col(COL_Q), col(COL_K), col(COL_V), col(COL_G)]
    args = [proj, proj, proj, proj]
    if rope:
        cos, sin = _rope_tables()
        tab = pl.BlockSpec((CHUNK, RET_DK), lambda s, t: (chunk(t), 0))
        in_specs += [tab, tab]
        args += [cos, sin]
    in_specs += [
        pl.BlockSpec((None, RET_HEADS, CHUNK, CHUNK), lambda s, t: (direction(t), 0, 0, 0)),
        pl.BlockSpec((None, CHUNK, D_RET), lambda s, t: (direction(t), 0, 0)),
        pl.BlockSpec((None, 8, CHUNK), lambda s, t: (direction(t), 0, 0)),
        pl.BlockSpec((None, 8, CHUNK), lambda s, t: (direction(t), 0, 0)),
        pl.BlockSpec((1, D_RET), lambda s, t: (0, 0)),
    ]
    args += [dmat, rowdec, wend, cdec, gn_g.reshape(1, D_RET)]
    st_spec = pl.BlockSpec((None, None, RET_HEADS, RET_DK, RET_DV), lambda s, t: (s, direction(t), 0, 0, 0))
    if not zero_init:
        in_specs.append(st_spec)
        args.append(s0)
    out_specs = [pl.BlockSpec((CHUNK, D_RET), lambda s, t: (out_rows(s, t) - row_off, 0))]
    out_shape = [jax.ShapeDtypeStruct((n_seq * nc * CHUNK, D_RET), BF16)]
    if emit_state:
        out_specs.append(st_spec)
        out_shape.append(jax.ShapeDtypeStruct((n_seq, 2, RET_HEADS, RET_DK, RET_DV), F32))
    res = pl.pallas_call(
        functools.partial(_ret_kernel, nc, zero_init, emit_state, rope),
        grid=(n_seq, 2 * nc),
        in_specs=in_specs,
        out_specs=out_specs,
        out_shape=out_shape,
        scratch_shapes=[pltpu.VMEM((RET_HEADS, RET_DK, RET_DV), F32), pltpu.VMEM((nc * CHUNK, D_RET), F32)],
        compiler_params=_cparams(("arbitrary", "arbitrary")),
        name="ret_ctx" if zero_init else "ret_dec",
    )(*args)
    return res if emit_state else (res[0], None)


OUTPROJ_TM = 512


def _outproj_kernel(x_ref, ya_ref, yb_ref, yc_ref, w_ref, g_ref, o_ref):
    y = _dot(ya_ref[...], w_ref[:D_SSD, :])
    y = y + _dot(yb_ref[...], w_ref[D_SSD:D_SSD + D_S5, :])
    y = y + _dot(yc_ref[...], w_ref[D_SSD + D_S5:, :])
    o_ref[...] = x_ref[...] + g_ref[...] * y


def _outproj(x, y_ssd, y_s5, y_ret, w_out, mod3, layer):
    tm = OUTPROJ_TM
    return pl.pallas_call(
        _outproj_kernel,
        grid=(N_TOK // tm,),
        in_specs=[
            pl.BlockSpec((tm, D_MODEL), lambda i: (i, 0)),
            pl.BlockSpec((tm, D_SSD), lambda i: (i, 0)),
            pl.BlockSpec((tm, D_S5), lambda i: (i, 0)),
            pl.BlockSpec((tm, D_RET), lambda i: (i, 0)),
            pl.BlockSpec((None, D_MODEL, D_MODEL), lambda i: (layer, 0, 0)),
            _mod_spec(layer, 2, tm),
        ],
        out_specs=pl.BlockSpec((tm, D_MODEL), lambda i: (i, 0)),
        out_shape=jax.ShapeDtypeStruct((N_TOK, D_MODEL), F32),
        compiler_params=_cparams(("arbitrary",)),
        name="outproj",
    )(x, y_ssd, y_s5, y_ret, w_out, mod3)


PEER_SC_TM = 256
PEER_NCAND = PEER_TOPK + 1


def _cand_pairs():
    return [(i, j) for i in range(PEER_NCAND) for j in range(PEER_NCAND) if (i + 1) * (j + 1) <= PEER_NCAND]


def _top_rows(work, n):
    rows = []
    for r in range(n):
        m = jnp.max(work, axis=0, keepdims=True)
        rows.append(m)
        if r < n - 1:
            work = jnp.where(work >= m, NEG_INF, work)
    return rows


def _peer_scores_kernel(x_ref, sh_ref, sc_ref, g_ref, wq_ref, keys_ref,
                        h_ref, thr_ref, g1_ref, s2_ref, e2_ref, cand_scr):
    tm = x_ref.shape[0]
    hb = _rms_modulate(x_ref[...], g_ref[...], sc_ref[...], sh_ref[...]).astype(BF16)
    h_ref[...] = hb
    q = _dot(hb, wq_ref[...]).astype(BF16)
    half = PEER_DQ // 2
    pairs = _cand_pairs()
    n_rows = cand_scr.shape[0]
    cand_scr[len(pairs):, :] = jnp.full((n_rows - len(pairs), tm), NEG_INF, F32)
    for h in range(PEER_HEADS):
        q1 = q[:, h * PEER_DQ:h * PEER_DQ + half]
        q2 = q[:, h * PEER_DQ + half:(h + 1) * PEER_DQ]
        s1 = _dot_nt(keys_ref[h, 0], q1)
        s2 = _dot_nt(keys_ref[h, 1], q2)
        a = _top_rows(s1, PEER_NCAND)
        b = _top_rows(s2, PEER_NCAND)
        for r, (i, j) in enumerate(pairs):
            cand_scr[r:r + 1, :] = a[i] + b[j]
        c = _top_rows(cand_scr[...], PEER_NCAND)
        top = a[0] + b[0]
        zsum = jnp.zeros_like(top)
        for r in range(PEER_TOPK):
            zsum = zsum + jnp.exp(c[r] - top)
        tau = 0.5 * (c[PEER_TOPK - 1] + c[PEER_TOPK])
        thr_ref[h] = tau - s1
        g1_ref[h] = jnp.exp(s1 - a[0]) * (0.5 / zsum)
        s2_ref[h] = s2
        e2_ref[h] = jnp.exp(s2 - b[0])


def _peer_scores(x, mod3, layer, norm_g, wq, keys):
    tm = PEER_SC_TM
    n_tok = x.shape[0]
    sc_spec = pl.BlockSpec((PEER_HEADS, PEER_NKEYS, tm), lambda i: (0, 0, i))
    sc_shape = jax.ShapeDtypeStruct((PEER_HEADS, PEER_NKEYS, n_tok), F32)
    n_cand_rows = -(-len(_cand_pairs()) // 8) * 8
    return pl.pallas_call(
        _peer_scores_kernel,
        grid=(n_tok // tm,),
        in_specs=[
            pl.BlockSpec((tm, D_MODEL), lambda i: (i, 0)),
            _mod_spec(layer, 3, tm),
            _mod_spec(layer, 4, tm),
            pl.BlockSpec((1, D_MODEL), lambda i: (0, 0)),
            pl.BlockSpec((None, D_MODEL, PEER_HEADS * PEER_DQ), lambda i: (layer, 0, 0)),
            pl.BlockSpec((None, PEER_HEADS, 2, PEER_NKEYS, PEER_DQ // 2), lambda i: (layer, 0, 0, 0, 0)),
        ],
        out_specs=[pl.BlockSpec((tm, D_MODEL), lambda i: (i, 0)), sc_spec, sc_spec, sc_spec, sc_spec],
        out_shape=[jax.ShapeDtypeStruct((n_tok, D_MODEL), BF16), sc_shape, sc_shape, sc_shape, sc_shape],
        scratch_shapes=[pltpu.VMEM((n_cand_rows, tm), F32)],
        compiler_params=_cparams(("arbitrary",)),
        name="peer_scores",
    )(x, mod3, mod3, norm_g.reshape(1, D_MODEL), wq, keys)


PEER_TM = 512
PEER_EBLK = 1024
GELU_C = math.sqrt(2.0 / math.pi)


def _peer_experts_kernel(x_ref, g2_ref, h_ref, thr_ref, g1_ref, s2_ref, e2_ref, u_ref, vt_ref,
                         o_ref, acc_scr, act_scr, w_scr, gl_scr):
    tm = x_ref.shape[0]
    eblk = u_ref.shape[0]
    j = pl.program_id(1)

    @pl.when(j == 0)
    def _():
        acc_scr[...] = jnp.zeros_like(acc_scr)

    nb = eblk // PEER_NKEYS
    rsub = 16
    na = 2

    def region(c):
        rows_c = slice(c * na * PEER_NKEYS, (c + 1) * na * PEER_NKEYS)
        act_scr[rows_c, :] = _dot_nt(u_ref[rows_c, :], h_ref[...])
        for lg in range(tm // 128):
            sl = slice(lg * 128, (lg + 1) * 128)
            for r0 in range(0, PEER_NKEYS, rsub):
                w = [jnp.zeros((rsub, 128), F32) for _ in range(na)]
                for h in range(PEER_HEADS):
                    s2 = s2_ref[h, r0:r0 + rsub, sl]
                    e2 = e2_ref[h, r0:r0 + rsub, sl]
                    for k in range(na):
                        a = c * na + k
                        w[k] = w[k] + jnp.where(s2 >= thr_ref[h, a:a + 1, sl], e2, 0.0) * g1_ref[h, a:a + 1, sl]
                for k in range(na):
                    r = (c * na + k) * PEER_NKEYS + r0
                    w_scr[r:r + rsub, sl] = w[k]

    for c in range(nb // na):
        pl.when(j + c < pl.num_programs(1) + c)(functools.partial(region, c))

    @pl.when(j + nb < pl.num_programs(1) + nb)
    def _():
        act = act_scr[...]
        inner = act * (GELU_C + (GELU_C * 0.044715) * (act * act))
        gl_scr[...] = ((act * w_scr[...]) * (1.0 + jnp.tanh(inner))).astype(BF16)

    acc_scr[...] += _dot(vt_ref[...], gl_scr[...])

    @pl.when(j == pl.num_programs(1) - 1)
    def _():
        o_ref[...] = x_ref[...] + g2_ref[...] * acc_scr[...].T


def _peer_experts(x, mod3, layer, hb, thr, g1, s2, e2, u_bf, vt_bf):
    tm, eblk = PEER_TM, PEER_EBLK
    n_tok = x.shape[0]
    once = pl.Buffered(1)
    sc_spec = pl.BlockSpec((PEER_HEADS, PEER_NKEYS, tm), lambda i, j: (0, 0, i))
    k1_spec = pl.BlockSpec((PEER_HEADS, eblk // PEER_NKEYS, tm), lambda i, j: (0, j, i))
    return pl.pallas_call(
        _peer_experts_kernel,
        grid=(n_tok // tm, PEER_EXPERTS // eblk),
        in_specs=[
            pl.BlockSpec((tm, D_MODEL), lambda i, j: (i, 0), pipeline_mode=once),
            _mod_spec(layer, 5, tm),
            pl.BlockSpec((tm, D_MODEL), lambda i, j: (i, 0)),
            k1_spec, k1_spec, sc_spec, sc_spec,
            pl.BlockSpec((None, eblk, D_MODEL), lambda i, j: (layer, j, 0)),
            pl.BlockSpec((None, D_MODEL, eblk), lambda i, j: (layer, 0, j)),
        ],
        out_specs=pl.BlockSpec((tm, D_MODEL), lambda i, j: (i, 0)),
        out_shape=jax.ShapeDtypeStruct((n_tok, D_MODEL), F32),
        scratch_shapes=[pltpu.VMEM((D_MODEL, tm), F32), pltpu.VMEM((eblk, tm), F32), pltpu.VMEM((eblk, tm), F32),
                        pltpu.VMEM((eblk, tm), BF16)],
        compiler_params=_cparams(("arbitrary", "arbitrary")),
        name="peer_experts",
    )(x, mod3, hb, thr, g1, s2, e2, u_bf, vt_bf)


FINAL_TM = 512


def _final_norm_kernel(n_ctx_tiles, x_ref, g_ref, oc_ref, od_ref):
    x = x_ref[...]
    var = jnp.mean(x * x, axis=-1, keepdims=True)
    y = x * lax.rsqrt(var + EPS) * g_ref[...]
    i = pl.program_id(0)

    @pl.when(i < n_ctx_tiles)
    def _():
        oc_ref[...] = y

    @pl.when(i >= n_ctx_tiles)
    def _():
        od_ref[...] = y


def _final_norm(x, g):
    tm = FINAL_TM
    n_ctx_tiles = N_CTX_TOK // tm
    return pl.pallas_call(
        functools.partial(_final_norm_kernel, n_ctx_tiles),
        grid=(N_TOK // tm,),
        in_specs=[pl.BlockSpec((tm, D_MODEL), lambda i: (i, 0)), pl.BlockSpec((1, D_MODEL), lambda i: (0, 0))],
        out_specs=[pl.BlockSpec((tm, D_MODEL), lambda i: (jnp.minimum(i, n_ctx_tiles - 1), 0)),
                   pl.BlockSpec((tm, D_MODEL), lambda i: (jnp.maximum(i - n_ctx_tiles, 0), 0))],
        out_shape=[jax.ShapeDtypeStruct((N_CTX_TOK, D_MODEL), F32),
                   jax.ShapeDtypeStruct((N_TOK - N_CTX_TOK, D_MODEL), F32)],
        compiler_params=_cparams(("arbitrary",)),
        name="final_norm",
    )(x, g.reshape(1, D_MODEL))


def _permute_w_in(w):
    cuts = np.cumsum([D_SSD, SSD_CONV_CH, 2 * SSD_HEADS, D_S5, D_RET, D_RET, D_RET])
    z, xbc, dt, u, rq, rk, rv, rg = jnp.split(w, [int(c) for c in cuts], axis=2)
    pad = jnp.zeros((DEPTH, D_MODEL, CHUNK - SSD_HEADS), w.dtype)
    tail = jnp.zeros((DEPTH, D_MODEL, PROJ_W - COL_DT - 2 * CHUNK), w.dtype)
    out = jnp.concatenate([z, rq, rk, rv, rg, xbc, u, dt[..., :SSD_HEADS], pad, dt[..., SSD_HEADS:], pad, tail], axis=2)
    return out.astype(BF16)


def _pad_lanes(a, width):
    return jnp.pad(a, [(0, 0)] * (a.ndim - 1) + [(0, width - a.shape[-1])])


def _ssd_params(conv_w, conv_b, dt_bias, a_log, d_skip, norm_g):
    return {
        "conv_w": jnp.pad(conv_w, ((0, 8 - SSD_CONV_K), (0, 0))),
        "conv_b": conv_b.reshape(1, SSD_CONV_CH),
        "dt_bias": _pad_lanes(dt_bias, CHUNK).reshape(2, 1, CHUNK),
        "a_exp": _pad_lanes(jnp.exp(a_log), CHUNK).reshape(2, 1, CHUNK),
        "d_skip": jnp.repeat(d_skip, SSD_P).reshape(1, D_SSD),
        "norm_g": norm_g.reshape(1, D_SSD),
    }


def _s5_params(layer, k_re, k_im, p_re, p_im, b_re, b_im, c_re, c_im, d_skip, glu_w, glu_b):
    kr = k_re.reshape(2, S5_GROUPS, S5_STATE, 1)
    ki = k_im.reshape(2, S5_GROUPS, S5_STATE, 1)
    bb_re = kr * b_re - ki * b_im
    bb_im = kr * b_im + ki * b_re
    eye = jnp.eye(S5_GROUPS, dtype=F32)

    def blockdiag_in(bb):
        return jnp.einsum("gh,dgnc->dgchn", eye, bb).reshape(2, D_S5, S5_LANES)

    def blockdiag_out(cc):
        return jnp.einsum("gh,dgcn->dgnhc", eye, cc).reshape(2, S5_LANES, D_S5)

    wb = jnp.concatenate([blockdiag_in(bb_re), blockdiag_in(bb_im)], axis=2).astype(BF16)
    wc = jnp.concatenate([blockdiag_out(c_re), blockdiag_out(-c_im)], axis=1).astype(BF16)
    return {"layer": layer, "wb": wb, "wc": wc, "p_re": p_re, "p_im": p_im,
            "d_skip": d_skip.reshape(1, D_S5), "glu_w": glu_w.astype(BF16), "glu_b": glu_b.reshape(1, D_S5)}


def _ssd_state_in(s):
    b = s.shape[0]
    s = s.reshape(b, 2, SSD_GROUPS, SSD_HPG, SSD_P, SSD_N)
    return jnp.transpose(s, (0, 1, 2, 5, 3, 4)).reshape(b, 2, SSD_GROUPS, SSD_N, SSD_HPG * SSD_P)


def kernel(x_prompt, x_sample, c, state_ssd, state_s5_re, state_s5_im, state_ret, c_ctx, ada_w, ada_b, norm1_g, norm2_g, w_in, w_out, ssd_conv_w, ssd_conv_b, ssd_dt_bias, ssd_a_log, ssd_d, ssd_norm_g, s5_lambda_re, s5_lambda_im, s5_log_dt, s5_b_re, s5_b_im, s5_c_re, s5_c_im, s5_d, s5_glu_w, s5_glu_b, ret_gn_g, peer_wq, peer_keys, peer_u, peer_v, final_norm_g):
    nc_ctx = CTX_LEN // CHUNK
    nc_dec = DEC_LEN // CHUNK
    dec_row_off = N_CTX_TOK // CHUNK

    cond = jnp.concatenate([c_ctx[None, :], c, jnp.zeros((N_COND - 1 - N_DEC_SEQ, D_MODEL), F32)], axis=0)
    mod3 = _adaln(cond, ada_w, ada_b).reshape(DEPTH * N_COND * N_MOD, 1, D_MODEL)
    p_re, p_im, k_re, k_im = _s5_prep(s5_lambda_re, s5_lambda_im, s5_log_dt)

    x = jnp.concatenate([x_prompt.reshape(N_CTX_TOK, D_MODEL), x_sample.reshape(N_TOK - N_CTX_TOK, D_MODEL)], axis=0)
    new_ssd, new_re, new_im, new_ret = [], [], [], []
    w_in_b = _permute_w_in(w_in)
    w_out_b = w_out.astype(BF16)
    wq_b = peer_wq.astype(BF16)
    keys_b = peer_keys.astype(BF16)
    u_b = peer_u.astype(BF16)
    vt_b = jnp.swapaxes(peer_v, 1, 2).astype(BF16)
    for l in range(DEPTH):
        proj = _inproj(x, mod3, l, norm1_g[l], w_in_b)

        sp = _ssd_params(ssd_conv_w[l], ssd_conv_b[l], ssd_dt_bias[l], ssd_a_log[l], ssd_d[l], ssd_norm_g[l])
        y_ssd_c, st_ssd = _ssd(proj, sp, N_CTX_SEQ, nc_ctx, 0, None)
        y_ssd_d, _ = _ssd(proj, sp, N_DEC_SEQ, nc_dec, dec_row_off, _ssd_state_in(state_ssd[:, l]))

        s5p = _s5_params(l, k_re[2 * l:2 * l + 2], k_im[2 * l:2 * l + 2], p_re, p_im, s5_b_re[l], s5_b_im[l],
                         s5_c_re[l], s5_c_im[l], s5_d[l], s5_glu_w[l], s5_glu_b[l])
        y_s5_c, st_re, st_im = _s5(proj, s5p, N_CTX_SEQ, nc_ctx, 0, None)
        x0 = (state_s5_re[:, l].reshape(N_DEC_SEQ, 2, 1, S5_LANES), state_s5_im[:, l].reshape(N_DEC_SEQ, 2, 1, S5_LANES))
        y_s5_d, _, _ = _s5(proj, s5p, N_DEC_SEQ, nc_dec, dec_row_off, x0)

        y_ret_c, st_ret = _ret(proj, ret_gn_g[l], N_CTX_SEQ, nc_ctx, 0, None, rope=False)
        y_ret_d, _ = _ret(proj, ret_gn_g[l], N_DEC_SEQ, nc_dec, dec_row_off,
                          jnp.swapaxes(state_ret[:, l], -1, -2), rope=True)

        x = _outproj(x, jnp.concatenate([y_ssd_c, y_ssd_d]), jnp.concatenate([y_s5_c, y_s5_d]),
                     jnp.concatenate([y_ret_c, y_ret_d]), w_out_b, mod3, l)

        hb, thr, g1, s2, e2 = _peer_scores(x, mod3, l, norm2_g[l], wq_b, keys_b)
        x = _peer_experts(x, mod3, l, hb, thr, g1, s2, e2, u_b, vt_b)

        new_ssd.append(st_ssd)
        new_re.append(st_re.reshape(N_CTX_SEQ, 2, S5_GROUPS, S5_STATE))
        new_im.append(st_im.reshape(N_CTX_SEQ, 2, S5_GROUPS, S5_STATE))
        new_ret.append(st_ret)

    y_ctx, y_dec = _final_norm(x, final_norm_g)
    y_prompt = y_ctx.reshape(N_CTX_SEQ, CTX_LEN, D_MODEL)
    y_sample = y_dec.reshape(N_DEC_SEQ, DEC_LEN, D_MODEL)
    return (y_prompt, y_sample, jnp.stack(new_ssd, axis=1), jnp.stack(new_re, axis=1),
            jnp.stack(new_im, axis=1), jnp.stack(new_ret, axis=1))
```

```python
import functools
import math

import jax
import jax.numpy as jnp
import numpy as np
from jax import lax
from jax.experimental import pallas as pl
from jax.experimental.pallas import tpu as pltpu

F32 = jnp.float32
BF16 = jnp.bfloat16

D_MODEL = 2048
N_CTX_SEQ = 16
CTX_LEN = 256
N_DEC_SEQ = 2
DEC_LEN = 2048
N_CTX_TOK = N_CTX_SEQ * CTX_LEN
N_TOK = N_CTX_TOK + N_DEC_SEQ * DEC_LEN
DEPTH = 2
GRID_W = 64
CHUNK = 128

SSD_HEADS = 12
SSD_P = 64
D_SSD = SSD_HEADS * SSD_P
SSD_N = 128
SSD_GROUPS = 2
SSD_HPG = SSD_HEADS // SSD_GROUPS
SSD_CONV_K = 5
SSD_CONV_CH = D_SSD + 2 * SSD_GROUPS * SSD_N
S5_CH = 16
S5_GROUPS = 32
D_S5 = S5_CH * S5_GROUPS
S5_STATE = 64
S5_LANES = S5_GROUPS * S5_STATE
S5_TILE = 8
RET_HEADS = 6
RET_DK = 128
RET_DV = 128
D_RET = RET_HEADS * RET_DV
ROPE_BASE = 10000.0
PEER_HEADS = 8
PEER_DQ = 256
PEER_NKEYS = 128
PEER_EXPERTS = PEER_NKEYS * PEER_NKEYS
PEER_TOPK = 16
N_MOD = 6
N_COND = 8
EPS = 1e-6

COL_Z, COL_Q, COL_K, COL_V, COL_G = 0, 768, 1536, 2304, 3072
COL_XBC = 3840
COL_U = 5120
COL_DT = 5632
PROJ_W = 6144

VMEM_LIMIT = 56 * 1024 * 1024

NEG_INF = float("-inf")


def _cparams(sem, vmem_limit=VMEM_LIMIT):
    return pltpu.CompilerParams(dimension_semantics=sem, vmem_limit_bytes=vmem_limit)


def _split3(a):
    hi = a.astype(BF16)
    r1 = a - hi.astype(F32)
    mid = r1.astype(BF16)
    lo = (r1 - mid.astype(F32)).astype(BF16)
    return hi, mid, lo


def _dot(a, b):
    return jnp.dot(a, b, preferred_element_type=F32)


def _dot_split_lhs(a, b_exact):
    hi, mid, lo = _split3(a)
    return _dot(hi, b_exact) + _dot(mid, b_exact) + _dot(lo, b_exact)


def _dot_split_rhs(a_exact, b):
    hi, mid, lo = _split3(b)
    return _dot(a_exact, hi) + _dot(a_exact, mid) + _dot(a_exact, lo)


def _dot_nt(a, b):
    return lax.dot_general(a, b, (((1,), (1,)), ((), ())), preferred_element_type=F32)


def _silu(x):
    return x * jax.nn.sigmoid(x)


def _softplus(x):
    return jnp.maximum(x, 0.0) + jnp.log1p(jnp.exp(-jnp.abs(x)))


ADA_TN = 1536


def _adaln_kernel(c_ref, w_ref, b_ref, o_ref):
    s = _silu(c_ref[...]).astype(BF16)
    o_ref[...] = _dot(s, w_ref[...].astype(BF16)) + b_ref[...]


def _adaln(cond, ada_w, ada_b):
    n_out = N_MOD * D_MODEL
    return pl.pallas_call(
        _adaln_kernel,
        grid=(DEPTH, n_out // ADA_TN),
        in_specs=[
            pl.BlockSpec((N_COND, D_MODEL), lambda l, j: (0, 0)),
            pl.BlockSpec((None, D_MODEL, ADA_TN), lambda l, j: (l, 0, j)),
            pl.BlockSpec((None, 1, ADA_TN), lambda l, j: (l, 0, j)),
        ],
        out_specs=pl.BlockSpec((None, N_COND, ADA_TN), lambda l, j: (l, 0, j)),
        out_shape=jax.ShapeDtypeStruct((DEPTH, N_COND, n_out), F32),
        compiler_params=_cparams(("arbitrary", "arbitrary")),
        name="adaln",
    )(cond, ada_w, ada_b.reshape(DEPTH, 1, n_out))


def _mod_spec(layer, which, tm):
    n_ctx_tiles = N_CTX_TOK // tm
    tiles_per_dec = DEC_LEN // tm

    def index(i, *_):
        cond = jnp.where(i < n_ctx_tiles, 0, 1 + (i - n_ctx_tiles) // tiles_per_dec)
        return ((layer * N_COND + cond) * N_MOD + which, 0, 0)

    return pl.BlockSpec((None, 1, D_MODEL), index)


def _rms_modulate(x, g, sc, sh):
    var = jnp.mean(x * x, axis=-1, keepdims=True)
    y = x * lax.rsqrt(var + EPS) * g
    return y * (1.0 + sc) + sh


INPROJ_TM = 256


def _inproj_kernel(x_ref, sh_ref, sc_ref, g_ref, w_ref, o_ref):
    h = _rms_modulate(x_ref[...], g_ref[...], sc_ref[...], sh_ref[...]).astype(BF16)
    o_ref[...] = _dot(h, w_ref[...])


def _inproj(x, mod3, layer, norm_g, w_perm):
    tm = INPROJ_TM
    return pl.pallas_call(
        _inproj_kernel,
        grid=(N_TOK // tm,),
        in_specs=[
            pl.BlockSpec((tm, D_MODEL), lambda i: (i, 0)),
            _mod_spec(layer, 0, tm),
            _mod_spec(layer, 1, tm),
            pl.BlockSpec((1, D_MODEL), lambda i: (0, 0)),
            pl.BlockSpec((None, D_MODEL, PROJ_W), lambda i: (layer, 0, 0), pipeline_mode=pl.Buffered(1)),
        ],
        out_specs=pl.BlockSpec((tm, PROJ_W), lambda i: (i, 0)),
        out_shape=jax.ShapeDtypeStruct((N_TOK, PROJ_W), F32),
        compiler_params=_cparams(("arbitrary",)),
        name="inproj",
    )(x, mod3, mod3, norm_g.reshape(1, D_MODEL), w_perm)


def _walk(nc):
    def direction(t):
        return jnp.where(t < nc, 1, 0)

    def chunk(t):
        return jnp.where(t < nc, nc - 1 - t, t - nc)

    return direction, chunk


def _seq_specs(nc, row_off):
    direction, chunk = _walk(nc)

    def rows(s, t):
        return row_off + s * nc + chunk(t)

    def out_rows(s, t):
        return row_off + s * nc + jnp.where(t < nc, 0, t - nc)

    return direction, chunk, rows, out_rows


def _ssd_consts():
    idx = np.arange(CHUNK)
    tri = np.stack([(idx[None, :] <= idx[:, None]), (idx[None, :] >= idx[:, None])]).astype(np.float32)
    e_p = np.zeros((CHUNK, D_SSD), np.float32)
    e_n = np.zeros((CHUNK, SSD_HEADS * CHUNK), np.float32)
    for h in range(SSD_HEADS):
        e_p[h, h * SSD_P:(h + 1) * SSD_P] = 1.0
        e_n[h, h * CHUNK:(h + 1) * CHUNK] = 1.0
    lane = np.arange(CHUNK)
    pair = np.concatenate([np.broadcast_to(lane < SSD_P, (CHUNK, CHUNK)),
                           np.broadcast_to(lane >= SSD_P, (CHUNK, CHUNK))]).astype(np.float32)
    return tri, e_p, e_n, pair


def _ssd_kernel(nc, zero_init, emit_state, has_prev, *refs):
    (z_ref, xc_ref, xp_ref, xn_ref, dt_ref, cw_ref, cb_ref, dtb_ref, aexp_ref, dsk_ref, ng_ref,
     tri_ref, ep_ref, en_ref, pair_ref) = refs[:15]
    pos = 15
    s0_ref = None
    if not zero_init:
        s0_ref = refs[pos]
        pos += 1
    if has_prev:
        pos += 1
    y_ref = refs[pos]
    pos += 1
    st_ref = None
    if emit_state:
        st_ref = refs[pos]
        pos += 1
    state_scr, stash_scr = refs[pos:]

    t = pl.program_id(1)
    fwd = t >= nc
    step = jnp.where(fwd, t - nc, t)
    chunk = jnp.where(fwd, t - nc, nc - 1 - t)
    row0 = pl.multiple_of(chunk * CHUNK, CHUNK)

    @pl.when(step == 0)
    def _():
        if zero_init:
            state_scr[...] = jnp.zeros_like(state_scr)
        else:
            state_scr[...] = s0_ref[...]

    prev = jnp.where(chunk > 0, xp_ref[...], 0.0)
    nxt = jnp.where(chunk < nc - 1, xn_ref[...], 0.0)
    ext = jnp.concatenate([prev, xc_ref[...], nxt], axis=0)
    conv = cb_ref[...]
    for k in range(SSD_CONV_K):
        off = 8 + k - SSD_CONV_K // 2
        conv = conv + cw_ref[k:k + 1, :] * ext[off:off + CHUNK, :]
    xbc = _silu(conv)
    x = xbc[:, :D_SSD]
    bm = xbc[:, D_SSD:D_SSD + SSD_GROUPS * SSD_N]
    cm = xbc[:, D_SSD + SSD_GROUPS * SSD_N:]

    dt = _softplus(dt_ref[...] + dtb_ref[...])
    la = -dt * aexp_ref[...]
    tri = tri_ref[...]
    cum = _dot_split_rhs(tri.astype(BF16), la)
    ep = ep_ref[...]
    cum_p = _dot_split_lhs(cum, ep)
    dt_p = _dot_split_lhs(dt, ep)
    cum_col = _dot_split_lhs(cum, en_ref[...])
    cum_t = cum.T
    tot_p = jnp.where(fwd, cum_p[CHUNK - 1:CHUNK, :], cum_p[0:1, :])

    v = x * dt_p
    vb = v.astype(BF16)
    mask = tri > 0.5
    pair = pair_ref[...]
    y_parts = []
    for g in range(SSD_GROUPS):
        cg = cm[:, g * SSD_N:(g + 1) * SSD_N].astype(BF16)
        bg = bm[:, g * SSD_N:(g + 1) * SSD_N]
        gmat = _dot_nt(cg, bg.astype(BF16))
        for hp in range(SSD_HPG // 2):
            scs = []
            for h in (g * SSD_HPG + 2 * hp, g * SSD_HPG + 2 * hp + 1):
                ci = cum_col[:, h * CHUNK:(h + 1) * CHUNK]
                cj = cum_t[h:h + 1, :]
                dec = jnp.exp(jnp.where(mask, ci - cj, NEG_INF))
                scs.append((gmat * dec).astype(BF16))
            c0 = (g * SSD_HPG + 2 * hp) * SSD_P
            v2 = vb[:, c0:c0 + 2 * SSD_P]
            vv = jnp.concatenate([v2, v2], axis=0) * pair
            y_parts.append(_dot(jnp.concatenate(scs, axis=1), vv))
    y = jnp.concatenate(y_parts, axis=1)

    w_p = SSD_HPG * SSD_P
    y_off = jnp.concatenate(
        [_dot(cm[:, g * SSD_N:(g + 1) * SSD_N].astype(BF16), state_scr[g].astype(BF16))
         for g in range(SSD_GROUPS)], axis=1)
    y = y + y_off * jnp.exp(cum_p)

    vw = (v * jnp.exp(tot_p - cum_p)).astype(BF16)
    cdec = jnp.exp(tot_p)
    for g in range(SSD_GROUPS):
        bt = bm[:, g * SSD_N:(g + 1) * SSD_N].T.astype(BF16)
        state_scr[g] = state_scr[g] * cdec[:, g * w_p:(g + 1) * w_p] + _dot(bt, vw[:, g * w_p:(g + 1) * w_p])

    @pl.when(jnp.logical_not(fwd))
    def _():
        stash_scr[pl.ds(row0, CHUNK), :] = y

    @pl.when(fwd)
    def _():
        ytot = y + stash_scr[pl.ds(row0, CHUNK), :] + x * dsk_ref[...]
        gated = ytot * _silu(z_ref[...])
        var = jnp.mean(gated * gated, axis=-1, keepdims=True)
        y_ref[...] = (gated * lax.rsqrt(var + EPS) * ng_ref[...]).astype(y_ref.dtype)

    if emit_state:
        @pl.when(step == nc - 1)
        def _():
            for g in range(SSD_GROUPS):
                st_t = state_scr[g].T
                for k in range(SSD_HPG):
                    st_ref[g * SSD_HPG + k] = st_t[k * SSD_P:(k + 1) * SSD_P, :]


def _ssd(proj, p, n_seq, nc, row_off, s0, layer=0, prev_states=None):
    zero_init = s0 is None
    emit_state = zero_init
    direction, chunk, rows, out_rows = _seq_specs(nc, row_off)
    tri, e_p, e_n, pair = _ssd_consts()
    n8 = N_TOK // 8
    w_st = SSD_HPG * SSD_P
    in_specs = [
        pl.BlockSpec((CHUNK, D_SSD), lambda s, t: (rows(s, t), COL_Z // D_SSD)),
        pl.BlockSpec((CHUNK, SSD_CONV_CH), lambda s, t: (rows(s, t), COL_XBC // SSD_CONV_CH)),
        pl.BlockSpec((8, SSD_CONV_CH),
                     lambda s, t: (jnp.maximum(rows(s, t) * (CHUNK // 8) - 1, 0), COL_XBC // SSD_CONV_CH)),
        pl.BlockSpec((8, SSD_CONV_CH),
                     lambda s, t: (jnp.minimum((rows(s, t) + 1) * (CHUNK // 8), n8 - 1), COL_XBC // SSD_CONV_CH)),
        pl.BlockSpec((CHUNK, CHUNK), lambda s, t: (rows(s, t), COL_DT // CHUNK + direction(t))),
        pl.BlockSpec((8, SSD_CONV_CH), lambda s, t: (0, 0)),
        pl.BlockSpec((1, SSD_CONV_CH), lambda s, t: (0, 0)),
        pl.BlockSpec((None, 1, CHUNK), lambda s, t: (direction(t), 0, 0)),
        pl.BlockSpec((None, 1, CHUNK), lambda s, t: (direction(t), 0, 0)),
        pl.BlockSpec((1, D_SSD), lambda s, t: (0, 0)),
        pl.BlockSpec((1, D_SSD), lambda s, t: (0, 0)),
        pl.BlockSpec((None, CHUNK, CHUNK), lambda s, t: (direction(t), 0, 0)),
        pl.BlockSpec((CHUNK, D_SSD), lambda s, t: (0, 0)),
        pl.BlockSpec((CHUNK, SSD_HEADS * CHUNK), lambda s, t: (0, 0)),
        pl.BlockSpec((2 * CHUNK, CHUNK), lambda s, t: (0, 0)),
    ]
    args = [proj, proj, proj, proj, proj, p["conv_w"], p["conv_b"], p["dt_bias"], p["a_exp"], p["d_skip"],
            p["norm_g"], jnp.asarray(tri), jnp.asarray(e_p, BF16), jnp.asarray(e_n, BF16), jnp.asarray(pair, BF16)]
    st_spec = pl.BlockSpec((None, None, SSD_GROUPS, SSD_N, w_st), lambda s, t: (s, direction(t), 0, 0, 0))
    if not zero_init:
        in_specs.append(st_spec)
        args.append(s0)
    out_specs = [pl.BlockSpec((CHUNK, D_SSD), lambda s, t: (out_rows(s, t) - row_off, 0))]
    out_shape = [jax.ShapeDtypeStruct((n_seq * nc * CHUNK, D_SSD), BF16)]
    aliases = {}
    if emit_state:
        out_specs.append(pl.BlockSpec((None, None, None, SSD_HEADS, SSD_P, SSD_N),
                                      lambda s, t: (s, layer, direction(t), 0, 0, 0)))
        out_shape.append(jax.ShapeDtypeStruct((n_seq, DEPTH, 2, SSD_HEADS, SSD_P, SSD_N), F32))
        if prev_states is not None:
            in_specs.append(pl.BlockSpec(memory_space=pl.ANY))
            args.append(prev_states)
            aliases = {len(args) - 1: 1}
    res = pl.pallas_call(
        functools.partial(_ssd_kernel, nc, zero_init, emit_state, prev_states is not None),
        grid=(n_seq, 2 * nc),
        in_specs=in_specs,
        out_specs=out_specs,
        out_shape=out_shape,
        scratch_shapes=[pltpu.VMEM((SSD_GROUPS, SSD_N, w_st), F32), pltpu.VMEM((nc * CHUNK, D_SSD), F32)],
        input_output_aliases=aliases,
        compiler_params=_cparams(("arbitrary", "arbitrary")),
        name="ssd_ctx" if zero_init else "ssd_dec",
    )(*args)
    return res if emit_state else (res[0], None)


def _s5_prep_kernel(lr_ref, li_ref, ldt_ref, pre_ref, pim_ref, kre_ref, kim_ref):
    lr = lr_ref[...]
    li = li_ref[...]
    dt = jnp.exp(ldt_ref[...])
    kk = (lax.broadcasted_iota(jnp.int32, (S5_TILE, S5_LANES), 0) + 1).astype(F32)
    mag = jnp.exp(kk * (lr * dt))
    ang = kk * (li * dt)
    p_re = mag * jnp.cos(ang)
    p_im = mag * jnp.sin(ang)
    pre_ref[...] = p_re
    pim_ref[...] = p_im
    a_re = p_re[0:1, :]
    a_im = p_im[0:1, :]
    den = lr * lr + li * li
    num_re = a_re - 1.0
    kre_ref[...] = (num_re * lr + a_im * li) / den
    kim_ref[...] = (a_im * lr - num_re * li) / den


def _s5_prep(lam_re, lam_im, log_dt):
    n = DEPTH * 2
    lr = lam_re.reshape(n, 1, S5_LANES)
    li = lam_im.reshape(n, 1, S5_LANES)
    ldt = jnp.broadcast_to(log_dt[..., None], (DEPTH, 2, S5_GROUPS, S5_STATE)).reshape(n, 1, S5_LANES)
    row = pl.BlockSpec((None, 1, S5_LANES), lambda i: (i, 0, 0))
    tab = pl.BlockSpec((None, S5_TILE, S5_LANES), lambda i: (i, 0, 0))
    return pl.pallas_call(
        _s5_prep_kernel,
        grid=(n,),
        in_specs=[row, row, row],
        out_specs=[tab, tab, row, row],
        out_shape=[jax.ShapeDtypeStruct((n, S5_TILE, S5_LANES), F32)] * 2
        + [jax.ShapeDtypeStruct((n, 1, S5_LANES), F32)] * 2,
        compiler_params=_cparams(("arbitrary",)),
        name="s5_prep",
    )(lr, li, ldt)


def _s5_kernel(nc, zero_init, emit_state, *refs):
    (u_ref, jm_ref, wb_ref, pre_ref, pim_ref, wc_ref, dsk_ref, gw_ref, gb_ref) = refs[:9]
    pos = 9
    x0re_ref = x0im_ref = None
    if not zero_init:
        x0re_ref, x0im_ref = refs[pos:pos + 2]
        pos += 2
    y_ref = refs[pos]
    pos += 1
    sre_ref = sim_ref = None
    if emit_state:
        sre_ref, sim_ref = refs[pos:pos + 2]
        pos += 2
    cre_scr, cim_scr, stash_scr = refs[pos:]

    t = pl.program_id(1)
    fwd = t >= nc
    step = jnp.where(fwd, t - nc, t)
    chunk = jnp.where(fwd, t - nc, nc - 1 - t)
    row0 = pl.multiple_of(chunk * CHUNK, CHUNK)

    @pl.when(step == 0)
    def _():
        if zero_init:
            cre_scr[...] = jnp.zeros_like(cre_scr)
            cim_scr[...] = jnp.zeros_like(cim_scr)
        else:
            cre_scr[...] = x0re_ref[...]
            cim_scr[...] = x0im_ref[...]

    u = u_ref[...]
    jm = jm_ref[...]
    us = _dot(jm, u.astype(BF16)).astype(BF16)
    bu = _dot(us, wb_ref[...])
    xr = bu[:, :S5_LANES]
    xi = bu[:, S5_LANES:]
    cr = cre_scr[...]
    ci = cim_scr[...]
    pr = pre_ref[...]
    pi = pim_ref[...]
    sub = lax.broadcasted_iota(jnp.int32, (S5_TILE, S5_LANES), 0)
    steps = []
    d = 1
    while d < S5_TILE:
        keep = sub >= d
        steps.append((d, jnp.where(keep, pre_ref[d - 1:d, :], 0.0), jnp.where(keep, pim_ref[d - 1:d, :], 0.0)))
        d *= 2
    tiles_r, tiles_i = [], []
    for b in range(CHUNK // S5_TILE):
        br = xr[b * S5_TILE:(b + 1) * S5_TILE, :]
        bi = xi[b * S5_TILE:(b + 1) * S5_TILE, :]
        for d, ar, ai in steps:
            sr = pltpu.roll(br, d, axis=0)
            si = pltpu.roll(bi, d, axis=0)
            br, bi = br + (ar * sr - ai * si), bi + (ar * si + ai * sr)
        br, bi = br + (pr * cr - pi * ci), bi + (pr * ci + pi * cr)
        cr = br[S5_TILE - 1:S5_TILE, :]
        ci = bi[S5_TILE - 1:S5_TILE, :]
        tiles_r.append(br)
        tiles_i.append(bi)
    xr = jnp.concatenate(tiles_r, axis=0)
    xi = jnp.concatenate(tiles_i, axis=0)
    cre_scr[...] = cr
    cim_scr[...] = ci

    ys = _dot(xr.astype(BF16), wc_ref[:S5_LANES, :]) + _dot(xi.astype(BF16), wc_ref[S5_LANES:, :])
    yh = ys.astype(BF16)
    yl = (ys - yh.astype(F32)).astype(BF16)
    y = _dot(jm, yh) + _dot(jm, yl)

    @pl.when(jnp.logical_not(fwd))
    def _():
        stash_scr[pl.ds(row0, CHUNK), :] = y

    @pl.when(fwd)
    def _():
        yt = y + stash_scr[pl.ds(row0, CHUNK), :] + u * dsk_ref[...]
        yt = jax.nn.gelu(yt)
        gate = jax.nn.sigmoid(_dot(yt.astype(BF16), gw_ref[...]) + gb_ref[...])
        y_ref[...] = (yt * gate).astype(y_ref.dtype)

    if emit_state:
        @pl.when(step == nc - 1)
        def _():
            sre_ref[...] = cre_scr[...]
            sim_ref[...] = cim_scr[...]


def _s5(proj, p, n_seq, nc, row_off, x0):
    zero_init = x0 is None
    emit_state = zero_init
    direction, chunk, rows, out_rows = _seq_specs(nc, row_off)
    eye = np.eye(CHUNK, dtype=np.float32)
    jm = jnp.asarray(np.stack([eye, eye[::-1]]), BF16)
    lyr = p["layer"]

    def dsel(t):
        return lyr * 2 + direction(t)

    in_specs = [
        pl.BlockSpec((CHUNK, D_S5), lambda s, t: (rows(s, t), COL_U // D_S5)),
        pl.BlockSpec((None, CHUNK, CHUNK), lambda s, t: (direction(t), 0, 0)),
        pl.BlockSpec((None, D_S5, 2 * S5_LANES), lambda s, t: (direction(t), 0, 0)),
        pl.BlockSpec((None, S5_TILE, S5_LANES), lambda s, t: (dsel(t), 0, 0)),
        pl.BlockSpec((None, S5_TILE, S5_LANES), lambda s, t: (dsel(t), 0, 0)),
        pl.BlockSpec((None, 2 * S5_LANES, D_S5), lambda s, t: (direction(t), 0, 0)),
        pl.BlockSpec((1, D_S5), lambda s, t: (0, 0)),
        pl.BlockSpec((D_S5, D_S5), lambda s, t: (0, 0)),
        pl.BlockSpec((1, D_S5), lambda s, t: (0, 0)),
    ]
    args = [proj, jm, p["wb"], p["p_re"], p["p_im"], p["wc"], p["d_skip"], p["glu_w"], p["glu_b"]]
    st_spec = pl.BlockSpec((None, None, 1, S5_LANES), lambda s, t: (s, direction(t), 0, 0))
    if not zero_init:
        in_specs += [st_spec, st_spec]
        args += [x0[0], x0[1]]
    out_specs = [pl.BlockSpec((CHUNK, D_S5), lambda s, t: (out_rows(s, t) - row_off, 0))]
    out_shape = [jax.ShapeDtypeStruct((n_seq * nc * CHUNK, D_S5), BF16)]
    if emit_state:
        out_specs += [st_spec, st_spec]
        out_shape += [jax.ShapeDtypeStruct((n_seq, 2, 1, S5_LANES), F32)] * 2
    res = pl.pallas_call(
        functools.partial(_s5_kernel, nc, zero_init, emit_state),
        grid=(n_seq, 2 * nc),
        in_specs=in_specs,
        out_specs=out_specs,
        out_shape=out_shape,
        scratch_shapes=[pltpu.VMEM((1, S5_LANES), F32), pltpu.VMEM((1, S5_LANES), F32),
                        pltpu.VMEM((nc * CHUNK, D_S5), F32)],
        compiler_params=_cparams(("arbitrary", "arbitrary")),
        name="s5_ctx" if zero_init else "s5_dec",
    )(*args)
    return res if emit_state else (res[0], None, None)


def _ret_consts():
    heads = np.arange(RET_HEADS, dtype=np.float64)
    lg = np.stack([np.log1p(-np.exp2(-5.0 - heads)), np.log1p(-np.exp2(-5.5 - heads))])
    i = np.arange(CHUNK, dtype=np.float64)
    diff = i[:, None] - i[None, :]
    dmat = np.zeros((2, RET_HEADS, CHUNK, CHUNK))
    rowdec = np.zeros((2, CHUNK, D_RET))
    wend = np.zeros((2, 8, CHUNK))
    cdec = np.zeros((2, 8, CHUNK))
    for h in range(RET_HEADS):
        dmat[0, h] = np.where(diff >= 0, np.exp(lg[0, h] * diff), 0.0)
        dmat[1, h] = np.where(diff <= 0, np.exp(-lg[1, h] * diff), 0.0)
        rowdec[0, :, h * RET_DV:(h + 1) * RET_DV] = np.exp(lg[0, h] * (i + 1))[:, None]
        rowdec[1, :, h * RET_DV:(h + 1) * RET_DV] = np.exp(lg[1, h] * (CHUNK - i))[:, None]
        wend[0, h] = np.exp(lg[0, h] * (CHUNK - 1 - i))
        wend[1, h] = np.exp(lg[1, h] * i)
        cdec[:, h] = np.exp(lg[:, h] * CHUNK)[:, None]
    return [jnp.asarray(a, F32) for a in (dmat, rowdec, wend, cdec)]


def _rope_tables():
    t = np.arange(DEC_LEN)
    row = (t // GRID_W).astype(np.float32)
    col = (t % GRID_W).astype(np.float32)
    quarter = RET_DK // 4
    freqs = (ROPE_BASE ** (-np.arange(quarter, dtype=np.float32) / quarter)).astype(np.float32)
    ar = (row[:, None] * freqs[None, :]).astype(np.float64)
    ac = (col[:, None] * freqs[None, :]).astype(np.float64)
    cos = np.concatenate([np.cos(ar), np.cos(ar), np.cos(ac), np.cos(ac)], axis=1)
    sin = np.concatenate([-np.sin(ar), np.sin(ar), -np.sin(ac), np.sin(ac)], axis=1)
    return jnp.asarray(cos, F32), jnp.asarray(sin, F32)


def _ret_kernel(nc, zero_init, emit_state, rope, has_prev, *refs):
    q_ref, k_ref, v_ref, g_ref = refs[:4]
    pos = 4
    cos_ref = sin_ref = None
    if rope:
        cos_ref, sin_ref = refs[pos:pos + 2]
        pos += 2
    dmat_ref, rowdec_ref, wend_ref, cdec_ref, gn_ref = refs[pos:pos + 5]
    pos += 5
    s0_ref = None
    if not zero_init:
        s0_ref = refs[pos]
        pos += 1
    if has_prev:
        pos += 1
    y_ref = refs[pos]
    pos += 1
    st_ref = None
    if emit_state:
        st_ref = refs[pos]
        pos += 1
    state_scr, stash_scr = refs[pos:]

    t = pl.program_id(1)
    fwd = t >= nc
    step = jnp.where(fwd, t - nc, t)
    chunk = jnp.where(fwd, t - nc, nc - 1 - t)
    row0 = pl.multiple_of(chunk * CHUNK, CHUNK)

    @pl.when(step == 0)
    def _():
        if zero_init:
            state_scr[...] = jnp.zeros_like(state_scr)
        else:
            state_scr[...] = s0_ref[...]

    q = q_ref[...]
    k = k_ref[...]
    if rope:
        cos = jnp.concatenate([cos_ref[...]] * RET_HEADS, axis=1)
        sin = jnp.concatenate([sin_ref[...]] * RET_HEADS, axis=1)
        lane = lax.broadcasted_iota(jnp.int32, (CHUNK, D_RET), 1)
        first = (lane // (RET_DK // 4)) % 2 == 0

        def rot(x):
            partner = jnp.where(first, pltpu.roll(x, D_RET - RET_DK // 4, axis=1), pltpu.roll(x, RET_DK // 4, axis=1))
            return x * cos + partner * sin

        q = rot(q)
        k = rot(k)
    k = k * (RET_DK ** -0.5)
    qb = q.astype(BF16)
    vb = v_ref[...].astype(BF16)
    rowdec = rowdec_ref[...]
    y_parts = []
    for h in range(RET_HEADS):
        sl = slice(h * RET_DK, (h + 1) * RET_DK)
        kh = k[:, sl]
        sc = _dot_nt(qb[:, sl], kh.astype(BF16)) * dmat_ref[h]
        yh = _dot(sc.astype(BF16), vb[:, sl])
        yh = yh + _dot(qb[:, sl], state_scr[h].astype(BF16)) * rowdec[:, sl]
        y_parts.append(yh)
        kt = (kh.T * wend_ref[h:h + 1, :]).astype(BF16)
        state_scr[h] = state_scr[h] * cdec_ref[h:h + 1, :] + _dot(kt, vb[:, sl])
    y = jnp.concatenate(y_parts, axis=1)

    @pl.when(jnp.logical_not(fwd))
    def _():
        stash_scr[pl.ds(row0, CHUNK), :] = y

    @pl.when(fwd)
    def _():
        yt = y + stash_scr[pl.ds(row0, CHUNK), :]
        outs = []
        for h in range(RET_HEADS):
            yh = yt[:, h * RET_DV:(h + 1) * RET_DV]
            yc = yh - jnp.mean(yh, axis=-1, keepdims=True)
            outs.append(yc * lax.rsqrt(jnp.mean(yc * yc, axis=-1, keepdims=True) + EPS))
        yn = jnp.concatenate(outs, axis=1)
        y_ref[...] = (yn * gn_ref[...] * _silu(g_ref[...])).astype(y_ref.dtype)

    if emit_state:
        @pl.when(step == nc - 1)
        def _():
            for h in range(RET_HEADS):
                st_ref[h] = state_scr[h].T


def _ret(proj, gn_g, n_seq, nc, row_off, s0, rope, layer=0, prev_states=None):
    zero_init = s0 is None
    emit_state = zero_init
    direction, chunk, rows, out_rows = _seq_specs(nc, row_off)
    dmat, rowdec, wend, cdec = _ret_consts()

    def col(c):
        return pl.BlockSpec((CHUNK, D_RET), lambda s, t: (rows(s, t), c // D_RET))

    in_specs = [col(COL_Q), col(COL_K), col(COL_V), col(COL_G)]
    args = [proj, proj, proj, proj]
    if rope:
        cos, sin = _rope_tables()
        tab = pl.BlockSpec((CHUNK, RET_DK), lambda s, t: (chunk(t), 0))
        in_specs += [tab, tab]
        args += [cos, sin]
    in_specs += [
        pl.BlockSpec((None, RET_HEADS, CHUNK, CHUNK), lambda s, t: (direction(t), 0, 0, 0)),
        pl.BlockSpec((None, CHUNK, D_RET), lambda s, t: (direction(t), 0, 0)),
        pl.BlockSpec((None, 8, CHUNK), lambda s, t: (direction(t), 0, 0)),
        pl.BlockSpec((None, 8, CHUNK), lambda s, t: (direction(t), 0, 0)),
        pl.BlockSpec((1, D_RET), lambda s, t: (0, 0)),
    ]
    args += [dmat, rowdec, wend, cdec, gn_g.reshape(1, D_RET)]
    st_spec = pl.BlockSpec((None, None, RET_HEADS, RET_DK, RET_DV), lambda s, t: (s, direction(t), 0, 0, 0))
    if not zero_init:
        in_specs.append(st_spec)
        args.append(s0)
    out_specs = [pl.BlockSpec((CHUNK, D_RET), lambda s, t: (out_rows(s, t) - row_off, 0))]
    out_shape = [jax.ShapeDtypeStruct((n_seq * nc * CHUNK, D_RET), BF16)]
    aliases = {}
    if emit_state:
        out_specs.append(pl.BlockSpec((None, None, None, RET_HEADS, RET_DV, RET_DK),
                                      lambda s, t: (s, layer, direction(t), 0, 0, 0)))
        out_shape.append(jax.ShapeDtypeStruct((n_seq, DEPTH, 2, RET_HEADS, RET_DV, RET_DK), F32))
        if prev_states is not None:
            in_specs.append(pl.BlockSpec(memory_space=pl.ANY))
            args.append(prev_states)
            aliases = {len(args) - 1: 1}
    res = pl.pallas_call(
        functools.partial(_ret_kernel, nc, zero_init, emit_state, rope, prev_states is not None),
        grid=(n_seq, 2 * nc),
        in_specs=in_specs,
        out_specs=out_specs,
        out_shape=out_shape,
        scratch_shapes=[pltpu.VMEM((RET_HEADS, RET_DK, RET_DV), F32), pltpu.VMEM((nc * CHUNK, D_RET), F32)],
        input_output_aliases=aliases,
        compiler_params=_cparams(("arbitrary", "arbitrary")),
        name="ret_ctx" if zero_init else "ret_dec",
    )(*args)
    return res if emit_state else (res[0], None)


OUTPROJ_TM = 512


def _outproj_kernel(x_ref, ya_ref, yb_ref, yc_ref, w_ref, g_ref, o_ref):
    y = _dot(ya_ref[...], w_ref[:D_SSD, :])
    y = y + _dot(yb_ref[...], w_ref[D_SSD:D_SSD + D_S5, :])
    y = y + _dot(yc_ref[...], w_ref[D_SSD + D_S5:, :])
    o_ref[...] = x_ref[...] + g_ref[...] * y


def _outproj(x, y_ssd, y_s5, y_ret, w_out, mod3, layer):
    tm = OUTPROJ_TM
    return pl.pallas_call(
        _outproj_kernel,
        grid=(N_TOK // tm,),
        in_specs=[
            pl.BlockSpec((tm, D_MODEL), lambda i: (i, 0)),
            pl.BlockSpec((tm, D_SSD), lambda i: (i, 0)),
            pl.BlockSpec((tm, D_S5), lambda i: (i, 0)),
            pl.BlockSpec((tm, D_RET), lambda i: (i, 0)),
            pl.BlockSpec((None, D_MODEL, D_MODEL), lambda i: (layer, 0, 0)),
            _mod_spec(layer, 2, tm),
        ],
        out_specs=pl.BlockSpec((tm, D_MODEL), lambda i: (i, 0)),
        out_shape=jax.ShapeDtypeStruct((N_TOK, D_MODEL), F32),
        compiler_params=_cparams(("arbitrary",)),
        name="outproj",
    )(x, y_ssd, y_s5, y_ret, w_out, mod3)


PEER_SC_TM = 256
PEER_NCAND = PEER_TOPK + 1


def _cand_pairs():
    return [(i, j) for i in range(PEER_NCAND) for j in range(PEER_NCAND) if (i + 1) * (j + 1) <= PEER_NCAND]


def _top_rows(work, n):
    rows = []
    for r in range(n):
        m = jnp.max(work, axis=0, keepdims=True)
        rows.append(m)
        if r < n - 1:
            work = jnp.where(work >= m, NEG_INF, work)
    return rows


def _peer_scores_kernel(x_ref, sh_ref, sc_ref, g_ref, wq_ref, keys_ref,
                        h_ref, thr_ref, g1_ref, s2_ref, e2_ref, cand_scr):
    tm = x_ref.shape[0]
    hb = _rms_modulate(x_ref[...], g_ref[...], sc_ref[...], sh_ref[...]).astype(BF16)
    h_ref[...] = hb
    q = _dot(hb, wq_ref[...]).astype(BF16)
    half = PEER_DQ // 2
    pairs = _cand_pairs()
    n_rows = cand_scr.shape[0]
    cand_scr[len(pairs):, :] = jnp.full((n_rows - len(pairs), tm), NEG_INF, F32)
    for h in range(PEER_HEADS):
        q1 = q[:, h * PEER_DQ:h * PEER_DQ + half]
        q2 = q[:, h * PEER_DQ + half:(h + 1) * PEER_DQ]
        s1 = _dot_nt(keys_ref[h, 0], q1)
        s2 = _dot_nt(keys_ref[h, 1], q2)
        a = _top_rows(s1, PEER_NCAND)
        b = _top_rows(s2, PEER_NCAND)
        for r, (i, j) in enumerate(pairs):
            cand_scr[r:r + 1, :] = a[i] + b[j]
        c = _top_rows(cand_scr[...], PEER_NCAND)
        top = a[0] + b[0]
        zsum = jnp.zeros_like(top)
        for r in range(PEER_TOPK):
            zsum = zsum + jnp.exp(c[r] - top)
        tau = 0.5 * (c[PEER_TOPK - 1] + c[PEER_TOPK])
        thr_ref[h] = tau - s1
        g1_ref[h] = jnp.exp(s1 - a[0]) * (0.5 / zsum)
        s2_ref[h] = s2
        e2_ref[h] = jnp.exp(s2 - b[0])


def _peer_scores(x, mod3, layer, norm_g, wq, keys):
    tm = PEER_SC_TM
    n_tok = x.shape[0]
    sc_spec = pl.BlockSpec((PEER_HEADS, PEER_NKEYS, tm), lambda i: (0, 0, i))
    sc_shape = jax.ShapeDtypeStruct((PEER_HEADS, PEER_NKEYS, n_tok), F32)
    n_cand_rows = -(-len(_cand_pairs()) // 8) * 8
    return pl.pallas_call(
        _peer_scores_kernel,
        grid=(n_tok // tm,),
        in_specs=[
            pl.BlockSpec((tm, D_MODEL), lambda i: (i, 0)),
            _mod_spec(layer, 3, tm),
            _mod_spec(layer, 4, tm),
            pl.BlockSpec((1, D_MODEL), lambda i: (0, 0)),
            pl.BlockSpec((None, D_MODEL, PEER_HEADS * PEER_DQ), lambda i: (layer, 0, 0)),
            pl.BlockSpec((None, PEER_HEADS, 2, PEER_NKEYS, PEER_DQ // 2), lambda i: (layer, 0, 0, 0, 0)),
        ],
        out_specs=[pl.BlockSpec((tm, D_MODEL), lambda i: (i, 0)), sc_spec, sc_spec, sc_spec, sc_spec],
        out_shape=[jax.ShapeDtypeStruct((n_tok, D_MODEL), BF16), sc_shape, sc_shape, sc_shape, sc_shape],
        scratch_shapes=[pltpu.VMEM((n_cand_rows, tm), F32)],
        compiler_params=_cparams(("arbitrary",)),
        name="peer_scores",
    )(x, mod3, mod3, norm_g.reshape(1, D_MODEL), wq, keys)


PEER_TM = 512
PEER_EBLK = 1024
GELU_C = math.sqrt(2.0 / math.pi)

def _peer_experts_kernel(x_ref, g2_ref, h_ref, thr_ref, g1_ref, s2_ref, e2_ref, u_ref, vt_ref,
                         o_ref, acc_scr, act_scr, w_scr, gl_scr):
    tm = x_ref.shape[0]
    eblk = u_ref.shape[0]
    j = pl.program_id(1)

    @pl.when(j == 0)
    def _():
        acc_scr[...] = jnp.zeros_like(acc_scr)

    nb = eblk // PEER_NKEYS
    rsub = 16
    na = 2

    def region(c):
        rows_c = slice(c * na * PEER_NKEYS, (c + 1) * na * PEER_NKEYS)
        act_scr[rows_c, :] = _dot_nt(u_ref[rows_c, :], h_ref[...])
        for lg in range(tm // 128):
            sl = slice(lg * 128, (lg + 1) * 128)
            for r0 in range(0, PEER_NKEYS, rsub):
                w = [jnp.zeros((rsub, 128), F32) for _ in range(na)]
                for h in range(PEER_HEADS):
                    s2 = s2_ref[h, r0:r0 + rsub, sl]
                    e2 = e2_ref[h, r0:r0 + rsub, sl]
                    for k in range(na):
                        a = c * na + k
                        w[k] = w[k] + jnp.where(s2 >= thr_ref[h, a:a + 1, sl], e2, 0.0) * g1_ref[h, a:a + 1, sl]
                for k in range(na):
                    r = (c * na + k) * PEER_NKEYS + r0
                    w_scr[r:r + rsub, sl] = w[k]

    for c in range(nb // na):
        pl.when(j + c < pl.num_programs(1) + c)(functools.partial(region, c))

    @pl.when(j + nb < pl.num_programs(1) + nb)
    def _():
        act = act_scr[...]
        inner = act * (GELU_C + (GELU_C * 0.044715) * (act * act))
        gl_scr[...] = ((act * w_scr[...]) * (1.0 + jnp.tanh(inner))).astype(BF16)

    acc_scr[...] += lax.dot_general(vt_ref[...], gl_scr[...], (((0,), (0,)), ((), ())),
                                    preferred_element_type=F32)

    @pl.when(j == pl.num_programs(1) - 1)
    def _():
        o_ref[...] = x_ref[...] + g2_ref[...] * acc_scr[...].T


def _peer_experts(x, mod3, layer, hb, thr, g1, s2, e2, u_bf, vt_bf):
    tm, eblk = PEER_TM, PEER_EBLK
    n_tok = x.shape[0]
    once = pl.Buffered(1)
    sc_spec = pl.BlockSpec((PEER_HEADS, PEER_NKEYS, tm), lambda i, j: (0, 0, i))
    k1_spec = pl.BlockSpec((PEER_HEADS, eblk // PEER_NKEYS, tm), lambda i, j: (0, j, i))
    return pl.pallas_call(
        _peer_experts_kernel,
        grid=(n_tok // tm, PEER_EXPERTS // eblk),
        in_specs=[
            pl.BlockSpec((tm, D_MODEL), lambda i, j: (i, 0), pipeline_mode=once),
            _mod_spec(layer, 5, tm),
            pl.BlockSpec((tm, D_MODEL), lambda i, j: (i, 0)),
            k1_spec, k1_spec, sc_spec, sc_spec,
            pl.BlockSpec((None, eblk, D_MODEL), lambda i, j: (layer, j, 0)),
            pl.BlockSpec((None, eblk, D_MODEL), lambda i, j: (layer, j, 0)),
        ],
        out_specs=pl.BlockSpec((tm, D_MODEL), lambda i, j: (i, 0)),
        out_shape=jax.ShapeDtypeStruct((n_tok, D_MODEL), F32),
        scratch_shapes=[pltpu.VMEM((D_MODEL, tm), F32), pltpu.VMEM((eblk, tm), F32), pltpu.VMEM((eblk, tm), F32),
                        pltpu.VMEM((eblk, tm), BF16)],
        compiler_params=_cparams(("arbitrary", "arbitrary")),
        name="peer_experts",
    )(x, mod3, hb, thr, g1, s2, e2, u_bf, vt_bf)


FINAL_TM = 512


def _final_norm_kernel(n_ctx_tiles, x_ref, g_ref, oc_ref, od_ref):
    x = x_ref[...]
    var = jnp.mean(x * x, axis=-1, keepdims=True)
    y = x * lax.rsqrt(var + EPS) * g_ref[...]
    i = pl.program_id(0)

    @pl.when(i < n_ctx_tiles)
    def _():
        oc_ref[...] = y

    @pl.when(i >= n_ctx_tiles)
    def _():
        od_ref[...] = y


def _final_norm(x, g):
    tm = FINAL_TM
    n_ctx_tiles = N_CTX_TOK // tm
    return pl.pallas_call(
        functools.partial(_final_norm_kernel, n_ctx_tiles),
        grid=(N_TOK // tm,),
        in_specs=[pl.BlockSpec((tm, D_MODEL), lambda i: (i, 0)), pl.BlockSpec((1, D_MODEL), lambda i: (0, 0))],
        out_specs=[pl.BlockSpec((tm, D_MODEL), lambda i: (jnp.minimum(i, n_ctx_tiles - 1), 0)),
                   pl.BlockSpec((tm, D_MODEL), lambda i: (jnp.maximum(i - n_ctx_tiles, 0), 0))],
        out_shape=[jax.ShapeDtypeStruct((N_CTX_TOK, D_MODEL), F32),
                   jax.ShapeDtypeStruct((N_TOK - N_CTX_TOK, D_MODEL), F32)],
        compiler_params=_cparams(("arbitrary",)),
        name="final_norm",
    )(x, g.reshape(1, D_MODEL))


def _permute_w_in(w):
    cuts = np.cumsum([D_SSD, SSD_CONV_CH, 2 * SSD_HEADS, D_S5, D_RET, D_RET, D_RET])
    z, xbc, dt, u, rq, rk, rv, rg = jnp.split(w, [int(c) for c in cuts], axis=2)
    pad = jnp.zeros((DEPTH, D_MODEL, CHUNK - SSD_HEADS), w.dtype)
    tail = jnp.zeros((DEPTH, D_MODEL, PROJ_W - COL_DT - 2 * CHUNK), w.dtype)
    out = jnp.concatenate([z, rq, rk, rv, rg, xbc, u, dt[..., :SSD_HEADS], pad, dt[..., SSD_HEADS:], pad, tail], axis=2)
    return out.astype(BF16)


def _pad_lanes(a, width):
    return jnp.pad(a, [(0, 0)] * (a.ndim - 1) + [(0, width - a.shape[-1])])


def _ssd_params(conv_w, conv_b, dt_bias, a_log, d_skip, norm_g):
    return {
        "conv_w": jnp.pad(conv_w, ((0, 8 - SSD_CONV_K), (0, 0))),
        "conv_b": conv_b.reshape(1, SSD_CONV_CH),
        "dt_bias": _pad_lanes(dt_bias, CHUNK).reshape(2, 1, CHUNK),
        "a_exp": _pad_lanes(jnp.exp(a_log), CHUNK).reshape(2, 1, CHUNK),
        "d_skip": jnp.repeat(d_skip, SSD_P).reshape(1, D_SSD),
        "norm_g": norm_g.reshape(1, D_SSD),
    }


def _s5_params(layer, k_re, k_im, p_re, p_im, b_re, b_im, c_re, c_im, d_skip, glu_w, glu_b):
    kr = k_re.reshape(2, S5_GROUPS, S5_STATE, 1)
    ki = k_im.reshape(2, S5_GROUPS, S5_STATE, 1)
    bb_re = kr * b_re - ki * b_im
    bb_im = kr * b_im + ki * b_re
    eye = jnp.eye(S5_GROUPS, dtype=F32)

    def blockdiag_in(bb):
        return jnp.einsum("gh,dgnc->dgchn", eye, bb).reshape(2, D_S5, S5_LANES)

    def blockdiag_out(cc):
        return jnp.einsum("gh,dgcn->dgnhc", eye, cc).reshape(2, S5_LANES, D_S5)

    wb = jnp.concatenate([blockdiag_in(bb_re), blockdiag_in(bb_im)], axis=2).astype(BF16)
    wc = jnp.concatenate([blockdiag_out(c_re), blockdiag_out(-c_im)], axis=1).astype(BF16)
    return {"layer": layer, "wb": wb, "wc": wc, "p_re": p_re, "p_im": p_im,
            "d_skip": d_skip.reshape(1, D_S5), "glu_w": glu_w.astype(BF16), "glu_b": glu_b.reshape(1, D_S5)}


def _ssd_state_in(s):
    b = s.shape[0]
    s = s.reshape(b, 2, SSD_GROUPS, SSD_HPG, SSD_P, SSD_N)
    return jnp.transpose(s, (0, 1, 2, 5, 3, 4)).reshape(b, 2, SSD_GROUPS, SSD_N, SSD_HPG * SSD_P)


def kernel(x_prompt, x_sample, c, state_ssd, state_s5_re, state_s5_im, state_ret, c_ctx, ada_w, ada_b, norm1_g, norm2_g, w_in, w_out, ssd_conv_w, ssd_conv_b, ssd_dt_bias, ssd_a_log, ssd_d, ssd_norm_g, s5_lambda_re, s5_lambda_im, s5_log_dt, s5_b_re, s5_b_im, s5_c_re, s5_c_im, s5_d, s5_glu_w, s5_glu_b, ret_gn_g, peer_wq, peer_keys, peer_u, peer_v, final_norm_g):
    nc_ctx = CTX_LEN // CHUNK
    nc_dec = DEC_LEN // CHUNK
    dec_row_off = N_CTX_TOK // CHUNK

    cond = jnp.concatenate([c_ctx[None, :], c, jnp.zeros((N_COND - 1 - N_DEC_SEQ, D_MODEL), F32)], axis=0)
    mod3 = _adaln(cond, ada_w, ada_b).reshape(DEPTH * N_COND * N_MOD, 1, D_MODEL)
    p_re, p_im, k_re, k_im = _s5_prep(s5_lambda_re, s5_lambda_im, s5_log_dt)

    x = jnp.concatenate([x_prompt.reshape(N_CTX_TOK, D_MODEL), x_sample.reshape(N_TOK - N_CTX_TOK, D_MODEL)], axis=0)
    st_ssd = st_ret = None
    new_re, new_im = [], []
    w_in_b = _permute_w_in(w_in)
    w_out_b = w_out.astype(BF16)
    wq_b = peer_wq.astype(BF16)
    keys_b = peer_keys.astype(BF16)
    u_b = peer_u.astype(BF16)
    vt_b = peer_v.astype(BF16)
    for l in range(DEPTH):
        proj = _inproj(x, mod3, l, norm1_g[l], w_in_b)

        sp = _ssd_params(ssd_conv_w[l], ssd_conv_b[l], ssd_dt_bias[l], ssd_a_log[l], ssd_d[l], ssd_norm_g[l])
        y_ssd_c, st_ssd = _ssd(proj, sp, N_CTX_SEQ, nc_ctx, 0, None, layer=l, prev_states=st_ssd)
        y_ssd_d, _ = _ssd(proj, sp, N_DEC_SEQ, nc_dec, dec_row_off, _ssd_state_in(state_ssd[:, l]))

        s5p = _s5_params(l, k_re[2 * l:2 * l + 2], k_im[2 * l:2 * l + 2], p_re, p_im, s5_b_re[l], s5_b_im[l],
                         s5_c_re[l], s5_c_im[l], s5_d[l], s5_glu_w[l], s5_glu_b[l])
        y_s5_c, st_re, st_im = _s5(proj, s5p, N_CTX_SEQ, nc_ctx, 0, None)
        x0 = (state_s5_re[:, l].reshape(N_DEC_SEQ, 2, 1, S5_LANES), state_s5_im[:, l].reshape(N_DEC_SEQ, 2, 1, S5_LANES))
        y_s5_d, _, _ = _s5(proj, s5p, N_DEC_SEQ, nc_dec, dec_row_off, x0)

        y_ret_c, st_ret = _ret(proj, ret_gn_g[l], N_CTX_SEQ, nc_ctx, 0, None, rope=False, layer=l, prev_states=st_ret)
        y_ret_d, _ = _ret(proj, ret_gn_g[l], N_DEC_SEQ, nc_dec, dec_row_off,
                          jnp.swapaxes(state_ret[:, l], -1, -2), rope=True)

        x = _outproj(x, jnp.concatenate([y_ssd_c, y_ssd_d]), jnp.concatenate([y_s5_c, y_s5_d]),
                     jnp.concatenate([y_ret_c, y_ret_d]), w_out_b, mod3, l)

        hb, thr, g1, s2, e2 = _peer_scores(x, mod3, l, norm2_g[l], wq_b, keys_b)
        x = _peer_experts(x, mod3, l, hb, thr, g1, s2, e2, u_b, vt_b)

        new_re.append(st_re.reshape(N_CTX_SEQ, 2, S5_GROUPS, S5_STATE))
        new_im.append(st_im.reshape(N_CTX_SEQ, 2, S5_GROUPS, S5_STATE))

    y_ctx, y_dec = _final_norm(x, final_norm_g)
    y_prompt = y_ctx.reshape(N_CTX_SEQ, CTX_LEN, D_MODEL)
    y_sample = y_dec.reshape(N_DEC_SEQ, DEC_LEN, D_MODEL)
    return (y_prompt, y_sample, st_ssd, jnp.stack(new_re, axis=1), jnp.stack(new_im, axis=1), st_ret)
```

```python
import functools
import math

import jax
import jax.numpy as jnp
import numpy as np
from jax import lax
from jax.experimental import pallas as pl
from jax.experimental.pallas import tpu as pltpu

F32 = jnp.float32
BF16 = jnp.bfloat16

D_MODEL = 2048
N_CTX_SEQ = 16
CTX_LEN = 256
N_DEC_SEQ = 2
DEC_LEN = 2048
N_CTX_TOK = N_CTX_SEQ * CTX_LEN
N_TOK = N_CTX_TOK + N_DEC_SEQ * DEC_LEN
DEPTH = 2
GRID_W = 64
CHUNK = 128

SSD_HEADS = 12
SSD_P = 64
D_SSD = SSD_HEADS * SSD_P
SSD_N = 128
SSD_GROUPS = 2
SSD_HPG = SSD_HEADS // SSD_GROUPS
SSD_CONV_K = 5
SSD_CONV_CH = D_SSD + 2 * SSD_GROUPS * SSD_N
S5_CH = 16
S5_GROUPS = 32
D_S5 = S5_CH * S5_GROUPS
S5_STATE = 64
S5_LANES = S5_GROUPS * S5_STATE
S5_TILE = 8
RET_HEADS = 6
RET_DK = 128
RET_DV = 128
D_RET = RET_HEADS * RET_DV
ROPE_BASE = 10000.0
PEER_HEADS = 8
PEER_DQ = 256
PEER_NKEYS = 128
PEER_EXPERTS = PEER_NKEYS * PEER_NKEYS
PEER_TOPK = 16
N_MOD = 6
N_COND = 8
EPS = 1e-6

COL_Z, COL_Q, COL_K, COL_V, COL_G = 0, 768, 1536, 2304, 3072
COL_XBC = 3840
COL_U = 5120
COL_DT = 5632
PROJ_W = 6144

VMEM_LIMIT = 56 * 1024 * 1024

NEG_INF = float("-inf")


def _cparams(sem, vmem_limit=VMEM_LIMIT):
    return pltpu.CompilerParams(dimension_semantics=sem, vmem_limit_bytes=vmem_limit)


def _split3(a):
    hi = a.astype(BF16)
    r1 = a - hi.astype(F32)
    mid = r1.astype(BF16)
    lo = (r1 - mid.astype(F32)).astype(BF16)
    return hi, mid, lo


def _dot(a, b):
    return jnp.dot(a, b, preferred_element_type=F32)


def _dot_split_lhs(a, b_exact):
    hi, mid, lo = _split3(a)
    return _dot(hi, b_exact) + _dot(mid, b_exact) + _dot(lo, b_exact)


def _dot_split_rhs(a_exact, b):
    hi, mid, lo = _split3(b)
    return _dot(a_exact, hi) + _dot(a_exact, mid) + _dot(a_exact, lo)


def _dot_nt(a, b):
    return lax.dot_general(a, b, (((1,), (1,)), ((), ())), preferred_element_type=F32)


def _silu(x):
    return x * jax.nn.sigmoid(x)


def _softplus(x):
    return jnp.maximum(x, 0.0) + jnp.log1p(jnp.exp(-jnp.abs(x)))


ADA_TN = 1536


def _adaln_kernel(c_ref, w_ref, b_ref, o_ref):
    s = _silu(c_ref[...]).astype(BF16)
    o_ref[...] = _dot(s, w_ref[...].astype(BF16)) + b_ref[...]


def _adaln(cond, ada_w, ada_b):
    n_out = N_MOD * D_MODEL
    return pl.pallas_call(
        _adaln_kernel,
        grid=(DEPTH, n_out // ADA_TN),
        in_specs=[
            pl.BlockSpec((N_COND, D_MODEL), lambda l, j: (0, 0)),
            pl.BlockSpec((None, D_MODEL, ADA_TN), lambda l, j: (l, 0, j)),
            pl.BlockSpec((None, 1, ADA_TN), lambda l, j: (l, 0, j)),
        ],
        out_specs=pl.BlockSpec((None, N_COND, ADA_TN), lambda l, j: (l, 0, j)),
        out_shape=jax.ShapeDtypeStruct((DEPTH, N_COND, n_out), F32),
        compiler_params=_cparams(("arbitrary", "arbitrary")),
        name="adaln",
    )(cond, ada_w, ada_b.reshape(DEPTH, 1, n_out))


def _mod_spec(layer, which, tm):
    n_ctx_tiles = N_CTX_TOK // tm
    tiles_per_dec = DEC_LEN // tm

    def index(i, *_):
        cond = jnp.where(i < n_ctx_tiles, 0, 1 + (i - n_ctx_tiles) // tiles_per_dec)
        return ((layer * N_COND + cond) * N_MOD + which, 0, 0)

    return pl.BlockSpec((None, 1, D_MODEL), index)


def _rms_modulate(x, g, sc, sh):
    var = jnp.mean(x * x, axis=-1, keepdims=True)
    y = x * lax.rsqrt(var + EPS) * g
    return y * (1.0 + sc) + sh


INPROJ_TM = 256


def _inproj_kernel(x_ref, sh_ref, sc_ref, g_ref, w_ref, o_ref):
    h = _rms_modulate(x_ref[...], g_ref[...], sc_ref[...], sh_ref[...]).astype(BF16)
    o_ref[...] = _dot(h, w_ref[...])


def _inproj(x, mod3, layer, norm_g, w_perm):
    tm = INPROJ_TM
    return pl.pallas_call(
        _inproj_kernel,
        grid=(N_TOK // tm,),
        in_specs=[
            pl.BlockSpec((tm, D_MODEL), lambda i: (i, 0)),
            _mod_spec(layer, 0, tm),
            _mod_spec(layer, 1, tm),
            pl.BlockSpec((1, D_MODEL), lambda i: (0, 0)),
            pl.BlockSpec((None, D_MODEL, PROJ_W), lambda i: (layer, 0, 0), pipeline_mode=pl.Buffered(1)),
        ],
        out_specs=pl.BlockSpec((tm, PROJ_W), lambda i: (i, 0)),
        out_shape=jax.ShapeDtypeStruct((N_TOK, PROJ_W), F32),
        compiler_params=_cparams(("arbitrary",)),
        name="inproj",
    )(x, mod3, mod3, norm_g.reshape(1, D_MODEL), w_perm)


def _walk(nc):
    def direction(t):
        return jnp.where(t < nc, 1, 0)

    def chunk(t):
        return jnp.where(t < nc, nc - 1 - t, t - nc)

    return direction, chunk


def _seq_specs(nc, row_off):
    direction, chunk = _walk(nc)

    def rows(s, t):
        return row_off + s * nc + chunk(t)

    def out_rows(s, t):
        return row_off + s * nc + jnp.where(t < nc, 0, t - nc)

    return direction, chunk, rows, out_rows


def _ssd_consts():
    idx = np.arange(CHUNK)
    tri = np.stack([(idx[None, :] <= idx[:, None]), (idx[None, :] >= idx[:, None])]).astype(np.float32)
    e_p = np.zeros((CHUNK, D_SSD), np.float32)
    e_n = np.zeros((CHUNK, SSD_HEADS * CHUNK), np.float32)
    for h in range(SSD_HEADS):
        e_p[h, h * SSD_P:(h + 1) * SSD_P] = 1.0
        e_n[h, h * CHUNK:(h + 1) * CHUNK] = 1.0
    lane = np.arange(CHUNK)
    pair = np.concatenate([np.broadcast_to(lane < SSD_P, (CHUNK, CHUNK)),
                           np.broadcast_to(lane >= SSD_P, (CHUNK, CHUNK))]).astype(np.float32)
    return tri, e_p, e_n, pair


def _ssd_kernel(nc, zero_init, emit_state, has_prev, *refs):
    (z_ref, xc_ref, xp_ref, xn_ref, dt_ref, cw_ref, cb_ref, dtb_ref, aexp_ref, dsk_ref, ng_ref,
     tri_ref, ep_ref, en_ref, pair_ref) = refs[:15]
    pos = 15
    s0_ref = None
    if not zero_init:
        s0_ref = refs[pos]
        pos += 1
    if has_prev:
        pos += 1
    y_ref = refs[pos]
    pos += 1
    st_ref = None
    if emit_state:
        st_ref = refs[pos]
        pos += 1
    state_scr, stash_scr = refs[pos:]

    t = pl.program_id(1)
    fwd = t >= nc
    step = jnp.where(fwd, t - nc, t)
    chunk = jnp.where(fwd, t - nc, nc - 1 - t)
    row0 = pl.multiple_of(chunk * CHUNK, CHUNK)

    @pl.when(step == 0)
    def _():
        if zero_init:
            state_scr[...] = jnp.zeros_like(state_scr)
        else:
            state_scr[...] = s0_ref[...]

    prev = jnp.where(chunk > 0, xp_ref[...], 0.0)
    nxt = jnp.where(chunk < nc - 1, xn_ref[...], 0.0)
    ext = jnp.concatenate([prev, xc_ref[...], nxt], axis=0)
    conv = cb_ref[...]
    for k in range(SSD_CONV_K):
        off = 8 + k - SSD_CONV_K // 2
        conv = conv + cw_ref[k:k + 1, :] * ext[off:off + CHUNK, :]
    xbc = _silu(conv)
    x = xbc[:, :D_SSD]
    bm = xbc[:, D_SSD:D_SSD + SSD_GROUPS * SSD_N]
    cm = xbc[:, D_SSD + SSD_GROUPS * SSD_N:]

    dt = _softplus(dt_ref[...] + dtb_ref[...])
    la = -dt * aexp_ref[...]
    tri = tri_ref[...]
    cum = _dot_split_rhs(tri.astype(BF16), la)
    ep = ep_ref[...]
    cum_p = _dot_split_lhs(cum, ep)
    dt_p = _dot_split_lhs(dt, ep)
    cum_col = _dot_split_lhs(cum, en_ref[...])
    cum_t = cum.T
    tot_p = jnp.where(fwd, cum_p[CHUNK - 1:CHUNK, :], cum_p[0:1, :])

    v = x * dt_p
    vb = v.astype(BF16)
    mask = tri > 0.5
    pair = pair_ref[...]
    y_parts = []
    for g in range(SSD_GROUPS):
        cg = cm[:, g * SSD_N:(g + 1) * SSD_N].astype(BF16)
        bg = bm[:, g * SSD_N:(g + 1) * SSD_N]
        gmat = _dot_nt(cg, bg.astype(BF16))
        for hp in range(SSD_HPG // 2):
            scs = []
            for h in (g * SSD_HPG + 2 * hp, g * SSD_HPG + 2 * hp + 1):
                ci = cum_col[:, h * CHUNK:(h + 1) * CHUNK]
                cj = cum_t[h:h + 1, :]
                dec = jnp.exp(jnp.where(mask, ci - cj, NEG_INF))
                scs.append((gmat * dec).astype(BF16))
            c0 = (g * SSD_HPG + 2 * hp) * SSD_P
            v2 = vb[:, c0:c0 + 2 * SSD_P]
            vv = jnp.concatenate([v2, v2], axis=0) * pair
            y_parts.append(_dot(jnp.concatenate(scs, axis=1), vv))
    y = jnp.concatenate(y_parts, axis=1)

    w_p = SSD_HPG * SSD_P
    y_off = jnp.concatenate(
        [_dot(cm[:, g * SSD_N:(g + 1) * SSD_N].astype(BF16), state_scr[g].astype(BF16))
         for g in range(SSD_GROUPS)], axis=1)
    y = y + y_off * jnp.exp(cum_p)

    vw = (v * jnp.exp(tot_p - cum_p)).astype(BF16)
    cdec = jnp.exp(tot_p)
    for g in range(SSD_GROUPS):
        bt = bm[:, g * SSD_N:(g + 1) * SSD_N].T.astype(BF16)
        state_scr[g] = state_scr[g] * cdec[:, g * w_p:(g + 1) * w_p] + _dot(bt, vw[:, g * w_p:(g + 1) * w_p])

    @pl.when(jnp.logical_not(fwd))
    def _():
        stash_scr[pl.ds(row0, CHUNK), :] = y

    @pl.when(fwd)
    def _():
        ytot = y + stash_scr[pl.ds(row0, CHUNK), :] + x * dsk_ref[...]
        gated = ytot * _silu(z_ref[...])
        var = jnp.mean(gated * gated, axis=-1, keepdims=True)
        y_ref[...] = (gated * lax.rsqrt(var + EPS) * ng_ref[...]).astype(y_ref.dtype)

    if emit_state:
        @pl.when(step == nc - 1)
        def _():
            for g in range(SSD_GROUPS):
                st_t = state_scr[g].T
                for k in range(SSD_HPG):
                    st_ref[g * SSD_HPG + k] = st_t[k * SSD_P:(k + 1) * SSD_P, :]


def _ssd(proj, p, n_seq, nc, row_off, s0, layer=0, prev_states=None):
    zero_init = s0 is None
    emit_state = zero_init
    direction, chunk, rows, out_rows = _seq_specs(nc, row_off)
    tri, e_p, e_n, pair = _ssd_consts()
    n8 = N_TOK // 8
    w_st = SSD_HPG * SSD_P
    in_specs = [
        pl.BlockSpec((CHUNK, D_SSD), lambda s, t: (rows(s, t), COL_Z // D_SSD)),
        pl.BlockSpec((CHUNK, SSD_CONV_CH), lambda s, t: (rows(s, t), COL_XBC // SSD_CONV_CH)),
        pl.BlockSpec((8, SSD_CONV_CH),
                     lambda s, t: (jnp.maximum(rows(s, t) * (CHUNK // 8) - 1, 0), COL_XBC // SSD_CONV_CH)),
        pl.BlockSpec((8, SSD_CONV_CH),
                     lambda s, t: (jnp.minimum((rows(s, t) + 1) * (CHUNK // 8), n8 - 1), COL_XBC // SSD_CONV_CH)),
        pl.BlockSpec((CHUNK, CHUNK), lambda s, t: (rows(s, t), COL_DT // CHUNK + direction(t))),
        pl.BlockSpec((8, SSD_CONV_CH), lambda s, t: (0, 0)),
        pl.BlockSpec((1, SSD_CONV_CH), lambda s, t: (0, 0)),
        pl.BlockSpec((None, 1, CHUNK), lambda s, t: (direction(t), 0, 0)),
        pl.BlockSpec((None, 1, CHUNK), lambda s, t: (direction(t), 0, 0)),
        pl.BlockSpec((1, D_SSD), lambda s, t: (0, 0)),
        pl.BlockSpec((1, D_SSD), lambda s, t: (0, 0)),
        pl.BlockSpec((None, CHUNK, CHUNK), lambda s, t: (direction(t), 0, 0)),
        pl.BlockSpec((CHUNK, D_SSD), lambda s, t: (0, 0)),
        pl.BlockSpec((CHUNK, SSD_HEADS * CHUNK), lambda s, t: (0, 0)),
        pl.BlockSpec((2 * CHUNK, CHUNK), lambda s, t: (0, 0)),
    ]
    args = [proj, proj, proj, proj, proj, p["conv_w"], p["conv_b"], p["dt_bias"], p["a_exp"], p["d_skip"],
            p["norm_g"], jnp.asarray(tri), jnp.asarray(e_p, BF16), jnp.asarray(e_n, BF16), jnp.asarray(pair, BF16)]
    st_spec = pl.BlockSpec((None, None, SSD_GROUPS, SSD_N, w_st), lambda s, t: (s, direction(t), 0, 0, 0))
    if not zero_init:
        in_specs.append(st_spec)
        args.append(s0)
    out_specs = [pl.BlockSpec((CHUNK, D_SSD), lambda s, t: (out_rows(s, t) - row_off, 0))]
    out_shape = [jax.ShapeDtypeStruct((n_seq * nc * CHUNK, D_SSD), BF16)]
    aliases = {}
    if emit_state:
        out_specs.append(pl.BlockSpec((None, None, None, SSD_HEADS, SSD_P, SSD_N),
                                      lambda s, t: (s, layer, direction(t), 0, 0, 0)))
        out_shape.append(jax.ShapeDtypeStruct((n_seq, DEPTH, 2, SSD_HEADS, SSD_P, SSD_N), F32))
        if prev_states is not None:
            in_specs.append(pl.BlockSpec(memory_space=pl.ANY))
            args.append(prev_states)
            aliases = {len(args) - 1: 1}
    res = pl.pallas_call(
        functools.partial(_ssd_kernel, nc, zero_init, emit_state, prev_states is not None),
        grid=(n_seq, 2 * nc),
        in_specs=in_specs,
        out_specs=out_specs,
        out_shape=out_shape,
        scratch_shapes=[pltpu.VMEM((SSD_GROUPS, SSD_N, w_st), F32), pltpu.VMEM((nc * CHUNK, D_SSD), F32)],
        input_output_aliases=aliases,
        compiler_params=_cparams(("arbitrary", "arbitrary")),
        name="ssd_ctx" if zero_init else "ssd_dec",
    )(*args)
    return res if emit_state else (res[0], None)


def _s5_prep_kernel(lr_ref, li_ref, ldt_ref, pre_ref, pim_ref, kre_ref, kim_ref):
    lr = lr_ref[...]
    li = li_ref[...]
    dt = jnp.exp(ldt_ref[...])
    kk = (lax.broadcasted_iota(jnp.int32, (S5_TILE, S5_LANES), 0) + 1).astype(F32)
    mag = jnp.exp(kk * (lr * dt))
    ang = kk * (li * dt)
    p_re = mag * jnp.cos(ang)
    p_im = mag * jnp.sin(ang)
    pre_ref[...] = p_re
    pim_ref[...] = p_im
    a_re = p_re[0:1, :]
    a_im = p_im[0:1, :]
    den = lr * lr + li * li
    num_re = a_re - 1.0
    kre_ref[...] = (num_re * lr + a_im * li) / den
    kim_ref[...] = (a_im * lr - num_re * li) / den


def _s5_prep(lam_re, lam_im, log_dt):
    n = DEPTH * 2
    lr = lam_re.reshape(n, 1, S5_LANES)
    li = lam_im.reshape(n, 1, S5_LANES)
    ldt = jnp.broadcast_to(log_dt[..., None], (DEPTH, 2, S5_GROUPS, S5_STATE)).reshape(n, 1, S5_LANES)
    row = pl.BlockSpec((None, 1, S5_LANES), lambda i: (i, 0, 0))
    tab = pl.BlockSpec((None, S5_TILE, S5_LANES), lambda i: (i, 0, 0))
    return pl.pallas_call(
        _s5_prep_kernel,
        grid=(n,),
        in_specs=[row, row, row],
        out_specs=[tab, tab, row, row],
        out_shape=[jax.ShapeDtypeStruct((n, S5_TILE, S5_LANES), F32)] * 2
        + [jax.ShapeDtypeStruct((n, 1, S5_LANES), F32)] * 2,
        compiler_params=_cparams(("arbitrary",)),
        name="s5_prep",
    )(lr, li, ldt)


def _s5_kernel(nc, zero_init, emit_state, *refs):
    (u_ref, jm_ref, wb_ref, pre_ref, pim_ref, wc_ref, dsk_ref, gw_ref, gb_ref) = refs[:9]
    pos = 9
    x0re_ref = x0im_ref = None
    if not zero_init:
        x0re_ref, x0im_ref = refs[pos:pos + 2]
        pos += 2
    y_ref = refs[pos]
    pos += 1
    sre_ref = sim_ref = None
    if emit_state:
        sre_ref, sim_ref = refs[pos:pos + 2]
        pos += 2
    cre_scr, cim_scr, stash_scr = refs[pos:]

    t = pl.program_id(1)
    fwd = t >= nc
    step = jnp.where(fwd, t - nc, t)
    chunk = jnp.where(fwd, t - nc, nc - 1 - t)
    row0 = pl.multiple_of(chunk * CHUNK, CHUNK)

    @pl.when(step == 0)
    def _():
        if zero_init:
            cre_scr[...] = jnp.zeros_like(cre_scr)
            cim_scr[...] = jnp.zeros_like(cim_scr)
        else:
            cre_scr[...] = x0re_ref[...]
            cim_scr[...] = x0im_ref[...]

    u = u_ref[...]
    jm = jm_ref[...]
    us = _dot(jm, u.astype(BF16)).astype(BF16)
    bu = _dot(us, wb_ref[...])
    xr = bu[:, :S5_LANES]
    xi = bu[:, S5_LANES:]
    cr = cre_scr[...]
    ci = cim_scr[...]
    pr = pre_ref[...]
    pi = pim_ref[...]
    sub = lax.broadcasted_iota(jnp.int32, (S5_TILE, S5_LANES), 0)
    steps = []
    d = 1
    while d < S5_TILE:
        keep = sub >= d
        steps.append((d, jnp.where(keep, pre_ref[d - 1:d, :], 0.0), jnp.where(keep, pim_ref[d - 1:d, :], 0.0)))
        d *= 2
    tiles_r, tiles_i = [], []
    for b in range(CHUNK // S5_TILE):
        br = xr[b * S5_TILE:(b + 1) * S5_TILE, :]
        bi = xi[b * S5_TILE:(b + 1) * S5_TILE, :]
        for d, ar, ai in steps:
            sr = pltpu.roll(br, d, axis=0)
            si = pltpu.roll(bi, d, axis=0)
            br, bi = br + (ar * sr - ai * si), bi + (ar * si + ai * sr)
        br, bi = br + (pr * cr - pi * ci), bi + (pr * ci + pi * cr)
        cr = br[S5_TILE - 1:S5_TILE, :]
        ci = bi[S5_TILE - 1:S5_TILE, :]
        tiles_r.append(br)
        tiles_i.append(bi)
    xr = jnp.concatenate(tiles_r, axis=0)
    xi = jnp.concatenate(tiles_i, axis=0)
    cre_scr[...] = cr
    cim_scr[...] = ci

    ys = _dot(xr.astype(BF16), wc_ref[:S5_LANES, :]) + _dot(xi.astype(BF16), wc_ref[S5_LANES:, :])
    yh = ys.astype(BF16)
    yl = (ys - yh.astype(F32)).astype(BF16)
    y = _dot(jm, yh) + _dot(jm, yl)

    @pl.when(jnp.logical_not(fwd))
    def _():
        stash_scr[pl.ds(row0, CHUNK), :] = y

    @pl.when(fwd)
    def _():
        yt = y + stash_scr[pl.ds(row0, CHUNK), :] + u * dsk_ref[...]
        yt = jax.nn.gelu(yt)
        gate = jax.nn.sigmoid(_dot(yt.astype(BF16), gw_ref[...]) + gb_ref[...])
        y_ref[...] = (yt * gate).astype(y_ref.dtype)

    if emit_state:
        @pl.when(step == nc - 1)
        def _():
            sre_ref[...] = cre_scr[...]
            sim_ref[...] = cim_scr[...]


def _s5(proj, p, n_seq, nc, row_off, x0):
    zero_init = x0 is None
    emit_state = zero_init
    direction, chunk, rows, out_rows = _seq_specs(nc, row_off)
    eye = np.eye(CHUNK, dtype=np.float32)
    jm = jnp.asarray(np.stack([eye, eye[::-1]]), BF16)
    lyr = p["layer"]

    def dsel(t):
        return lyr * 2 + direction(t)

    in_specs = [
        pl.BlockSpec((CHUNK, D_S5), lambda s, t: (rows(s, t), COL_U // D_S5)),
        pl.BlockSpec((None, CHUNK, CHUNK), lambda s, t: (direction(t), 0, 0)),
        pl.BlockSpec((None, D_S5, 2 * S5_LANES), lambda s, t: (direction(t), 0, 0)),
        pl.BlockSpec((None, S5_TILE, S5_LANES), lambda s, t: (dsel(t), 0, 0)),
        pl.BlockSpec((None, S5_TILE, S5_LANES), lambda s, t: (dsel(t), 0, 0)),
        pl.BlockSpec((None, 2 * S5_LANES, D_S5), lambda s, t: (direction(t), 0, 0)),
        pl.BlockSpec((1, D_S5), lambda s, t: (0, 0)),
        pl.BlockSpec((D_S5, D_S5), lambda s, t: (0, 0)),
        pl.BlockSpec((1, D_S5), lambda s, t: (0, 0)),
    ]
    args = [proj, jm, p["wb"], p["p_re"], p["p_im"], p["wc"], p["d_skip"], p["glu_w"], p["glu_b"]]
    st_spec = pl.BlockSpec((None, None, 1, S5_LANES), lambda s, t: (s, direction(t), 0, 0))
    if not zero_init:
        in_specs += [st_spec, st_spec]
        args += [x0[0], x0[1]]
    out_specs = [pl.BlockSpec((CHUNK, D_S5), lambda s, t: (out_rows(s, t) - row_off, 0))]
    out_shape = [jax.ShapeDtypeStruct((n_seq * nc * CHUNK, D_S5), BF16)]
    if emit_state:
        out_specs += [st_spec, st_spec]
        out_shape += [jax.ShapeDtypeStruct((n_seq, 2, 1, S5_LANES), F32)] * 2
    res = pl.pallas_call(
        functools.partial(_s5_kernel, nc, zero_init, emit_state),
        grid=(n_seq, 2 * nc),
        in_specs=in_specs,
        out_specs=out_specs,
        out_shape=out_shape,
        scratch_shapes=[pltpu.VMEM((1, S5_LANES), F32), pltpu.VMEM((1, S5_LANES), F32),
                        pltpu.VMEM((nc * CHUNK, D_S5), F32)],
        compiler_params=_cparams(("arbitrary", "arbitrary")),
        name="s5_ctx" if zero_init else "s5_dec",
    )(*args)
    return res if emit_state else (res[0], None, None)


def _ret_consts():
    heads = np.arange(RET_HEADS, dtype=np.float64)
    lg = np.stack([np.log1p(-np.exp2(-5.0 - heads)), np.log1p(-np.exp2(-5.5 - heads))])
    i = np.arange(CHUNK, dtype=np.float64)
    diff = i[:, None] - i[None, :]
    dmat = np.zeros((2, RET_HEADS, CHUNK, CHUNK))
    rowdec = np.zeros((2, CHUNK, D_RET))
    wend = np.zeros((2, 8, CHUNK))
    cdec = np.zeros((2, 8, CHUNK))
    for h in range(RET_HEADS):
        dmat[0, h] = np.where(diff >= 0, np.exp(lg[0, h] * diff), 0.0)
        dmat[1, h] = np.where(diff <= 0, np.exp(-lg[1, h] * diff), 0.0)
        rowdec[0, :, h * RET_DV:(h + 1) * RET_DV] = np.exp(lg[0, h] * (i + 1))[:, None]
        rowdec[1, :, h * RET_DV:(h + 1) * RET_DV] = np.exp(lg[1, h] * (CHUNK - i))[:, None]
        wend[0, h] = np.exp(lg[0, h] * (CHUNK - 1 - i))
        wend[1, h] = np.exp(lg[1, h] * i)
        cdec[:, h] = np.exp(lg[:, h] * CHUNK)[:, None]
    return [jnp.asarray(a, F32) for a in (dmat, rowdec, wend, cdec)]


def _rope_tables():
    t = np.arange(DEC_LEN)
    row = (t // GRID_W).astype(np.float32)
    col = (t % GRID_W).astype(np.float32)
    quarter = RET_DK // 4
    freqs = (ROPE_BASE ** (-np.arange(quarter, dtype=np.float32) / quarter)).astype(np.float32)
    ar = (row[:, None] * freqs[None, :]).astype(np.float64)
    ac = (col[:, None] * freqs[None, :]).astype(np.float64)
    cos = np.concatenate([np.cos(ar), np.cos(ar), np.cos(ac), np.cos(ac)], axis=1)
    sin = np.concatenate([-np.sin(ar), np.sin(ar), -np.sin(ac), np.sin(ac)], axis=1)
    return jnp.asarray(cos, F32), jnp.asarray(sin, F32)


def _ret_kernel(nc, zero_init, emit_state, rope, has_prev, *refs):
    q_ref, k_ref, v_ref, g_ref = refs[:4]
    pos = 4
    cos_ref = sin_ref = None
    if rope:
        cos_ref, sin_ref = refs[pos:pos + 2]
        pos += 2
    dmat_ref, rowdec_ref, wend_ref, cdec_ref, gn_ref = refs[pos:pos + 5]
    pos += 5
    s0_ref = None
    if not zero_init:
        s0_ref = refs[pos]
        pos += 1
    if has_prev:
        pos += 1
    y_ref = refs[pos]
    pos += 1
    st_ref = None
    if emit_state:
        st_ref = refs[pos]
        pos += 1
    state_scr, stash_scr = refs[pos:]

    t = pl.program_id(1)
    fwd = t >= nc
    step = jnp.where(fwd, t - nc, t)
    chunk = jnp.where(fwd, t - nc, nc - 1 - t)
    row0 = pl.multiple_of(chunk * CHUNK, CHUNK)

    @pl.when(step == 0)
    def _():
        if zero_init:
            state_scr[...] = jnp.zeros_like(state_scr)
        else:
            state_scr[...] = s0_ref[...]

    q = q_ref[...]
    k = k_ref[...]
    if rope:
        cos = jnp.concatenate([cos_ref[...]] * RET_HEADS, axis=1)
        sin = jnp.concatenate([sin_ref[...]] * RET_HEADS, axis=1)
        lane = lax.broadcasted_iota(jnp.int32, (CHUNK, D_RET), 1)
        first = (lane // (RET_DK // 4)) % 2 == 0

        def rot(x):
            partner = jnp.where(first, pltpu.roll(x, D_RET - RET_DK // 4, axis=1), pltpu.roll(x, RET_DK // 4, axis=1))
            return x * cos + partner * sin

        q = rot(q)
        k = rot(k)
    k = k * (RET_DK ** -0.5)
    qb = q.astype(BF16)
    vb = v_ref[...].astype(BF16)
    rowdec = rowdec_ref[...]
    y_parts = []
    for h in range(RET_HEADS):
        sl = slice(h * RET_DK, (h + 1) * RET_DK)
        kh = k[:, sl]
        sc = _dot_nt(qb[:, sl], kh.astype(BF16)) * dmat_ref[h]
        yh = _dot(sc.astype(BF16), vb[:, sl])
        yh = yh + _dot(qb[:, sl], state_scr[h].astype(BF16)) * rowdec[:, sl]
        y_parts.append(yh)
        kt = (kh.T * wend_ref[h:h + 1, :]).astype(BF16)
        state_scr[h] = state_scr[h] * cdec_ref[h:h + 1, :] + _dot(kt, vb[:, sl])
    y = jnp.concatenate(y_parts, axis=1)

    @pl.when(jnp.logical_not(fwd))
    def _():
        stash_scr[pl.ds(row0, CHUNK), :] = y

    @pl.when(fwd)
    def _():
        yt = y + stash_scr[pl.ds(row0, CHUNK), :]
        outs = []
        for h in range(RET_HEADS):
            yh = yt[:, h * RET_DV:(h + 1) * RET_DV]
            yc = yh - jnp.mean(yh, axis=-1, keepdims=True)
            outs.append(yc * lax.rsqrt(jnp.mean(yc * yc, axis=-1, keepdims=True) + EPS))
        yn = jnp.concatenate(outs, axis=1)
        y_ref[...] = (yn * gn_ref[...] * _silu(g_ref[...])).astype(y_ref.dtype)

    if emit_state:
        @pl.when(step == nc - 1)
        def _():
            for h in range(RET_HEADS):
                st_ref[h] = state_scr[h].T


def _ret(proj, gn_g, n_seq, nc, row_off, s0, rope, layer=0, prev_states=None):
    zero_init = s0 is None
    emit_state = zero_init
    direction, chunk, rows, out_rows = _seq_specs(nc, row_off)
    dmat, rowdec, wend, cdec = _ret_consts()

    def col(c):
        return pl.BlockSpec((CHUNK, D_RET), lambda s, t: (rows(s, t), c // D_RET))

    in_specs = [col(COL_Q), col(COL_K), col(COL_V), col(COL_G)]
    args = [proj, proj, proj, proj]
    if rope:
        cos, sin = _rope_tables()
        tab = pl.BlockSpec((CHUNK, RET_DK), lambda s, t: (chunk(t), 0))
        in_specs += [tab, tab]
        args += [cos, sin]
    in_specs += [
        pl.BlockSpec((None, RET_HEADS, CHUNK, CHUNK), lambda s, t: (direction(t), 0, 0, 0)),
        pl.BlockSpec((None, CHUNK, D_RET), lambda s, t: (direction(t), 0, 0)),
        pl.BlockSpec((None, 8, CHUNK), lambda s, t: (direction(t), 0, 0)),
        pl.BlockSpec((None, 8, CHUNK), lambda s, t: (direction(t), 0, 0)),
        pl.BlockSpec((1, D_RET), lambda s, t: (0, 0)),
    ]
    args += [dmat, rowdec, wend, cdec, gn_g.reshape(1, D_RET)]
    st_spec = pl.BlockSpec((None, None, RET_HEADS, RET_DK, RET_DV), lambda s, t: (s, direction(t), 0, 0, 0))
    if not zero_init:
        in_specs.append(st_spec)
        args.append(s0)
    out_specs = [pl.BlockSpec((CHUNK, D_RET), lambda s, t: (out_rows(s, t) - row_off, 0))]
    out_shape = [jax.ShapeDtypeStruct((n_seq * nc * CHUNK, D_RET), BF16)]
    aliases = {}
    if emit_state:
        out_specs.append(pl.BlockSpec((None, None, None, RET_HEADS, RET_DV, RET_DK),
                                      lambda s, t: (s, layer, direction(t), 0, 0, 0)))
        out_shape.append(jax.ShapeDtypeStruct((n_seq, DEPTH, 2, RET_HEADS, RET_DV, RET_DK), F32))
        if prev_states is not None:
            in_specs.append(pl.BlockSpec(memory_space=pl.ANY))
            args.append(prev_states)
            aliases = {len(args) - 1: 1}
    res = pl.pallas_call(
        functools.partial(_ret_kernel, nc, zero_init, emit_state, rope, prev_states is not None),
        grid=(n_seq, 2 * nc),
        in_specs=in_specs,
        out_specs=out_specs,
        out_shape=out_shape,
        scratch_shapes=[pltpu.VMEM((RET_HEADS, RET_DK, RET_DV), F32), pltpu.VMEM((nc * CHUNK, D_RET), F32)],
        input_output_aliases=aliases,
        compiler_params=_cparams(("arbitrary", "arbitrary")),
        name="ret_ctx" if zero_init else "ret_dec",
    )(*args)
    return res if emit_state else (res[0], None)


OUTPROJ_TM = 512


def _outproj_kernel(x_ref, ya_ref, yb_ref, yc_ref, w_ref, g_ref, o_ref):
    y = _dot(ya_ref[...], w_ref[:D_SSD, :])
    y = y + _dot(yb_ref[...], w_ref[D_SSD:D_SSD + D_S5, :])
    y = y + _dot(yc_ref[...], w_ref[D_SSD + D_S5:, :])
    o_ref[...] = x_ref[...] + g_ref[...] * y


def _outproj(x, y_ssd, y_s5, y_ret, w_out, mod3, layer):
    tm = OUTPROJ_TM
    return pl.pallas_call(
        _outproj_kernel,
        grid=(N_TOK // tm,),
        in_specs=[
            pl.BlockSpec((tm, D_MODEL), lambda i: (i, 0)),
            pl.BlockSpec((tm, D_SSD), lambda i: (i, 0)),
            pl.BlockSpec((tm, D_S5), lambda i: (i, 0)),
            pl.BlockSpec((tm, D_RET), lambda i: (i, 0)),
            pl.BlockSpec((None, D_MODEL, D_MODEL), lambda i: (layer, 0, 0)),
            _mod_spec(layer, 2, tm),
        ],
        out_specs=pl.BlockSpec((tm, D_MODEL), lambda i: (i, 0)),
        out_shape=jax.ShapeDtypeStruct((N_TOK, D_MODEL), F32),
        compiler_params=_cparams(("arbitrary",)),
        name="outproj",
    )(x, y_ssd, y_s5, y_ret, w_out, mod3)


PEER_SC_TM = 256
PEER_NCAND = PEER_TOPK + 1


def _cand_pairs():
    return [(i, j) for i in range(PEER_NCAND) for j in range(PEER_NCAND) if (i + 1) * (j + 1) <= PEER_NCAND]


def _top_rows(work, n):
    rows = []
    for r in range(n):
        m = jnp.max(work, axis=0, keepdims=True)
        rows.append(m)
        if r < n - 1:
            work = jnp.where(work >= m, NEG_INF, work)
    return rows


def _peer_scores_kernel(x_ref, sh_ref, sc_ref, g_ref, wq_ref, keys_ref,
                        h_ref, thr_ref, g1_ref, s2_ref, e2_ref, cand_scr):
    tm = x_ref.shape[0]
    hb = _rms_modulate(x_ref[...], g_ref[...], sc_ref[...], sh_ref[...]).astype(BF16)
    h_ref[...] = hb
    q = _dot(hb, wq_ref[...]).astype(BF16)
    half = PEER_DQ // 2
    pairs = _cand_pairs()
    n_rows = cand_scr.shape[0]
    cand_scr[len(pairs):, :] = jnp.full((n_rows - len(pairs), tm), NEG_INF, F32)
    for h in range(PEER_HEADS):
        q1 = q[:, h * PEER_DQ:h * PEER_DQ + half]
        q2 = q[:, h * PEER_DQ + half:(h + 1) * PEER_DQ]
        s1 = _dot_nt(keys_ref[h, 0], q1)
        s2 = _dot_nt(keys_ref[h, 1], q2)
        a = _top_rows(s1, PEER_NCAND)
        b = _top_rows(s2, PEER_NCAND)
        for r, (i, j) in enumerate(pairs):
            cand_scr[r:r + 1, :] = a[i] + b[j]
        c = _top_rows(cand_scr[...], PEER_NCAND)
        top = a[0] + b[0]
        zsum = jnp.zeros_like(top)
        for r in range(PEER_TOPK):
            zsum = zsum + jnp.exp(c[r] - top)
        tau = 0.5 * (c[PEER_TOPK - 1] + c[PEER_TOPK])
        thr_ref[h] = tau - s1
        g1_ref[h] = jnp.exp(s1 - a[0]) * (0.5 / zsum)
        s2_ref[h] = s2
        e2_ref[h] = jnp.exp(s2 - b[0])


def _peer_scores(x, mod3, layer, norm_g, wq, keys):
    tm = PEER_SC_TM
    n_tok = x.shape[0]
    sc_spec = pl.BlockSpec((PEER_HEADS, PEER_NKEYS, tm), lambda i: (0, 0, i))
    sc_shape = jax.ShapeDtypeStruct((PEER_HEADS, PEER_NKEYS, n_tok), F32)
    n_cand_rows = -(-len(_cand_pairs()) // 8) * 8
    return pl.pallas_call(
        _peer_scores_kernel,
        grid=(n_tok // tm,),
        in_specs=[
            pl.BlockSpec((tm, D_MODEL), lambda i: (i, 0)),
            _mod_spec(layer, 3, tm),
            _mod_spec(layer, 4, tm),
            pl.BlockSpec((1, D_MODEL), lambda i: (0, 0)),
            pl.BlockSpec((None, D_MODEL, PEER_HEADS * PEER_DQ), lambda i: (layer, 0, 0)),
            pl.BlockSpec((None, PEER_HEADS, 2, PEER_NKEYS, PEER_DQ // 2), lambda i: (layer, 0, 0, 0, 0)),
        ],
        out_specs=[pl.BlockSpec((tm, D_MODEL), lambda i: (i, 0)), sc_spec, sc_spec, sc_spec, sc_spec],
        out_shape=[jax.ShapeDtypeStruct((n_tok, D_MODEL), BF16), sc_shape, sc_shape, sc_shape, sc_shape],
        scratch_shapes=[pltpu.VMEM((n_cand_rows, tm), F32)],
        compiler_params=_cparams(("arbitrary",)),
        name="peer_scores",
    )(x, mod3, mod3, norm_g.reshape(1, D_MODEL), wq, keys)


PEER_TM = 512
PEER_EBLK = 1024
GELU_C = math.sqrt(2.0 / math.pi)

def _peer_experts_kernel(x_ref, g2_ref, h_ref, thr_ref, g1_ref, s2_ref, e2_ref, u_ref, vt_ref,
                         o_ref, acc_scr, act_scr, w_scr, gl_scr):
    tm = x_ref.shape[0]
    eblk = u_ref.shape[0]
    j = pl.program_id(1)

    @pl.when(j == 0)
    def _():
        acc_scr[...] = jnp.zeros_like(acc_scr)

    nb = eblk // PEER_NKEYS
    rsub = 16

    @pl.when(j < pl.num_programs(1))
    def _():
        act_scr[...] = _dot_nt(u_ref[...], h_ref[...])
        for lg in range(tm // 128):
            sl = slice(lg * 128, (lg + 1) * 128)
            for r0 in range(0, PEER_NKEYS, rsub):
                w = [jnp.zeros((rsub, 128), F32) for _ in range(nb)]
                for h in range(PEER_HEADS):
                    s2 = s2_ref[h, r0:r0 + rsub, sl]
                    e2 = e2_ref[h, r0:r0 + rsub, sl]
                    for a in range(nb):
                        w[a] = w[a] + jnp.where(s2 >= thr_ref[h, a:a + 1, sl], e2, 0.0) * g1_ref[h, a:a + 1, sl]
                for a in range(nb):
                    w_scr[a * PEER_NKEYS + r0:a * PEER_NKEYS + r0 + rsub, sl] = w[a]

    @pl.when(j + 1 < pl.num_programs(1) + 1)
    def _():
        act = act_scr[...]
        inner = act * (GELU_C + (GELU_C * 0.044715) * (act * act))
        gl_scr[...] = ((act * w_scr[...]) * (1.0 + jnp.tanh(inner))).astype(BF16)

    acc_scr[...] += lax.dot_general(vt_ref[...], gl_scr[...], (((0,), (0,)), ((), ())),
                                    preferred_element_type=F32)

    @pl.when(j == pl.num_programs(1) - 1)
    def _():
        o_ref[...] = x_ref[...] + g2_ref[...] * acc_scr[...].T


def _peer_experts(x, mod3, layer, hb, thr, g1, s2, e2, u_bf, vt_bf):
    tm, eblk = PEER_TM, PEER_EBLK
    n_tok = x.shape[0]
    once = pl.Buffered(1)
    sc_spec = pl.BlockSpec((PEER_HEADS, PEER_NKEYS, tm), lambda i, j: (0, 0, i))
    k1_spec = pl.BlockSpec((PEER_HEADS, eblk // PEER_NKEYS, tm), lambda i, j: (0, j, i))
    return pl.pallas_call(
        _peer_experts_kernel,
        grid=(n_tok // tm, PEER_EXPERTS // eblk),
        in_specs=[
            pl.BlockSpec((tm, D_MODEL), lambda i, j: (i, 0), pipeline_mode=once),
            _mod_spec(layer, 5, tm),
            pl.BlockSpec((tm, D_MODEL), lambda i, j: (i, 0)),
            k1_spec, k1_spec, sc_spec, sc_spec,
            pl.BlockSpec((None, eblk, D_MODEL), lambda i, j: (layer, j, 0)),
            pl.BlockSpec((None, eblk, D_MODEL), lambda i, j: (layer, j, 0)),
        ],
        out_specs=pl.BlockSpec((tm, D_MODEL), lambda i, j: (i, 0)),
        out_shape=jax.ShapeDtypeStruct((n_tok, D_MODEL), F32),
        scratch_shapes=[pltpu.VMEM((D_MODEL, tm), F32), pltpu.VMEM((eblk, tm), F32), pltpu.VMEM((eblk, tm), F32),
                        pltpu.VMEM((eblk, tm), BF16)],
        compiler_params=_cparams(("arbitrary", "arbitrary")),
        name="peer_experts",
    )(x, mod3, hb, thr, g1, s2, e2, u_bf, vt_bf)


FINAL_TM = 512


def _final_norm_kernel(n_ctx_tiles, x_ref, g_ref, oc_ref, od_ref):
    x = x_ref[...]
    var = jnp.mean(x * x, axis=-1, keepdims=True)
    y = x * lax.rsqrt(var + EPS) * g_ref[...]
    i = pl.program_id(0)

    @pl.when(i < n_ctx_tiles)
    def _():
        oc_ref[...] = y

    @pl.when(i >= n_ctx_tiles)
    def _():
        od_ref[...] = y


def _final_norm(x, g):
    tm = FINAL_TM
    n_ctx_tiles = N_CTX_TOK // tm
    return pl.pallas_call(
        functools.partial(_final_norm_kernel, n_ctx_tiles),
        grid=(N_TOK // tm,),
        in_specs=[pl.BlockSpec((tm, D_MODEL), lambda i: (i, 0)), pl.BlockSpec((1, D_MODEL), lambda i: (0, 0))],
        out_specs=[pl.BlockSpec((tm, D_MODEL), lambda i: (jnp.minimum(i, n_ctx_tiles - 1), 0)),
                   pl.BlockSpec((tm, D_MODEL), lambda i: (jnp.maximum(i - n_ctx_tiles, 0), 0))],
        out_shape=[jax.ShapeDtypeStruct((N_CTX_TOK, D_MODEL), F32),
                   jax.ShapeDtypeStruct((N_TOK - N_CTX_TOK, D_MODEL), F32)],
        compiler_params=_cparams(("arbitrary",)),
        name="final_norm",
    )(x, g.reshape(1, D_MODEL))


def _permute_w_in(w):
    cuts = np.cumsum([D_SSD, SSD_CONV_CH, 2 * SSD_HEADS, D_S5, D_RET, D_RET, D_RET])
    z, xbc, dt, u, rq, rk, rv, rg = jnp.split(w, [int(c) for c in cuts], axis=2)
    pad = jnp.zeros((DEPTH, D_MODEL, CHUNK - SSD_HEADS), w.dtype)
    tail = jnp.zeros((DEPTH, D_MODEL, PROJ_W - COL_DT - 2 * CHUNK), w.dtype)
    out = jnp.concatenate([z, rq, rk, rv, rg, xbc, u, dt[..., :SSD_HEADS], pad, dt[..., SSD_HEADS:], pad, tail], axis=2)
    return out.astype(BF16)


def _pad_lanes(a, width):
    return jnp.pad(a, [(0, 0)] * (a.ndim - 1) + [(0, width - a.shape[-1])])


def _ssd_params(conv_w, conv_b, dt_bias, a_log, d_skip, norm_g):
    return {
        "conv_w": jnp.pad(conv_w, ((0, 8 - SSD_CONV_K), (0, 0))),
        "conv_b": conv_b.reshape(1, SSD_CONV_CH),
        "dt_bias": _pad_lanes(dt_bias, CHUNK).reshape(2, 1, CHUNK),
        "a_exp": _pad_lanes(jnp.exp(a_log), CHUNK).reshape(2, 1, CHUNK),
        "d_skip": jnp.repeat(d_skip, SSD_P).reshape(1, D_SSD),
        "norm_g": norm_g.reshape(1, D_SSD),
    }


def _s5_params(layer, k_re, k_im, p_re, p_im, b_re, b_im, c_re, c_im, d_skip, glu_w, glu_b):
    kr = k_re.reshape(2, S5_GROUPS, S5_STATE, 1)
    ki = k_im.reshape(2, S5_GROUPS, S5_STATE, 1)
    bb_re = kr * b_re - ki * b_im
    bb_im = kr * b_im + ki * b_re
    eye = jnp.eye(S5_GROUPS, dtype=F32)

    def blockdiag_in(bb):
        return jnp.einsum("gh,dgnc->dgchn", eye, bb).reshape(2, D_S5, S5_LANES)

    def blockdiag_out(cc):
        return jnp.einsum("gh,dgcn->dgnhc", eye, cc).reshape(2, S5_LANES, D_S5)

    wb = jnp.concatenate([blockdiag_in(bb_re), blockdiag_in(bb_im)], axis=2).astype(BF16)
    wc = jnp.concatenate([blockdiag_out(c_re), blockdiag_out(-c_im)], axis=1).astype(BF16)
    return {"layer": layer, "wb": wb, "wc": wc, "p_re": p_re, "p_im": p_im,
            "d_skip": d_skip.reshape(1, D_S5), "glu_w": glu_w.astype(BF16), "glu_b": glu_b.reshape(1, D_S5)}


def _ssd_state_in(s):
    b = s.shape[0]
    s = s.reshape(b, 2, SSD_GROUPS, SSD_HPG, SSD_P, SSD_N)
    return jnp.transpose(s, (0, 1, 2, 5, 3, 4)).reshape(b, 2, SSD_GROUPS, SSD_N, SSD_HPG * SSD_P)


def kernel(x_prompt, x_sample, c, state_ssd, state_s5_re, state_s5_im, state_ret, c_ctx, ada_w, ada_b, norm1_g, norm2_g, w_in, w_out, ssd_conv_w, ssd_conv_b, ssd_dt_bias, ssd_a_log, ssd_d, ssd_norm_g, s5_lambda_re, s5_lambda_im, s5_log_dt, s5_b_re, s5_b_im, s5_c_re, s5_c_im, s5_d, s5_glu_w, s5_glu_b, ret_gn_g, peer_wq, peer_keys, peer_u, peer_v, final_norm_g):
    nc_ctx = CTX_LEN // CHUNK
    nc_dec = DEC_LEN // CHUNK
    dec_row_off = N_CTX_TOK // CHUNK

    cond = jnp.concatenate([c_ctx[None, :], c, jnp.zeros((N_COND - 1 - N_DEC_SEQ, D_MODEL), F32)], axis=0)
    mod3 = _adaln(cond, ada_w, ada_b).reshape(DEPTH * N_COND * N_MOD, 1, D_MODEL)
    p_re, p_im, k_re, k_im = _s5_prep(s5_lambda_re, s5_lambda_im, s5_log_dt)

    x = jnp.concatenate([x_prompt.reshape(N_CTX_TOK, D_MODEL), x_sample.reshape(N_TOK - N_CTX_TOK, D_MODEL)], axis=0)
    st_ssd = st_ret = None
    new_re, new_im = [], []
    w_in_b = _permute_w_in(w_in)
    w_out_b = w_out.astype(BF16)
    wq_b = peer_wq.astype(BF16)
    keys_b = peer_keys.astype(BF16)
    u_b = peer_u.astype(BF16)
    vt_b = peer_v.astype(BF16)
    for l in range(DEPTH):
        proj = _inproj(x, mod3, l, norm1_g[l], w_in_b)

        sp = _ssd_params(ssd_conv_w[l], ssd_conv_b[l], ssd_dt_bias[l], ssd_a_log[l], ssd_d[l], ssd_norm_g[l])
        y_ssd_c, st_ssd = _ssd(proj, sp, N_CTX_SEQ, nc_ctx, 0, None, layer=l, prev_states=st_ssd)
        y_ssd_d, _ = _ssd(proj, sp, N_DEC_SEQ, nc_dec, dec_row_off, _ssd_state_in(state_ssd[:, l]))

        s5p = _s5_params(l, k_re[2 * l:2 * l + 2], k_im[2 * l:2 * l + 2], p_re, p_im, s5_b_re[l], s5_b_im[l],
                         s5_c_re[l], s5_c_im[l], s5_d[l], s5_glu_w[l], s5_glu_b[l])
        y_s5_c, st_re, st_im = _s5(proj, s5p, N_CTX_SEQ, nc_ctx, 0, None)
        x0 = (state_s5_re[:, l].reshape(N_DEC_SEQ, 2, 1, S5_LANES), state_s5_im[:, l].reshape(N_DEC_SEQ, 2, 1, S5_LANES))
        y_s5_d, _, _ = _s5(proj, s5p, N_DEC_SEQ, nc_dec, dec_row_off, x0)

        y_ret_c, st_ret = _ret(proj, ret_gn_g[l], N_CTX_SEQ, nc_ctx, 0, None, rope=False, layer=l, prev_states=st_ret)
        y_ret_d, _ = _ret(proj, ret_gn_g[l], N_DEC_SEQ, nc_dec, dec_row_off,
                          jnp.swapaxes(state_ret[:, l], -1, -2), rope=True)

        x = _outproj(x, jnp.concatenate([y_ssd_c, y_ssd_d]), jnp.concatenate([y_s5_c, y_s5_d]),
                     jnp.concatenate([y_ret_c, y_ret_d]), w_out_b, mod3, l)

        hb, thr, g1, s2, e2 = _peer_scores(x, mod3, l, norm2_g[l], wq_b, keys_b)
        x = _peer_experts(x, mod3, l, hb, thr, g1, s2, e2, u_b, vt_b)

        new_re.append(st_re.reshape(N_CTX_SEQ, 2, S5_GROUPS, S5_STATE))
        new_im.append(st_im.reshape(N_CTX_SEQ, 2, S5_GROUPS, S5_STATE))

    y_ctx, y_dec = _final_norm(x, final_norm_g)
    y_prompt = y_ctx.reshape(N_CTX_SEQ, CTX_LEN, D_MODEL)
    y_sample = y_dec.reshape(N_DEC_SEQ, DEC_LEN, D_MODEL)
    return (y_prompt, y_sample, st_ssd, jnp.stack(new_re, axis=1), jnp.stack(new_im, axis=1), st_ret)
```

```python
import functools
import math

import jax
import jax.numpy as jnp
import numpy as np
from jax import lax
from jax.experimental import pallas as pl
from jax.experimental.pallas import tpu as pltpu

F32 = jnp.float32
BF16 = jnp.bfloat16

D_MODEL = 2048
N_CTX_SEQ = 16
CTX_LEN = 256
N_DEC_SEQ = 2
DEC_LEN = 2048
N_CTX_TOK = N_CTX_SEQ * CTX_LEN
N_TOK = N_CTX_TOK + N_DEC_SEQ * DEC_LEN
DEPTH = 2
GRID_W = 64
CHUNK = 128

SSD_HEADS = 12
SSD_P = 64
D_SSD = SSD_HEADS * SSD_P
SSD_N = 128
SSD_GROUPS = 2
SSD_HPG = SSD_HEADS // SSD_GROUPS
SSD_CONV_K = 5
SSD_CONV_CH = D_SSD + 2 * SSD_GROUPS * SSD_N
S5_CH = 16
S5_GROUPS = 32
D_S5 = S5_CH * S5_GROUPS
S5_STATE = 64
S5_LANES = S5_GROUPS * S5_STATE
S5_TILE = 8
S5_ROWS = 256
RET_HEADS = 6
RET_DK = 128
RET_DV = 128
D_RET = RET_HEADS * RET_DV
ROPE_BASE = 10000.0
PEER_HEADS = 8
PEER_DQ = 256
PEER_NKEYS = 128
PEER_EXPERTS = PEER_NKEYS * PEER_NKEYS
PEER_TOPK = 16
N_MOD = 6
N_COND = 8
EPS = 1e-6

COL_Z, COL_Q, COL_K, COL_V, COL_G = 0, 768, 1536, 2304, 3072
COL_XBC = 3840
COL_U = 5120
COL_DT = 5632
PROJ_W = 6144

VMEM_LIMIT = 56 * 1024 * 1024

NEG_INF = float("-inf")


def _cparams(sem, vmem_limit=VMEM_LIMIT):
    return pltpu.CompilerParams(dimension_semantics=sem, vmem_limit_bytes=vmem_limit)


def _split3(a):
    hi = a.astype(BF16)
    r1 = a - hi.astype(F32)
    mid = r1.astype(BF16)
    lo = (r1 - mid.astype(F32)).astype(BF16)
    return hi, mid, lo


def _dot(a, b):
    return jnp.dot(a, b, preferred_element_type=F32)


def _dot_split_lhs(a, b_exact):
    hi, mid, lo = _split3(a)
    return _dot(hi, b_exact) + _dot(mid, b_exact) + _dot(lo, b_exact)


def _dot_split_rhs(a_exact, b):
    hi, mid, lo = _split3(b)
    return _dot(a_exact, hi) + _dot(a_exact, mid) + _dot(a_exact, lo)


def _dot_nt(a, b):
    return lax.dot_general(a, b, (((1,), (1,)), ((), ())), preferred_element_type=F32)


def _silu(x):
    return x * jax.nn.sigmoid(x)


def _softplus(x):
    return jnp.maximum(x, 0.0) + jnp.log1p(jnp.exp(-jnp.abs(x)))


ADA_TN = 1536


def _adaln_kernel(c_ref, w_ref, b_ref, o_ref):
    s = _silu(c_ref[...]).astype(BF16)
    o_ref[...] = _dot(s, w_ref[...].astype(BF16)) + b_ref[...]


def _adaln(cond, ada_w, ada_b):
    n_out = N_MOD * D_MODEL
    return pl.pallas_call(
        _adaln_kernel,
        grid=(DEPTH, n_out // ADA_TN),
        in_specs=[
            pl.BlockSpec((N_COND, D_MODEL), lambda l, j: (0, 0)),
            pl.BlockSpec((None, D_MODEL, ADA_TN), lambda l, j: (l, 0, j)),
            pl.BlockSpec((None, 1, ADA_TN), lambda l, j: (l, 0, j)),
        ],
        out_specs=pl.BlockSpec((None, N_COND, ADA_TN), lambda l, j: (l, 0, j)),
        out_shape=jax.ShapeDtypeStruct((DEPTH, N_COND, n_out), F32),
        compiler_params=_cparams(("arbitrary", "arbitrary")),
        name="adaln",
    )(cond, ada_w, ada_b.reshape(DEPTH, 1, n_out))


def _mod_spec(layer, which, tm):
    n_ctx_tiles = N_CTX_TOK // tm
    tiles_per_dec = DEC_LEN // tm

    def index(i, *_):
        cond = jnp.where(i < n_ctx_tiles, 0, 1 + (i - n_ctx_tiles) // tiles_per_dec)
        return ((layer * N_COND + cond) * N_MOD + which, 0, 0)

    return pl.BlockSpec((None, 1, D_MODEL), index)


def _rms_modulate(x, g, sc, sh):
    var = jnp.mean(x * x, axis=-1, keepdims=True)
    y = x * lax.rsqrt(var + EPS) * g
    return y * (1.0 + sc) + sh


INPROJ_TM = 256


def _inproj_kernel(x_ref, sh_ref, sc_ref, g_ref, w_ref, o_ref):
    h = _rms_modulate(x_ref[...], g_ref[...], sc_ref[...], sh_ref[...]).astype(BF16)
    o_ref[...] = _dot(h, w_ref[...])


def _inproj(x, mod3, layer, norm_g, w_perm):
    tm = INPROJ_TM
    return pl.pallas_call(
        _inproj_kernel,
        grid=(N_TOK // tm,),
        in_specs=[
            pl.BlockSpec((tm, D_MODEL), lambda i: (i, 0)),
            _mod_spec(layer, 0, tm),
            _mod_spec(layer, 1, tm),
            pl.BlockSpec((1, D_MODEL), lambda i: (0, 0)),
            pl.BlockSpec((None, D_MODEL, PROJ_W), lambda i: (layer, 0, 0), pipeline_mode=pl.Buffered(1)),
        ],
        out_specs=pl.BlockSpec((tm, PROJ_W), lambda i: (i, 0)),
        out_shape=jax.ShapeDtypeStruct((N_TOK, PROJ_W), F32),
        compiler_params=_cparams(("arbitrary",)),
        name="inproj",
    )(x, mod3, mod3, norm_g.reshape(1, D_MODEL), w_perm)


def _walk(nc):
    def direction(t):
        return jnp.where(t < nc, 1, 0)

    def chunk(t):
        return jnp.where(t < nc, nc - 1 - t, t - nc)

    return direction, chunk


def _seq_specs(nc, row_off):
    direction, chunk = _walk(nc)

    def rows(s, t):
        return row_off + s * nc + chunk(t)

    def out_rows(s, t):
        return row_off + s * nc + jnp.where(t < nc, 0, t - nc)

    return direction, chunk, rows, out_rows


def _ssd_consts():
    idx = np.arange(CHUNK)
    tri = np.stack([(idx[None, :] <= idx[:, None]), (idx[None, :] >= idx[:, None])]).astype(np.float32)
    e_p = np.zeros((CHUNK, D_SSD), np.float32)
    e_n = np.zeros((CHUNK, SSD_HEADS * CHUNK), np.float32)
    for h in range(SSD_HEADS):
        e_p[h, h * SSD_P:(h + 1) * SSD_P] = 1.0
        e_n[h, h * CHUNK:(h + 1) * CHUNK] = 1.0
    lane = np.arange(CHUNK)
    pair = np.concatenate([np.broadcast_to(lane < SSD_P, (CHUNK, CHUNK)),
                           np.broadcast_to(lane >= SSD_P, (CHUNK, CHUNK))]).astype(np.float32)
    return tri, e_p, e_n, pair


def _ssd_kernel(nc, zero_init, emit_state, has_prev, *refs):
    (z_ref, xc_ref, xp_ref, xn_ref, dt_ref, cw_ref, cb_ref, dtb_ref, aexp_ref, dsk_ref, ng_ref,
     tri_ref, ep_ref, en_ref, pair_ref) = refs[:15]
    pos = 15
    s0_ref = None
    if not zero_init:
        s0_ref = refs[pos]
        pos += 1
    if has_prev:
        pos += 1
    y_ref = refs[pos]
    pos += 1
    st_ref = None
    if emit_state:
        st_ref = refs[pos]
        pos += 1
    state_scr, stash_scr = refs[pos:]

    t = pl.program_id(1)
    fwd = t >= nc
    step = jnp.where(fwd, t - nc, t)
    chunk = jnp.where(fwd, t - nc, nc - 1 - t)
    row0 = pl.multiple_of(chunk * CHUNK, CHUNK)

    @pl.when(step == 0)
    def _():
        if zero_init:
            state_scr[...] = jnp.zeros_like(state_scr)
        else:
            state_scr[...] = s0_ref[...]

    prev = jnp.where(chunk > 0, xp_ref[...], 0.0)
    nxt = jnp.where(chunk < nc - 1, xn_ref[...], 0.0)
    ext = jnp.concatenate([prev, xc_ref[...], nxt], axis=0)
    conv = cb_ref[...]
    for k in range(SSD_CONV_K):
        off = 8 + k - SSD_CONV_K // 2
        conv = conv + cw_ref[k:k + 1, :] * ext[off:off + CHUNK, :]
    xbc = _silu(conv)
    x = xbc[:, :D_SSD]
    bm = xbc[:, D_SSD:D_SSD + SSD_GROUPS * SSD_N]
    cm = xbc[:, D_SSD + SSD_GROUPS * SSD_N:]

    dt = _softplus(dt_ref[...] + dtb_ref[...])
    la = -dt * aexp_ref[...]
    tri = tri_ref[...]
    cum = _dot_split_rhs(tri.astype(BF16), la)
    ep = ep_ref[...]
    cum_p = _dot_split_lhs(cum, ep)
    dt_p = _dot_split_lhs(dt, ep)
    cum_col = _dot_split_lhs(cum, en_ref[...])
    cum_t = cum.T
    tot_p = jnp.where(fwd, cum_p[CHUNK - 1:CHUNK, :], cum_p[0:1, :])

    v = x * dt_p
    vb = v.astype(BF16)
    mask = tri > 0.5
    pair = pair_ref[...]
    y_parts = []
    for g in range(SSD_GROUPS):
        cg = cm[:, g * SSD_N:(g + 1) * SSD_N].astype(BF16)
        bg = bm[:, g * SSD_N:(g + 1) * SSD_N]
        gmat = _dot_nt(cg, bg.astype(BF16))
        for hp in range(SSD_HPG // 2):
            scs = []
            for h in (g * SSD_HPG + 2 * hp, g * SSD_HPG + 2 * hp + 1):
                ci = cum_col[:, h * CHUNK:(h + 1) * CHUNK]
                cj = cum_t[h:h + 1, :]
                dec = jnp.exp(jnp.where(mask, ci - cj, NEG_INF))
                scs.append((gmat * dec).astype(BF16))
            c0 = (g * SSD_HPG + 2 * hp) * SSD_P
            v2 = vb[:, c0:c0 + 2 * SSD_P]
            vv = jnp.concatenate([v2, v2], axis=0) * pair
            y_parts.append(_dot(jnp.concatenate(scs, axis=1), vv))
    y = jnp.concatenate(y_parts, axis=1)

    w_p = SSD_HPG * SSD_P
    y_off = jnp.concatenate(
        [_dot(cm[:, g * SSD_N:(g + 1) * SSD_N].astype(BF16), state_scr[g].astype(BF16))
         for g in range(SSD_GROUPS)], axis=1)
    y = y + y_off * jnp.exp(cum_p)

    vw = (v * jnp.exp(tot_p - cum_p)).astype(BF16)
    cdec = jnp.exp(tot_p)
    for g in range(SSD_GROUPS):
        bt = bm[:, g * SSD_N:(g + 1) * SSD_N].T.astype(BF16)
        state_scr[g] = state_scr[g] * cdec[:, g * w_p:(g + 1) * w_p] + _dot(bt, vw[:, g * w_p:(g + 1) * w_p])

    @pl.when(jnp.logical_not(fwd))
    def _():
        stash_scr[pl.ds(row0, CHUNK), :] = y

    @pl.when(fwd)
    def _():
        ytot = y + stash_scr[pl.ds(row0, CHUNK), :] + x * dsk_ref[...]
        gated = ytot * _silu(z_ref[...])
        var = jnp.mean(gated * gated, axis=-1, keepdims=True)
        y_ref[...] = (gated * lax.rsqrt(var + EPS) * ng_ref[...]).astype(y_ref.dtype)

    if emit_state:
        @pl.when(step == nc - 1)
        def _():
            for g in range(SSD_GROUPS):
                st_t = state_scr[g].T
                for k in range(SSD_HPG):
                    st_ref[g * SSD_HPG + k] = st_t[k * SSD_P:(k + 1) * SSD_P, :]


def _ssd(proj, p, n_seq, nc, row_off, s0, layer=0, prev_states=None):
    zero_init = s0 is None
    emit_state = zero_init
    direction, chunk, rows, out_rows = _seq_specs(nc, row_off)
    tri, e_p, e_n, pair = _ssd_consts()
    n8 = N_TOK // 8
    w_st = SSD_HPG * SSD_P
    in_specs = [
        pl.BlockSpec((CHUNK, D_SSD), lambda s, t: (rows(s, t), COL_Z // D_SSD)),
        pl.BlockSpec((CHUNK, SSD_CONV_CH), lambda s, t: (rows(s, t), COL_XBC // SSD_CONV_CH)),
        pl.BlockSpec((8, SSD_CONV_CH),
                     lambda s, t: (jnp.maximum(rows(s, t) * (CHUNK // 8) - 1, 0), COL_XBC // SSD_CONV_CH)),
        pl.BlockSpec((8, SSD_CONV_CH),
                     lambda s, t: (jnp.minimum((rows(s, t) + 1) * (CHUNK // 8), n8 - 1), COL_XBC // SSD_CONV_CH)),
        pl.BlockSpec((CHUNK, CHUNK), lambda s, t: (rows(s, t), COL_DT // CHUNK + direction(t))),
        pl.BlockSpec((8, SSD_CONV_CH), lambda s, t: (0, 0)),
        pl.BlockSpec((1, SSD_CONV_CH), lambda s, t: (0, 0)),
        pl.BlockSpec((None, 1, CHUNK), lambda s, t: (direction(t), 0, 0)),
        pl.BlockSpec((None, 1, CHUNK), lambda s, t: (direction(t), 0, 0)),
        pl.BlockSpec((1, D_SSD), lambda s, t: (0, 0)),
        pl.BlockSpec((1, D_SSD), lambda s, t: (0, 0)),
        pl.BlockSpec((None, CHUNK, CHUNK), lambda s, t: (direction(t), 0, 0)),
        pl.BlockSpec((CHUNK, D_SSD), lambda s, t: (0, 0)),
        pl.BlockSpec((CHUNK, SSD_HEADS * CHUNK), lambda s, t: (0, 0)),
        pl.BlockSpec((2 * CHUNK, CHUNK), lambda s, t: (0, 0)),
    ]
    args = [proj, proj, proj, proj, proj, p["conv_w"], p["conv_b"], p["dt_bias"], p["a_exp"], p["d_skip"],
            p["norm_g"], jnp.asarray(tri), jnp.asarray(e_p, BF16), jnp.asarray(e_n, BF16), jnp.asarray(pair, BF16)]
    st_spec = pl.BlockSpec((None, None, SSD_GROUPS, SSD_N, w_st), lambda s, t: (s, direction(t), 0, 0, 0))
    if not zero_init:
        in_specs.append(st_spec)
        args.append(s0)
    out_specs = [pl.BlockSpec((CHUNK, D_SSD), lambda s, t: (out_rows(s, t) - row_off, 0))]
    out_shape = [jax.ShapeDtypeStruct((n_seq * nc * CHUNK, D_SSD), BF16)]
    aliases = {}
    if emit_state:
        out_specs.append(pl.BlockSpec((None, None, None, SSD_HEADS, SSD_P, SSD_N),
                                      lambda s, t: (s, layer, direction(t), 0, 0, 0)))
        out_shape.append(jax.ShapeDtypeStruct((n_seq, DEPTH, 2, SSD_HEADS, SSD_P, SSD_N), F32))
        if prev_states is not None:
            in_specs.append(pl.BlockSpec(memory_space=pl.ANY))
            args.append(prev_states)
            aliases = {len(args) - 1: 1}
    res = pl.pallas_call(
        functools.partial(_ssd_kernel, nc, zero_init, emit_state, prev_states is not None),
        grid=(n_seq, 2 * nc),
        in_specs=in_specs,
        out_specs=out_specs,
        out_shape=out_shape,
        scratch_shapes=[pltpu.VMEM((SSD_GROUPS, SSD_N, w_st), F32), pltpu.VMEM((nc * CHUNK, D_SSD), F32)],
        input_output_aliases=aliases,
        compiler_params=_cparams(("arbitrary", "arbitrary")),
        name="ssd_ctx" if zero_init else "ssd_dec",
    )(*args)
    return res if emit_state else (res[0], None)


def _s5_prep_kernel(lr_ref, li_ref, ldt_ref, pre_ref, pim_ref, kre_ref, kim_ref):
    lr = lr_ref[...]
    li = li_ref[...]
    dt = jnp.exp(ldt_ref[...])
    kk = (lax.broadcasted_iota(jnp.int32, (S5_TILE, S5_LANES), 0) + 1).astype(F32)
    mag = jnp.exp(kk * (lr * dt))
    ang = kk * (li * dt)
    p_re = mag * jnp.cos(ang)
    p_im = mag * jnp.sin(ang)
    pre_ref[...] = p_re
    pim_ref[...] = p_im
    a_re = p_re[0:1, :]
    a_im = p_im[0:1, :]
    den = lr * lr + li * li
    num_re = a_re - 1.0
    kre_ref[...] = (num_re * lr + a_im * li) / den
    kim_ref[...] = (a_im * lr - num_re * li) / den


def _s5_prep(lam_re, lam_im, log_dt):
    n = DEPTH * 2
    lr = lam_re.reshape(n, 1, S5_LANES)
    li = lam_im.reshape(n, 1, S5_LANES)
    ldt = jnp.broadcast_to(log_dt[..., None], (DEPTH, 2, S5_GROUPS, S5_STATE)).reshape(n, 1, S5_LANES)
    row = pl.BlockSpec((None, 1, S5_LANES), lambda i: (i, 0, 0))
    tab = pl.BlockSpec((None, S5_TILE, S5_LANES), lambda i: (i, 0, 0))
    return pl.pallas_call(
        _s5_prep_kernel,
        grid=(n,),
        in_specs=[row, row, row],
        out_specs=[tab, tab, row, row],
        out_shape=[jax.ShapeDtypeStruct((n, S5_TILE, S5_LANES), F32)] * 2
        + [jax.ShapeDtypeStruct((n, 1, S5_LANES), F32)] * 2,
        compiler_params=_cparams(("arbitrary",)),
        name="s5_prep",
    )(lr, li, ldt)


def _s5_kernel(nc, zero_init, emit_state, *refs):
    (u_ref, jm_ref, wb_ref, pre_ref, pim_ref, wc_ref, dsk_ref, gw_ref, gb_ref) = refs[:9]
    pos = 9
    x0re_ref = x0im_ref = None
    if not zero_init:
        x0re_ref, x0im_ref = refs[pos:pos + 2]
        pos += 2
    y_ref = refs[pos]
    pos += 1
    sre_ref = sim_ref = None
    if emit_state:
        sre_ref, sim_ref = refs[pos:pos + 2]
        pos += 2
    cre_scr, cim_scr, stash_scr = refs[pos:]

    t = pl.program_id(1)
    fwd = t >= nc
    step = jnp.where(fwd, t - nc, t)
    chunk = jnp.where(fwd, t - nc, nc - 1 - t)
    row0 = pl.multiple_of(chunk * S5_ROWS, S5_ROWS)

    @pl.when(step == 0)
    def _():
        if zero_init:
            cre_scr[...] = jnp.zeros_like(cre_scr)
            cim_scr[...] = jnp.zeros_like(cim_scr)
        else:
            cre_scr[...] = x0re_ref[...]
            cim_scr[...] = x0im_ref[...]

    u = u_ref[...]
    jm = jm_ref[...]
    us = _dot(jm, u.astype(BF16)).astype(BF16)
    bu = _dot(us, wb_ref[...])
    xr = bu[:, :S5_LANES]
    xi = bu[:, S5_LANES:]
    cr = cre_scr[...]
    ci = cim_scr[...]
    pr = pre_ref[...]
    pi = pim_ref[...]
    sub = lax.broadcasted_iota(jnp.int32, (S5_TILE, S5_LANES), 0)
    steps = []
    d = 1
    while d < S5_TILE:
        keep = sub >= d
        steps.append((d, jnp.where(keep, pre_ref[d - 1:d, :], 0.0), jnp.where(keep, pim_ref[d - 1:d, :], 0.0)))
        d *= 2
    tiles_r, tiles_i = [], []
    for b in range(S5_ROWS // S5_TILE):
        br = xr[b * S5_TILE:(b + 1) * S5_TILE, :]
        bi = xi[b * S5_TILE:(b + 1) * S5_TILE, :]
        for d, ar, ai in steps:
            sr = pltpu.roll(br, d, axis=0)
            si = pltpu.roll(bi, d, axis=0)
            br, bi = br + (ar * sr - ai * si), bi + (ar * si + ai * sr)
        br, bi = br + (pr * cr - pi * ci), bi + (pr * ci + pi * cr)
        cr = br[S5_TILE - 1:S5_TILE, :]
        ci = bi[S5_TILE - 1:S5_TILE, :]
        tiles_r.append(br)
        tiles_i.append(bi)
    xr = jnp.concatenate(tiles_r, axis=0)
    xi = jnp.concatenate(tiles_i, axis=0)
    cre_scr[...] = cr
    cim_scr[...] = ci

    ys = _dot(xr.astype(BF16), wc_ref[:S5_LANES, :]) + _dot(xi.astype(BF16), wc_ref[S5_LANES:, :])
    yh = ys.astype(BF16)
    yl = (ys - yh.astype(F32)).astype(BF16)
    y = _dot(jm, yh) + _dot(jm, yl)

    @pl.when(jnp.logical_not(fwd))
    def _():
        stash_scr[pl.ds(row0, S5_ROWS), :] = y

    @pl.when(fwd)
    def _():
        yt = y + stash_scr[pl.ds(row0, S5_ROWS), :] + u * dsk_ref[...]
        yt = jax.nn.gelu(yt)
        gate = jax.nn.sigmoid(_dot(yt.astype(BF16), gw_ref[...]) + gb_ref[...])
        y_ref[...] = (yt * gate).astype(y_ref.dtype)

    if emit_state:
        @pl.when(step == nc - 1)
        def _():
            sre_ref[...] = cre_scr[...]
            sim_ref[...] = cim_scr[...]


def _s5(proj, p, n_seq, nc, row_off, x0):
    zero_init = x0 is None
    emit_state = zero_init
    direction, chunk, rows, out_rows = _seq_specs(nc, row_off)
    eye = np.eye(S5_ROWS, dtype=np.float32)
    jm = jnp.asarray(np.stack([eye, eye[::-1]]), BF16)
    lyr = p["layer"]

    def dsel(t):
        return lyr * 2 + direction(t)

    in_specs = [
        pl.BlockSpec((S5_ROWS, D_S5), lambda s, t: (rows(s, t), COL_U // D_S5)),
        pl.BlockSpec((None, S5_ROWS, S5_ROWS), lambda s, t: (direction(t), 0, 0)),
        pl.BlockSpec((None, D_S5, 2 * S5_LANES), lambda s, t: (direction(t), 0, 0)),
        pl.BlockSpec((None, S5_TILE, S5_LANES), lambda s, t: (dsel(t), 0, 0)),
        pl.BlockSpec((None, S5_TILE, S5_LANES), lambda s, t: (dsel(t), 0, 0)),
        pl.BlockSpec((None, 2 * S5_LANES, D_S5), lambda s, t: (direction(t), 0, 0)),
        pl.BlockSpec((1, D_S5), lambda s, t: (0, 0)),
        pl.BlockSpec((D_S5, D_S5), lambda s, t: (0, 0)),
        pl.BlockSpec((1, D_S5), lambda s, t: (0, 0)),
    ]
    args = [proj, jm, p["wb"], p["p_re"], p["p_im"], p["wc"], p["d_skip"], p["glu_w"], p["glu_b"]]
    st_spec = pl.BlockSpec((None, None, 1, S5_LANES), lambda s, t: (s, direction(t), 0, 0))
    if not zero_init:
        in_specs += [st_spec, st_spec]
        args += [x0[0], x0[1]]
    out_specs = [pl.BlockSpec((S5_ROWS, D_S5), lambda s, t: (out_rows(s, t) - row_off, 0))]
    out_shape = [jax.ShapeDtypeStruct((n_seq * nc * S5_ROWS, D_S5), BF16)]
    if emit_state:
        out_specs += [st_spec, st_spec]
        out_shape += [jax.ShapeDtypeStruct((n_seq, 2, 1, S5_LANES), F32)] * 2
    res = pl.pallas_call(
        functools.partial(_s5_kernel, nc, zero_init, emit_state),
        grid=(n_seq, 2 * nc),
        in_specs=in_specs,
        out_specs=out_specs,
        out_shape=out_shape,
        scratch_shapes=[pltpu.VMEM((1, S5_LANES), F32), pltpu.VMEM((1, S5_LANES), F32),
                        pltpu.VMEM((nc * S5_ROWS, D_S5), F32)],
        compiler_params=_cparams(("arbitrary", "arbitrary")),
        name="s5_ctx" if zero_init else "s5_dec",
    )(*args)
    return res if emit_state else (res[0], None, None)


def _ret_consts():
    heads = np.arange(RET_HEADS, dtype=np.float64)
    lg = np.stack([np.log1p(-np.exp2(-5.0 - heads)), np.log1p(-np.exp2(-5.5 - heads))])
    i = np.arange(CHUNK, dtype=np.float64)
    diff = i[:, None] - i[None, :]
    dmat = np.zeros((2, RET_HEADS, CHUNK, CHUNK))
    rowdec = np.zeros((2, CHUNK, D_RET))
    wend = np.zeros((2, 8, CHUNK))
    cdec = np.zeros((2, 8, CHUNK))
    for h in range(RET_HEADS):
        dmat[0, h] = np.where(diff >= 0, np.exp(lg[0, h] * diff), 0.0)
        dmat[1, h] = np.where(diff <= 0, np.exp(-lg[1, h] * diff), 0.0)
        rowdec[0, :, h * RET_DV:(h + 1) * RET_DV] = np.exp(lg[0, h] * (i + 1))[:, None]
        rowdec[1, :, h * RET_DV:(h + 1) * RET_DV] = np.exp(lg[1, h] * (CHUNK - i))[:, None]
        wend[0, h] = np.exp(lg[0, h] * (CHUNK - 1 - i))
        wend[1, h] = np.exp(lg[1, h] * i)
        cdec[:, h] = np.exp(lg[:, h] * CHUNK)[:, None]
    return [jnp.asarray(a, F32) for a in (dmat, rowdec, wend, cdec)]


def _rope_tables():
    t = np.arange(DEC_LEN)
    row = (t // GRID_W).astype(np.float32)
    col = (t % GRID_W).astype(np.float32)
    quarter = RET_DK // 4
    freqs = (ROPE_BASE ** (-np.arange(quarter, dtype=np.float32) / quarter)).astype(np.float32)
    ar = (row[:, None] * freqs[None, :]).astype(np.float64)
    ac = (col[:, None] * freqs[None, :]).astype(np.float64)
    cos = np.concatenate([np.cos(ar), np.cos(ar), np.cos(ac), np.cos(ac)], axis=1)
    sin = np.concatenate([-np.sin(ar), np.sin(ar), -np.sin(ac), np.sin(ac)], axis=1)
    return jnp.asarray(cos, F32), jnp.asarray(sin, F32)


def _ret_kernel(nc, zero_init, emit_state, rope, has_prev, *refs):
    q_ref, k_ref, v_ref, g_ref = refs[:4]
    pos = 4
    cos_ref = sin_ref = None
    if rope:
        cos_ref, sin_ref = refs[pos:pos + 2]
        pos += 2
    dmat_ref, rowdec_ref, wend_ref, cdec_ref, gn_ref = refs[pos:pos + 5]
    pos += 5
    s0_ref = None
    if not zero_init:
        s0_ref = refs[pos]
        pos += 1
    if has_prev:
        pos += 1
    y_ref = refs[pos]
    pos += 1
    st_ref = None
    if emit_state:
        st_ref = refs[pos]
        pos += 1
    state_scr, stash_scr = refs[pos:]

    t = pl.program_id(1)
    fwd = t >= nc
    step = jnp.where(fwd, t - nc, t)
    chunk = jnp.where(fwd, t - nc, nc - 1 - t)
    row0 = pl.multiple_of(chunk * CHUNK, CHUNK)

    @pl.when(step == 0)
    def _():
        if zero_init:
            state_scr[...] = jnp.zeros_like(state_scr)
        else:
            state_scr[...] = s0_ref[...]

    q = q_ref[...]
    k = k_ref[...]
    if rope:
        cos = jnp.concatenate([cos_ref[...]] * RET_HEADS, axis=1)
        sin = jnp.concatenate([sin_ref[...]] * RET_HEADS, axis=1)
        lane = lax.broadcasted_iota(jnp.int32, (CHUNK, D_RET), 1)
        first = (lane // (RET_DK // 4)) % 2 == 0

        def rot(x):
            partner = jnp.where(first, pltpu.roll(x, D_RET - RET_DK // 4, axis=1), pltpu.roll(x, RET_DK // 4, axis=1))
            return x * cos + partner * sin

        q = rot(q)
        k = rot(k)
    k = k * (RET_DK ** -0.5)
    qb = q.astype(BF16)
    vb = v_ref[...].astype(BF16)
    rowdec = rowdec_ref[...]
    y_parts = []
    for h in range(RET_HEADS):
        sl = slice(h * RET_DK, (h + 1) * RET_DK)
        kh = k[:, sl]
        sc = _dot_nt(qb[:, sl], kh.astype(BF16)) * dmat_ref[h]
        yh = _dot(sc.astype(BF16), vb[:, sl])
        yh = yh + _dot(qb[:, sl], state_scr[h].astype(BF16)) * rowdec[:, sl]
        y_parts.append(yh)
        kt = (kh.T * wend_ref[h:h + 1, :]).astype(BF16)
        state_scr[h] = state_scr[h] * cdec_ref[h:h + 1, :] + _dot(kt, vb[:, sl])
    y = jnp.concatenate(y_parts, axis=1)

    @pl.when(jnp.logical_not(fwd))
    def _():
        stash_scr[pl.ds(row0, CHUNK), :] = y

    @pl.when(fwd)
    def _():
        yt = y + stash_scr[pl.ds(row0, CHUNK), :]
        outs = []
        for h in range(RET_HEADS):
            yh = yt[:, h * RET_DV:(h + 1) * RET_DV]
            yc = yh - jnp.mean(yh, axis=-1, keepdims=True)
            outs.append(yc * lax.rsqrt(jnp.mean(yc * yc, axis=-1, keepdims=True) + EPS))
        yn = jnp.concatenate(outs, axis=1)
        y_ref[...] = (yn * gn_ref[...] * _silu(g_ref[...])).astype(y_ref.dtype)

    if emit_state:
        @pl.when(step == nc - 1)
        def _():
            for h in range(RET_HEADS):
                st_ref[h] = state_scr[h].T


def _ret(proj, gn_g, n_seq, nc, row_off, s0, rope, layer=0, prev_states=None):
    zero_init = s0 is None
    emit_state = zero_init
    direction, chunk, rows, out_rows = _seq_specs(nc, row_off)
    dmat, rowdec, wend, cdec = _ret_consts()

    def col(c):
        return pl.BlockSpec((CHUNK, D_RET), lambda s, t: (rows(s, t), c // D_RET))

    in_specs = [col(COL_Q), col(COL_K), col(COL_V), col(COL_G)]
    args = [proj, proj, proj, proj]
    if rope:
        cos, sin = _rope_tables()
        tab = pl.BlockSpec((CHUNK, RET_DK), lambda s, t: (chunk(t), 0))
        in_specs += [tab, tab]
        args += [cos, sin]
    in_specs += [
        pl.BlockSpec((None, RET_HEADS, CHUNK, CHUNK), lambda s, t: (direction(t), 0, 0, 0)),
        pl.BlockSpec((None, CHUNK, D_RET), lambda s, t: (direction(t), 0, 0)),
        pl.BlockSpec((None, 8, CHUNK), lambda s, t: (direction(t), 0, 0)),
        pl.BlockSpec((None, 8, CHUNK), lambda s, t: (direction(t), 0, 0)),
        pl.BlockSpec((1, D_RET), lambda s, t: (0, 0)),
    ]
    args += [dmat, rowdec, wend, cdec, gn_g.reshape(1, D_RET)]
    st_spec = pl.BlockSpec((None, None, RET_HEADS, RET_DK, RET_DV), lambda s, t: (s, direction(t), 0, 0, 0))
    if not zero_init:
        in_specs.append(st_spec)
        args.append(s0)
    out_specs = [pl.BlockSpec((CHUNK, D_RET), lambda s, t: (out_rows(s, t) - row_off, 0))]
    out_shape = [jax.ShapeDtypeStruct((n_seq * nc * CHUNK, D_RET), BF16)]
    aliases = {}
    if emit_state:
        out_specs.append(pl.BlockSpec((None, None, None, RET_HEADS, RET_DV, RET_DK),
                                      lambda s, t: (s, layer, direction(t), 0, 0, 0)))
        out_shape.append(jax.ShapeDtypeStruct((n_seq, DEPTH, 2, RET_HEADS, RET_DV, RET_DK), F32))
        if prev_states is not None:
            in_specs.append(pl.BlockSpec(memory_space=pl.ANY))
            args.append(prev_states)
            aliases = {len(args) - 1: 1}
    res = pl.pallas_call(
        functools.partial(_ret_kernel, nc, zero_init, emit_state, rope, prev_states is not None),
        grid=(n_seq, 2 * nc),
        in_specs=in_specs,
        out_specs=out_specs,
        out_shape=out_shape,
        scratch_shapes=[pltpu.VMEM((RET_HEADS, RET_DK, RET_DV), F32), pltpu.VMEM((nc * CHUNK, D_RET), F32)],
        input_output_aliases=aliases,
        compiler_params=_cparams(("arbitrary", "arbitrary")),
        name="ret_ctx" if zero_init else "ret_dec",
    )(*args)
    return res if emit_state else (res[0], None)


OUTPROJ_TM = 512


def _outproj_kernel(x_ref, ya_ref, yb_ref, yc_ref, w_ref, g_ref, o_ref):
    y = _dot(ya_ref[...], w_ref[:D_SSD, :])
    y = y + _dot(yb_ref[...], w_ref[D_SSD:D_SSD + D_S5, :])
    y = y + _dot(yc_ref[...], w_ref[D_SSD + D_S5:, :])
    o_ref[...] = x_ref[...] + g_ref[...] * y


def _outproj(x, y_ssd, y_s5, y_ret, w_out, mod3, layer):
    tm = OUTPROJ_TM
    return pl.pallas_call(
        _outproj_kernel,
        grid=(N_TOK // tm,),
        in_specs=[
            pl.BlockSpec((tm, D_MODEL), lambda i: (i, 0)),
            pl.BlockSpec((tm, D_SSD), lambda i: (i, 0)),
            pl.BlockSpec((tm, D_S5), lambda i: (i, 0)),
            pl.BlockSpec((tm, D_RET), lambda i: (i, 0)),
            pl.BlockSpec((None, D_MODEL, D_MODEL), lambda i: (layer, 0, 0)),
            _mod_spec(layer, 2, tm),
        ],
        out_specs=pl.BlockSpec((tm, D_MODEL), lambda i: (i, 0)),
        out_shape=jax.ShapeDtypeStruct((N_TOK, D_MODEL), F32),
        compiler_params=_cparams(("arbitrary",)),
        name="outproj",
    )(x, y_ssd, y_s5, y_ret, w_out, mod3)


PEER_SC_TM = 256
PEER_NCAND = PEER_TOPK + 1


def _cand_pairs():
    return [(i, j) for i in range(PEER_NCAND) for j in range(PEER_NCAND) if (i + 1) * (j + 1) <= PEER_NCAND]


def _top_rows(work, n):
    rows = []
    for r in range(n):
        m = jnp.max(work, axis=0, keepdims=True)
        rows.append(m)
        if r < n - 1:
            work = jnp.where(work >= m, NEG_INF, work)
    return rows


def _peer_scores_kernel(x_ref, sh_ref, sc_ref, g_ref, wq_ref, keys_ref,
                        h_ref, thr_ref, g1_ref, s2_ref, e2_ref, cand_scr):
    tm = x_ref.shape[0]
    hb = _rms_modulate(x_ref[...], g_ref[...], sc_ref[...], sh_ref[...]).astype(BF16)
    h_ref[...] = hb
    q = _dot(hb, wq_ref[...]).astype(BF16)
    half = PEER_DQ // 2
    pairs = _cand_pairs()
    n_rows = cand_scr.shape[0]
    cand_scr[len(pairs):, :] = jnp.full((n_rows - len(pairs), tm), NEG_INF, F32)
    for h in range(PEER_HEADS):
        q1 = q[:, h * PEER_DQ:h * PEER_DQ + half]
        q2 = q[:, h * PEER_DQ + half:(h + 1) * PEER_DQ]
        s1 = _dot_nt(keys_ref[h, 0], q1)
        s2 = _dot_nt(keys_ref[h, 1], q2)
        a = _top_rows(s1, PEER_NCAND)
        b = _top_rows(s2, PEER_NCAND)
        for r, (i, j) in enumerate(pairs):
            cand_scr[r:r + 1, :] = a[i] + b[j]
        c = _top_rows(cand_scr[...], PEER_NCAND)
        top = a[0] + b[0]
        zsum = jnp.zeros_like(top)
        for r in range(PEER_TOPK):
            zsum = zsum + jnp.exp(c[r] - top)
        tau = 0.5 * (c[PEER_TOPK - 1] + c[PEER_TOPK])
        thr_ref[h] = tau - s1
        g1_ref[h] = jnp.exp(s1 - a[0]) * (0.5 / zsum)
        s2_ref[h] = s2
        e2_ref[h] = jnp.exp(s2 - b[0])


def _peer_scores(x, mod3, layer, norm_g, wq, keys):
    tm = PEER_SC_TM
    n_tok = x.shape[0]
    sc_spec = pl.BlockSpec((PEER_HEADS, PEER_NKEYS, tm), lambda i: (0, 0, i))
    sc_shape = jax.ShapeDtypeStruct((PEER_HEADS, PEER_NKEYS, n_tok), F32)
    n_cand_rows = -(-len(_cand_pairs()) // 8) * 8
    return pl.pallas_call(
        _peer_scores_kernel,
        grid=(n_tok // tm,),
        in_specs=[
            pl.BlockSpec((tm, D_MODEL), lambda i: (i, 0)),
            _mod_spec(layer, 3, tm),
            _mod_spec(layer, 4, tm),
            pl.BlockSpec((1, D_MODEL), lambda i: (0, 0)),
            pl.BlockSpec((None, D_MODEL, PEER_HEADS * PEER_DQ), lambda i: (layer, 0, 0)),
            pl.BlockSpec((None, PEER_HEADS, 2, PEER_NKEYS, PEER_DQ // 2), lambda i: (layer, 0, 0, 0, 0)),
        ],
        out_specs=[pl.BlockSpec((tm, D_MODEL), lambda i: (i, 0)), sc_spec, sc_spec, sc_spec, sc_spec],
        out_shape=[jax.ShapeDtypeStruct((n_tok, D_MODEL), BF16), sc_shape, sc_shape, sc_shape, sc_shape],
        scratch_shapes=[pltpu.VMEM((n_cand_rows, tm), F32)],
        compiler_params=_cparams(("arbitrary",)),
        name="peer_scores",
    )(x, mod3, mod3, norm_g.reshape(1, D_MODEL), wq, keys)


PEER_TM = 512
PEER_EBLK = 1024
GELU_C = math.sqrt(2.0 / math.pi)

def _peer_experts_kernel(x_ref, g2_ref, h_ref, thr_ref, g1_ref, s2_ref, e2_ref, u_ref, vt_ref,
                         o_ref, acc_scr, act_scr, w_scr, gl_scr):
    tm = x_ref.shape[0]
    eblk = u_ref.shape[0]
    j = pl.program_id(1)

    @pl.when(j == 0)
    def _():
        acc_scr[...] = jnp.zeros_like(acc_scr)

    nb = eblk // PEER_NKEYS
    rsub = 16

    @pl.when(j < pl.num_programs(1))
    def _():
        act_scr[...] = _dot_nt(u_ref[...], h_ref[...])
        for lg in range(tm // 128):
            sl = slice(lg * 128, (lg + 1) * 128)
            for r0 in range(0, PEER_NKEYS, rsub):
                w = [jnp.zeros((rsub, 128), F32) for _ in range(nb)]
                for h in range(PEER_HEADS):
                    s2 = s2_ref[h, r0:r0 + rsub, sl]
                    e2 = e2_ref[h, r0:r0 + rsub, sl]
                    for a in range(nb):
                        w[a] = w[a] + jnp.where(s2 >= thr_ref[h, a:a + 1, sl], e2, 0.0) * g1_ref[h, a:a + 1, sl]
                for a in range(nb):
                    w_scr[a * PEER_NKEYS + r0:a * PEER_NKEYS + r0 + rsub, sl] = w[a]

    @pl.when(j + 1 < pl.num_programs(1) + 1)
    def _():
        act = act_scr[...]
        inner = act * (GELU_C + (GELU_C * 0.044715) * (act * act))
        gl_scr[...] = ((act * w_scr[...]) * (1.0 + jnp.tanh(inner))).astype(BF16)

    acc_scr[...] += lax.dot_general(vt_ref[...], gl_scr[...], (((0,), (0,)), ((), ())),
                                    preferred_element_type=F32)

    @pl.when(j == pl.num_programs(1) - 1)
    def _():
        o_ref[...] = x_ref[...] + g2_ref[...] * acc_scr[...].T


def _peer_experts(x, mod3, layer, hb, thr, g1, s2, e2, u_bf, vt_bf):
    tm, eblk = PEER_TM, PEER_EBLK
    n_tok = x.shape[0]
    once = pl.Buffered(1)
    sc_spec = pl.BlockSpec((PEER_HEADS, PEER_NKEYS, tm), lambda i, j: (0, 0, i))
    k1_spec = pl.BlockSpec((PEER_HEADS, eblk // PEER_NKEYS, tm), lambda i, j: (0, j, i))
    return pl.pallas_call(
        _peer_experts_kernel,
        grid=(n_tok // tm, PEER_EXPERTS // eblk),
        in_specs=[
            pl.BlockSpec((tm, D_MODEL), lambda i, j: (i, 0), pipeline_mode=once),
            _mod_spec(layer, 5, tm),
            pl.BlockSpec((tm, D_MODEL), lambda i, j: (i, 0)),
            k1_spec, k1_spec, sc_spec, sc_spec,
            pl.BlockSpec((None, eblk, D_MODEL), lambda i, j: (layer, j, 0)),
            pl.BlockSpec((None, eblk, D_MODEL), lambda i, j: (layer, j, 0)),
        ],
        out_specs=pl.BlockSpec((tm, D_MODEL), lambda i, j: (i, 0)),
        out_shape=jax.ShapeDtypeStruct((n_tok, D_MODEL), F32),
        scratch_shapes=[pltpu.VMEM((D_MODEL, tm), F32), pltpu.VMEM((eblk, tm), F32), pltpu.VMEM((eblk, tm), F32),
                        pltpu.VMEM((eblk, tm), BF16)],
        compiler_params=_cparams(("arbitrary", "arbitrary")),
        name="peer_experts",
    )(x, mod3, hb, thr, g1, s2, e2, u_bf, vt_bf)


FINAL_TM = 512


def _final_norm_kernel(n_ctx_tiles, x_ref, g_ref, oc_ref, od_ref):
    x = x_ref[...]
    var = jnp.mean(x * x, axis=-1, keepdims=True)
    y = x * lax.rsqrt(var + EPS) * g_ref[...]
    i = pl.program_id(0)

    @pl.when(i < n_ctx_tiles)
    def _():
        oc_ref[...] = y

    @pl.when(i >= n_ctx_tiles)
    def _():
        od_ref[...] = y


def _final_norm(x, g):
    tm = FINAL_TM
    n_ctx_tiles = N_CTX_TOK // tm
    return pl.pallas_call(
        functools.partial(_final_norm_kernel, n_ctx_tiles),
        grid=(N_TOK // tm,),
        in_specs=[pl.BlockSpec((tm, D_MODEL), lambda i: (i, 0)), pl.BlockSpec((1, D_MODEL), lambda i: (0, 0))],
        out_specs=[pl.BlockSpec((tm, D_MODEL), lambda i: (jnp.minimum(i, n_ctx_tiles - 1), 0)),
                   pl.BlockSpec((tm, D_MODEL), lambda i: (jnp.maximum(i - n_ctx_tiles, 0), 0))],
        out_shape=[jax.ShapeDtypeStruct((N_CTX_TOK, D_MODEL), F32),
                   jax.ShapeDtypeStruct((N_TOK - N_CTX_TOK, D_MODEL), F32)],
        compiler_params=_cparams(("arbitrary",)),
        name="final_norm",
    )(x, g.reshape(1, D_MODEL))


def _permute_w_in(w):
    cuts = np.cumsum([D_SSD, SSD_CONV_CH, 2 * SSD_HEADS, D_S5, D_RET, D_RET, D_RET])
    z, xbc, dt, u, rq, rk, rv, rg = jnp.split(w, [int(c) for c in cuts], axis=2)
    pad = jnp.zeros((DEPTH, D_MODEL, CHUNK - SSD_HEADS), w.dtype)
    tail = jnp.zeros((DEPTH, D_MODEL, PROJ_W - COL_DT - 2 * CHUNK), w.dtype)
    out = jnp.concatenate([z, rq, rk, rv, rg, xbc, u, dt[..., :SSD_HEADS], pad, dt[..., SSD_HEADS:], pad, tail], axis=2)
    return out.astype(BF16)


def _pad_lanes(a, width):
    return jnp.pad(a, [(0, 0)] * (a.ndim - 1) + [(0, width - a.shape[-1])])


def _ssd_params(conv_w, conv_b, dt_bias, a_log, d_skip, norm_g):
    return {
        "conv_w": jnp.pad(conv_w, ((0, 8 - SSD_CONV_K), (0, 0))),
        "conv_b": conv_b.reshape(1, SSD_CONV_CH),
        "dt_bias": _pad_lanes(dt_bias, CHUNK).reshape(2, 1, CHUNK),
        "a_exp": _pad_lanes(jnp.exp(a_log), CHUNK).reshape(2, 1, CHUNK),
        "d_skip": jnp.repeat(d_skip, SSD_P).reshape(1, D_SSD),
        "norm_g": norm_g.reshape(1, D_SSD),
    }


def _s5_params(layer, k_re, k_im, p_re, p_im, b_re, b_im, c_re, c_im, d_skip, glu_w, glu_b):
    kr = k_re.reshape(2, S5_GROUPS, S5_STATE, 1)
    ki = k_im.reshape(2, S5_GROUPS, S5_STATE, 1)
    bb_re = kr * b_re - ki * b_im
    bb_im = kr * b_im + ki * b_re
    eye = jnp.eye(S5_GROUPS, dtype=F32)

    def blockdiag_in(bb):
        return jnp.einsum("gh,dgnc->dgchn", eye, bb).reshape(2, D_S5, S5_LANES)

    def blockdiag_out(cc):
        return jnp.einsum("gh,dgcn->dgnhc", eye, cc).reshape(2, S5_LANES, D_S5)

    wb = jnp.concatenate([blockdiag_in(bb_re), blockdiag_in(bb_im)], axis=2).astype(BF16)
    wc = jnp.concatenate([blockdiag_out(c_re), blockdiag_out(-c_im)], axis=1).astype(BF16)
    return {"layer": layer, "wb": wb, "wc": wc, "p_re": p_re, "p_im": p_im,
            "d_skip": d_skip.reshape(1, D_S5), "glu_w": glu_w.astype(BF16), "glu_b": glu_b.reshape(1, D_S5)}


def _ssd_state_in(s):
    b = s.shape[0]
    s = s.reshape(b, 2, SSD_GROUPS, SSD_HPG, SSD_P, SSD_N)
    return jnp.transpose(s, (0, 1, 2, 5, 3, 4)).reshape(b, 2, SSD_GROUPS, SSD_N, SSD_HPG * SSD_P)


def kernel(x_prompt, x_sample, c, state_ssd, state_s5_re, state_s5_im, state_ret, c_ctx, ada_w, ada_b, norm1_g, norm2_g, w_in, w_out, ssd_conv_w, ssd_conv_b, ssd_dt_bias, ssd_a_log, ssd_d, ssd_norm_g, s5_lambda_re, s5_lambda_im, s5_log_dt, s5_b_re, s5_b_im, s5_c_re, s5_c_im, s5_d, s5_glu_w, s5_glu_b, ret_gn_g, peer_wq, peer_keys, peer_u, peer_v, final_norm_g):
    nc_ctx = CTX_LEN // CHUNK
    nc_dec = DEC_LEN // CHUNK
    dec_row_off = N_CTX_TOK // CHUNK

    cond = jnp.concatenate([c_ctx[None, :], c, jnp.zeros((N_COND - 1 - N_DEC_SEQ, D_MODEL), F32)], axis=0)
    mod3 = _adaln(cond, ada_w, ada_b).reshape(DEPTH * N_COND * N_MOD, 1, D_MODEL)
    p_re, p_im, k_re, k_im = _s5_prep(s5_lambda_re, s5_lambda_im, s5_log_dt)

    x = jnp.concatenate([x_prompt.reshape(N_CTX_TOK, D_MODEL), x_sample.reshape(N_TOK - N_CTX_TOK, D_MODEL)], axis=0)
    st_ssd = jnp.zeros((N_CTX_SEQ, DEPTH, 2, SSD_HEADS, SSD_P, SSD_N), F32)
    st_ret = jnp.zeros((N_CTX_SEQ, DEPTH, 2, RET_HEADS, RET_DV, RET_DK), F32)
    new_re, new_im = [], []
    w_in_b = _permute_w_in(w_in)
    w_out_b = w_out.astype(BF16)
    wq_b = peer_wq.astype(BF16)
    keys_b = peer_keys.astype(BF16)
    u_b = peer_u.astype(BF16)
    vt_b = peer_v.astype(BF16)
    for l in range(DEPTH):
        proj = _inproj(x, mod3, l, norm1_g[l], w_in_b)

        sp = _ssd_params(ssd_conv_w[l], ssd_conv_b[l], ssd_dt_bias[l], ssd_a_log[l], ssd_d[l], ssd_norm_g[l])
        y_ssd_c, st_ssd = _ssd(proj, sp, N_CTX_SEQ, nc_ctx, 0, None, layer=l, prev_states=st_ssd)
        y_ssd_d, _ = _ssd(proj, sp, N_DEC_SEQ, nc_dec, dec_row_off, _ssd_state_in(state_ssd[:, l]))

        s5p = _s5_params(l, k_re[2 * l:2 * l + 2], k_im[2 * l:2 * l + 2], p_re, p_im, s5_b_re[l], s5_b_im[l],
                         s5_c_re[l], s5_c_im[l], s5_d[l], s5_glu_w[l], s5_glu_b[l])
        y_s5_c, st_re, st_im = _s5(proj, s5p, N_CTX_SEQ, CTX_LEN // S5_ROWS, 0, None)
        x0 = (state_s5_re[:, l].reshape(N_DEC_SEQ, 2, 1, S5_LANES), state_s5_im[:, l].reshape(N_DEC_SEQ, 2, 1, S5_LANES))
        y_s5_d, _, _ = _s5(proj, s5p, N_DEC_SEQ, DEC_LEN // S5_ROWS, N_CTX_TOK // S5_ROWS, x0)

        y_ret_c, st_ret = _ret(proj, ret_gn_g[l], N_CTX_SEQ, nc_ctx, 0, None, rope=False, layer=l, prev_states=st_ret)
        y_ret_d, _ = _ret(proj, ret_gn_g[l], N_DEC_SEQ, nc_dec, dec_row_off,
                          jnp.swapaxes(state_ret[:, l], -1, -2), rope=True)

        x = _outproj(x, jnp.concatenate([y_ssd_c, y_ssd_d]), jnp.concatenate([y_s5_c, y_s5_d]),
                     jnp.concatenate([y_ret_c, y_ret_d]), w_out_b, mod3, l)

        hb, thr, g1, s2, e2 = _peer_scores(x, mod3, l, norm2_g[l], wq_b, keys_b)
        x = _peer_experts(x, mod3, l, hb, thr, g1, s2, e2, u_b, vt_b)

        new_re.append(st_re.reshape(N_CTX_SEQ, 2, S5_GROUPS, S5_STATE))
        new_im.append(st_im.reshape(N_CTX_SEQ, 2, S5_GROUPS, S5_STATE))

    y_ctx, y_dec = _final_norm(x, final_norm_g)
    y_prompt = y_ctx.reshape(N_CTX_SEQ, CTX_LEN, D_MODEL)
    y_sample = y_dec.reshape(N_DEC_SEQ, DEC_LEN, D_MODEL)
    return (y_prompt, y_sample, st_ssd, jnp.stack(new_re, axis=1), jnp.stack(new_im, axis=1), st_ret)
```

```python
import functools
import math

import jax
import jax.numpy as jnp
import numpy as np
from jax import lax
from jax.experimental import pallas as pl
from jax.experimental.pallas import tpu as pltpu

F32 = jnp.float32
BF16 = jnp.bfloat16

D_MODEL = 2048
N_CTX_SEQ = 16
CTX_LEN = 256
N_DEC_SEQ = 2
DEC_LEN = 2048
N_CTX_TOK = N_CTX_SEQ * CTX_LEN
N_TOK = N_CTX_TOK + N_DEC_SEQ * DEC_LEN
DEPTH = 2
GRID_W = 64
CHUNK = 128

SSD_HEADS = 12
SSD_P = 64
D_SSD = SSD_HEADS * SSD_P
SSD_N = 128
SSD_GROUPS = 2
SSD_HPG = SSD_HEADS // SSD_GROUPS
SSD_CONV_K = 5
SSD_CONV_CH = D_SSD + 2 * SSD_GROUPS * SSD_N
S5_CH = 16
S5_GROUPS = 32
D_S5 = S5_CH * S5_GROUPS
S5_STATE = 64
S5_LANES = S5_GROUPS * S5_STATE
S5_TILE = 8
S5_ROWS = 256
RET_HEADS = 6
RET_DK = 128
RET_DV = 128
D_RET = RET_HEADS * RET_DV
ROPE_BASE = 10000.0
PEER_HEADS = 8
PEER_DQ = 256
PEER_NKEYS = 128
PEER_EXPERTS = PEER_NKEYS * PEER_NKEYS
PEER_TOPK = 16
N_MOD = 6
N_COND = 8
EPS = 1e-6

COL_Z, COL_Q, COL_K, COL_V, COL_G = 0, 768, 1536, 2304, 3072
COL_XBC = 3840
COL_U = 5120
COL_DT = 5632
PROJ_W = 6144

VMEM_LIMIT = 56 * 1024 * 1024

NEG_INF = float("-inf")


def _cparams(sem, vmem_limit=VMEM_LIMIT):
    return pltpu.CompilerParams(dimension_semantics=sem, vmem_limit_bytes=vmem_limit)


def _split3(a):
    hi = a.astype(BF16)
    r1 = a - hi.astype(F32)
    mid = r1.astype(BF16)
    lo = (r1 - mid.astype(F32)).astype(BF16)
    return hi, mid, lo


def _dot(a, b):
    return jnp.dot(a, b, preferred_element_type=F32)


def _dot_split_lhs(a, b_exact):
    hi, mid, lo = _split3(a)
    return _dot(hi, b_exact) + _dot(mid, b_exact) + _dot(lo, b_exact)


def _dot_split_rhs(a_exact, b):
    hi, mid, lo = _split3(b)
    return _dot(a_exact, hi) + _dot(a_exact, mid) + _dot(a_exact, lo)


def _dot_nt(a, b):
    return lax.dot_general(a, b, (((1,), (1,)), ((), ())), preferred_element_type=F32)


def _silu(x):
    return x * jax.nn.sigmoid(x)


def _softplus(x):
    return jnp.maximum(x, 0.0) + jnp.log1p(jnp.exp(-jnp.abs(x)))


ADA_TN = 1536


def _adaln_kernel(c_ref, w_ref, b_ref, o_ref):
    s = _silu(c_ref[...]).astype(BF16)
    o_ref[...] = _dot(s, w_ref[...].astype(BF16)) + b_ref[...]


def _adaln(cond, ada_w, ada_b):
    n_out = N_MOD * D_MODEL
    return pl.pallas_call(
        _adaln_kernel,
        grid=(DEPTH, n_out // ADA_TN),
        in_specs=[
            pl.BlockSpec((N_COND, D_MODEL), lambda l, j: (0, 0)),
            pl.BlockSpec((None, D_MODEL, ADA_TN), lambda l, j: (l, 0, j)),
            pl.BlockSpec((None, 1, ADA_TN), lambda l, j: (l, 0, j)),
        ],
        out_specs=pl.BlockSpec((None, N_COND, ADA_TN), lambda l, j: (l, 0, j)),
        out_shape=jax.ShapeDtypeStruct((DEPTH, N_COND, n_out), F32),
        compiler_params=_cparams(("arbitrary", "arbitrary")),
        name="adaln",
    )(cond, ada_w, ada_b.reshape(DEPTH, 1, n_out))


def _mod_spec(layer, which, tm):
    n_ctx_tiles = N_CTX_TOK // tm
    tiles_per_dec = DEC_LEN // tm

    def index(i, *_):
        cond = jnp.where(i < n_ctx_tiles, 0, 1 + (i - n_ctx_tiles) // tiles_per_dec)
        return ((layer * N_COND + cond) * N_MOD + which, 0, 0)

    return pl.BlockSpec((None, 1, D_MODEL), index)


def _rms_modulate(x, g, sc, sh):
    var = jnp.mean(x * x, axis=-1, keepdims=True)
    y = x * lax.rsqrt(var + EPS) * g
    return y * (1.0 + sc) + sh


INPROJ_TM = 256


def _inproj_kernel(x_ref, sh_ref, sc_ref, g_ref, w_ref, o_ref):
    h = _rms_modulate(x_ref[...], g_ref[...], sc_ref[...], sh_ref[...]).astype(BF16)
    o_ref[...] = _dot(h, w_ref[...])


def _inproj(x, mod3, layer, norm_g, w_perm):
    tm = INPROJ_TM
    return pl.pallas_call(
        _inproj_kernel,
        grid=(N_TOK // tm,),
        in_specs=[
            pl.BlockSpec((tm, D_MODEL), lambda i: (i, 0)),
            _mod_spec(layer, 0, tm),
            _mod_spec(layer, 1, tm),
            pl.BlockSpec((1, D_MODEL), lambda i: (0, 0)),
            pl.BlockSpec((None, D_MODEL, PROJ_W), lambda i: (layer, 0, 0), pipeline_mode=pl.Buffered(1)),
        ],
        out_specs=pl.BlockSpec((tm, PROJ_W), lambda i: (i, 0)),
        out_shape=jax.ShapeDtypeStruct((N_TOK, PROJ_W), F32),
        compiler_params=_cparams(("arbitrary",)),
        name="inproj",
    )(x, mod3, mod3, norm_g.reshape(1, D_MODEL), w_perm)


def _walk(nc):
    def direction(t):
        return jnp.where(t < nc, 1, 0)

    def chunk(t):
        return jnp.where(t < nc, nc - 1 - t, t - nc)

    return direction, chunk


def _seq_specs(nc, row_off):
    direction, chunk = _walk(nc)

    def rows(s, t):
        return row_off + s * nc + chunk(t)

    def out_rows(s, t):
        return row_off + s * nc + jnp.where(t < nc, 0, t - nc)

    return direction, chunk, rows, out_rows


def _ssd_consts():
    idx = np.arange(CHUNK)
    tri = np.stack([(idx[None, :] <= idx[:, None]), (idx[None, :] >= idx[:, None])]).astype(np.float32)
    e_p = np.zeros((CHUNK, D_SSD), np.float32)
    e_n = np.zeros((CHUNK, SSD_HEADS * CHUNK), np.float32)
    for h in range(SSD_HEADS):
        e_p[h, h * SSD_P:(h + 1) * SSD_P] = 1.0
        e_n[h, h * CHUNK:(h + 1) * CHUNK] = 1.0
    lane = np.arange(CHUNK)
    pair = np.concatenate([np.broadcast_to(lane < SSD_P, (CHUNK, CHUNK)),
                           np.broadcast_to(lane >= SSD_P, (CHUNK, CHUNK))]).astype(np.float32)
    return tri, e_p, e_n, pair


def _ssd_kernel(nc, zero_init, emit_state, has_prev, *refs):
    (z_ref, xc_ref, xp_ref, xn_ref, dt_ref, cw_ref, cb_ref, dtb_ref, aexp_ref, dsk_ref, ng_ref,
     tri_ref, ep_ref, en_ref, pair_ref) = refs[:15]
    pos = 15
    s0_ref = None
    if not zero_init:
        s0_ref = refs[pos]
        pos += 1
    if has_prev:
        pos += 1
    y_ref = refs[pos]
    pos += 1
    st_ref = None
    if emit_state:
        st_ref = refs[pos]
        pos += 1
    state_scr, stash_scr = refs[pos:]

    t = pl.program_id(1)
    fwd = t >= nc
    step = jnp.where(fwd, t - nc, t)
    chunk = jnp.where(fwd, t - nc, nc - 1 - t)
    row0 = pl.multiple_of(chunk * CHUNK, CHUNK)

    @pl.when(step == 0)
    def _():
        if zero_init:
            state_scr[...] = jnp.zeros_like(state_scr)
        else:
            state_scr[...] = s0_ref[...]

    prev = jnp.where(chunk > 0, xp_ref[...], 0.0)
    nxt = jnp.where(chunk < nc - 1, xn_ref[...], 0.0)
    ext = jnp.concatenate([prev, xc_ref[...], nxt], axis=0)
    conv = cb_ref[...]
    for k in range(SSD_CONV_K):
        off = 8 + k - SSD_CONV_K // 2
        conv = conv + cw_ref[k:k + 1, :] * ext[off:off + CHUNK, :]
    xbc = _silu(conv)
    x = xbc[:, :D_SSD]
    bm = xbc[:, D_SSD:D_SSD + SSD_GROUPS * SSD_N]
    cm = xbc[:, D_SSD + SSD_GROUPS * SSD_N:]

    dt = _softplus(dt_ref[...] + dtb_ref[...])
    la = -dt * aexp_ref[...]
    tri = tri_ref[...]
    cum = _dot_split_rhs(tri.astype(BF16), la)
    ep = ep_ref[...]
    cum_p = _dot_split_lhs(cum, ep)
    dt_p = _dot_split_lhs(dt, ep)
    cum_col = _dot_split_lhs(cum, en_ref[...])
    cum_t = cum.T
    tot_p = jnp.where(fwd, cum_p[CHUNK - 1:CHUNK, :], cum_p[0:1, :])

    v = x * dt_p
    vb = v.astype(BF16)
    mask = tri > 0.5
    pair = pair_ref[...]
    y_parts = []
    for g in range(SSD_GROUPS):
        cg = cm[:, g * SSD_N:(g + 1) * SSD_N].astype(BF16)
        bg = bm[:, g * SSD_N:(g + 1) * SSD_N]
        gmat = _dot_nt(cg, bg.astype(BF16))
        for hp in range(SSD_HPG // 2):
            scs = []
            for h in (g * SSD_HPG + 2 * hp, g * SSD_HPG + 2 * hp + 1):
                ci = cum_col[:, h * CHUNK:(h + 1) * CHUNK]
                cj = cum_t[h:h + 1, :]
                dec = jnp.exp(jnp.where(mask, ci - cj, NEG_INF))
                scs.append((gmat * dec).astype(BF16))
            c0 = (g * SSD_HPG + 2 * hp) * SSD_P
            v2 = vb[:, c0:c0 + 2 * SSD_P]
            vv = jnp.concatenate([v2, v2], axis=0) * pair
            y_parts.append(_dot(jnp.concatenate(scs, axis=1), vv))
    y = jnp.concatenate(y_parts, axis=1)

    w_p = SSD_HPG * SSD_P
    y_off = jnp.concatenate(
        [_dot(cm[:, g * SSD_N:(g + 1) * SSD_N].astype(BF16), state_scr[g].astype(BF16))
         for g in range(SSD_GROUPS)], axis=1)
    y = y + y_off * jnp.exp(cum_p)

    vw = (v * jnp.exp(tot_p - cum_p)).astype(BF16)
    cdec = jnp.exp(tot_p)
    for g in range(SSD_GROUPS):
        bt = bm[:, g * SSD_N:(g + 1) * SSD_N].T.astype(BF16)
        state_scr[g] = state_scr[g] * cdec[:, g * w_p:(g + 1) * w_p] + _dot(bt, vw[:, g * w_p:(g + 1) * w_p])

    @pl.when(jnp.logical_not(fwd))
    def _():
        stash_scr[pl.ds(row0, CHUNK), :] = y

    @pl.when(fwd)
    def _():
        ytot = y + stash_scr[pl.ds(row0, CHUNK), :] + x * dsk_ref[...]
        gated = ytot * _silu(z_ref[...])
        var = jnp.mean(gated * gated, axis=-1, keepdims=True)
        y_ref[...] = (gated * lax.rsqrt(var + EPS) * ng_ref[...]).astype(y_ref.dtype)

    if emit_state:
        @pl.when(step == nc - 1)
        def _():
            for g in range(SSD_GROUPS):
                st_t = state_scr[g].T
                for k in range(SSD_HPG):
                    st_ref[g * SSD_HPG + k] = st_t[k * SSD_P:(k + 1) * SSD_P, :]


def _ssd(proj, p, n_seq, nc, row_off, s0, layer=0, prev_states=None):
    zero_init = s0 is None
    emit_state = zero_init
    direction, chunk, rows, out_rows = _seq_specs(nc, row_off)
    tri, e_p, e_n, pair = _ssd_consts()
    n8 = N_TOK // 8
    w_st = SSD_HPG * SSD_P
    in_specs = [
        pl.BlockSpec((CHUNK, D_SSD), lambda s, t: (rows(s, t), COL_Z // D_SSD)),
        pl.BlockSpec((CHUNK, SSD_CONV_CH), lambda s, t: (rows(s, t), COL_XBC // SSD_CONV_CH)),
        pl.BlockSpec((8, SSD_CONV_CH),
                     lambda s, t: (jnp.maximum(rows(s, t) * (CHUNK // 8) - 1, 0), COL_XBC // SSD_CONV_CH)),
        pl.BlockSpec((8, SSD_CONV_CH),
                     lambda s, t: (jnp.minimum((rows(s, t) + 1) * (CHUNK // 8), n8 - 1), COL_XBC // SSD_CONV_CH)),
        pl.BlockSpec((CHUNK, CHUNK), lambda s, t: (rows(s, t), COL_DT // CHUNK + direction(t))),
        pl.BlockSpec((8, SSD_CONV_CH), lambda s, t: (0, 0)),
        pl.BlockSpec((1, SSD_CONV_CH), lambda s, t: (0, 0)),
        pl.BlockSpec((None, 1, CHUNK), lambda s, t: (direction(t), 0, 0)),
        pl.BlockSpec((None, 1, CHUNK), lambda s, t: (direction(t), 0, 0)),
        pl.BlockSpec((1, D_SSD), lambda s, t: (0, 0)),
        pl.BlockSpec((1, D_SSD), lambda s, t: (0, 0)),
        pl.BlockSpec((None, CHUNK, CHUNK), lambda s, t: (direction(t), 0, 0)),
        pl.BlockSpec((CHUNK, D_SSD), lambda s, t: (0, 0)),
        pl.BlockSpec((CHUNK, SSD_HEADS * CHUNK), lambda s, t: (0, 0)),
        pl.BlockSpec((2 * CHUNK, CHUNK), lambda s, t: (0, 0)),
    ]
    args = [proj, proj, proj, proj, proj, p["conv_w"], p["conv_b"], p["dt_bias"], p["a_exp"], p["d_skip"],
            p["norm_g"], jnp.asarray(tri), jnp.asarray(e_p, BF16), jnp.asarray(e_n, BF16), jnp.asarray(pair, BF16)]
    st_spec = pl.BlockSpec((None, None, SSD_GROUPS, SSD_N, w_st), lambda s, t: (s, direction(t), 0, 0, 0))
    if not zero_init:
        in_specs.append(st_spec)
        args.append(s0)
    out_specs = [pl.BlockSpec((CHUNK, D_SSD), lambda s, t: (out_rows(s, t) - row_off, 0))]
    out_shape = [jax.ShapeDtypeStruct((n_seq * nc * CHUNK, D_SSD), BF16)]
    aliases = {}
    if emit_state:
        out_specs.append(pl.BlockSpec((None, None, None, SSD_HEADS, SSD_P, SSD_N),
                                      lambda s, t: (s, layer, direction(t), 0, 0, 0)))
        out_shape.append(jax.ShapeDtypeStruct((n_seq, DEPTH, 2, SSD_HEADS, SSD_P, SSD_N), F32))
        if prev_states is not None:
            in_specs.append(pl.BlockSpec(memory_space=pl.ANY))
            args.append(prev_states)
            aliases = {len(args) - 1: 1}
    res = pl.pallas_call(
        functools.partial(_ssd_kernel, nc, zero_init, emit_state, prev_states is not None),
        grid=(n_seq, 2 * nc),
        in_specs=in_specs,
        out_specs=out_specs,
        out_shape=out_shape,
        scratch_shapes=[pltpu.VMEM((SSD_GROUPS, SSD_N, w_st), F32), pltpu.VMEM((nc * CHUNK, D_SSD), F32)],
        input_output_aliases=aliases,
        compiler_params=_cparams(("arbitrary", "arbitrary")),
        name="ssd_ctx" if zero_init else "ssd_dec",
    )(*args)
    return res if emit_state else (res[0], None)


def _s5_prep_kernel(lr_ref, li_ref, ldt_ref, pre_ref, pim_ref, kre_ref, kim_ref):
    lr = lr_ref[...]
    li = li_ref[...]
    dt = jnp.exp(ldt_ref[...])
    kk = (lax.broadcasted_iota(jnp.int32, (S5_TILE, S5_LANES), 0) + 1).astype(F32)
    mag = jnp.exp(kk * (lr * dt))
    ang = kk * (li * dt)
    p_re = mag * jnp.cos(ang)
    p_im = mag * jnp.sin(ang)
    pre_ref[...] = p_re
    pim_ref[...] = p_im
    a_re = p_re[0:1, :]
    a_im = p_im[0:1, :]
    den = lr * lr + li * li
    num_re = a_re - 1.0
    kre_ref[...] = (num_re * lr + a_im * li) / den
    kim_ref[...] = (a_im * lr - num_re * li) / den


def _s5_prep(lam_re, lam_im, log_dt):
    n = DEPTH * 2
    lr = lam_re.reshape(n, 1, S5_LANES)
    li = lam_im.reshape(n, 1, S5_LANES)
    ldt = jnp.broadcast_to(log_dt[..., None], (DEPTH, 2, S5_GROUPS, S5_STATE)).reshape(n, 1, S5_LANES)
    row = pl.BlockSpec((None, 1, S5_LANES), lambda i: (i, 0, 0))
    tab = pl.BlockSpec((None, S5_TILE, S5_LANES), lambda i: (i, 0, 0))
    return pl.pallas_call(
        _s5_prep_kernel,
        grid=(n,),
        in_specs=[row, row, row],
        out_specs=[tab, tab, row, row],
        out_shape=[jax.ShapeDtypeStruct((n, S5_TILE, S5_LANES), F32)] * 2
        + [jax.ShapeDtypeStruct((n, 1, S5_LANES), F32)] * 2,
        compiler_params=_cparams(("arbitrary",)),
        name="s5_prep",
    )(lr, li, ldt)


def _s5_kernel(nc, zero_init, emit_state, *refs):
    (u_ref, jm_ref, wb_ref, pre_ref, pim_ref, wc_ref, dsk_ref, gw_ref, gb_ref) = refs[:9]
    pos = 9
    x0re_ref = x0im_ref = None
    if not zero_init:
        x0re_ref, x0im_ref = refs[pos:pos + 2]
        pos += 2
    y_ref = refs[pos]
    pos += 1
    sre_ref = sim_ref = None
    if emit_state:
        sre_ref, sim_ref = refs[pos:pos + 2]
        pos += 2
    cre_scr, cim_scr, stash_scr = refs[pos:]

    t = pl.program_id(1)
    fwd = t >= nc
    step = jnp.where(fwd, t - nc, t)
    chunk = jnp.where(fwd, t - nc, nc - 1 - t)
    row0 = pl.multiple_of(chunk * S5_ROWS, S5_ROWS)

    @pl.when(step == 0)
    def _():
        if zero_init:
            cre_scr[...] = jnp.zeros_like(cre_scr)
            cim_scr[...] = jnp.zeros_like(cim_scr)
        else:
            cre_scr[...] = x0re_ref[...]
            cim_scr[...] = x0im_ref[...]

    u = u_ref[...]
    jm = jm_ref[...]
    us = _dot(jm, u.astype(BF16)).astype(BF16)
    bu = _dot(us, wb_ref[...])
    xr = bu[:, :S5_LANES]
    xi = bu[:, S5_LANES:]
    cr = cre_scr[...]
    ci = cim_scr[...]
    pr = pre_ref[...]
    pi = pim_ref[...]
    sub = lax.broadcasted_iota(jnp.int32, (S5_TILE, S5_LANES), 0)
    steps = []
    d = 1
    while d < S5_TILE:
        keep = sub >= d
        steps.append((d, jnp.where(keep, pre_ref[d - 1:d, :], 0.0), jnp.where(keep, pim_ref[d - 1:d, :], 0.0)))
        d *= 2
    tiles_r, tiles_i = [], []
    for b in range(S5_ROWS // S5_TILE):
        br = xr[b * S5_TILE:(b + 1) * S5_TILE, :]
        bi = xi[b * S5_TILE:(b + 1) * S5_TILE, :]
        for d, ar, ai in steps:
            sr = pltpu.roll(br, d, axis=0)
            si = pltpu.roll(bi, d, axis=0)
            br, bi = br + (ar * sr - ai * si), bi + (ar * si + ai * sr)
        br, bi = br + (pr * cr - pi * ci), bi + (pr * ci + pi * cr)
        cr = br[S5_TILE - 1:S5_TILE, :]
        ci = bi[S5_TILE - 1:S5_TILE, :]
        tiles_r.append(br)
        tiles_i.append(bi)
    xr = jnp.concatenate(tiles_r, axis=0)
    xi = jnp.concatenate(tiles_i, axis=0)
    cre_scr[...] = cr
    cim_scr[...] = ci

    ys = _dot(xr.astype(BF16), wc_ref[:S5_LANES, :]) + _dot(xi.astype(BF16), wc_ref[S5_LANES:, :])
    yh = ys.astype(BF16)
    yl = (ys - yh.astype(F32)).astype(BF16)
    y = _dot(jm, yh) + _dot(jm, yl)

    @pl.when(jnp.logical_not(fwd))
    def _():
        stash_scr[pl.ds(row0, S5_ROWS), :] = y

    @pl.when(fwd)
    def _():
        yt = y + stash_scr[pl.ds(row0, S5_ROWS), :] + u * dsk_ref[...]
        yt = jax.nn.gelu(yt)
        gate = jax.nn.sigmoid(_dot(yt.astype(BF16), gw_ref[...]) + gb_ref[...])
        y_ref[...] = (yt * gate).astype(y_ref.dtype)

    if emit_state:
        @pl.when(step == nc - 1)
        def _():
            sre_ref[...] = cre_scr[...]
            sim_ref[...] = cim_scr[...]


def _s5(proj, p, n_seq, nc, row_off, x0):
    zero_init = x0 is None
    emit_state = zero_init
    direction, chunk, rows, out_rows = _seq_specs(nc, row_off)
    eye = np.eye(S5_ROWS, dtype=np.float32)
    jm = jnp.asarray(np.stack([eye, eye[::-1]]), BF16)
    lyr = p["layer"]

    def dsel(t):
        return lyr * 2 + direction(t)

    in_specs = [
        pl.BlockSpec((S5_ROWS, D_S5), lambda s, t: (rows(s, t), COL_U // D_S5)),
        pl.BlockSpec((None, S5_ROWS, S5_ROWS), lambda s, t: (direction(t), 0, 0)),
        pl.BlockSpec((None, D_S5, 2 * S5_LANES), lambda s, t: (direction(t), 0, 0)),
        pl.BlockSpec((None, S5_TILE, S5_LANES), lambda s, t: (dsel(t), 0, 0)),
        pl.BlockSpec((None, S5_TILE, S5_LANES), lambda s, t: (dsel(t), 0, 0)),
        pl.BlockSpec((None, 2 * S5_LANES, D_S5), lambda s, t: (direction(t), 0, 0)),
        pl.BlockSpec((1, D_S5), lambda s, t: (0, 0)),
        pl.BlockSpec((D_S5, D_S5), lambda s, t: (0, 0)),
        pl.BlockSpec((1, D_S5), lambda s, t: (0, 0)),
    ]
    args = [proj, jm, p["wb"], p["p_re"], p["p_im"], p["wc"], p["d_skip"], p["glu_w"], p["glu_b"]]
    st_spec = pl.BlockSpec((None, None, 1, S5_LANES), lambda s, t: (s, direction(t), 0, 0))
    if not zero_init:
        in_specs += [st_spec, st_spec]
        args += [x0[0], x0[1]]
    out_specs = [pl.BlockSpec((S5_ROWS, D_S5), lambda s, t: (out_rows(s, t) - row_off, 0))]
    out_shape = [jax.ShapeDtypeStruct((n_seq * nc * S5_ROWS, D_S5), BF16)]
    if emit_state:
        out_specs += [st_spec, st_spec]
        out_shape += [jax.ShapeDtypeStruct((n_seq, 2, 1, S5_LANES), F32)] * 2
    res = pl.pallas_call(
        functools.partial(_s5_kernel, nc, zero_init, emit_state),
        grid=(n_seq, 2 * nc),
        in_specs=in_specs,
        out_specs=out_specs,
        out_shape=out_shape,
        scratch_shapes=[pltpu.VMEM((1, S5_LANES), F32), pltpu.VMEM((1, S5_LANES), F32),
                        pltpu.VMEM((nc * S5_ROWS, D_S5), F32)],
        compiler_params=_cparams(("arbitrary", "arbitrary")),
        name="s5_ctx" if zero_init else "s5_dec",
    )(*args)
    return res if emit_state else (res[0], None, None)


def _ret_consts():
    heads = np.arange(RET_HEADS, dtype=np.float64)
    lg = np.stack([np.log1p(-np.exp2(-5.0 - heads)), np.log1p(-np.exp2(-5.5 - heads))])
    i = np.arange(CHUNK, dtype=np.float64)
    diff = i[:, None] - i[None, :]
    dmat = np.zeros((2, RET_HEADS, CHUNK, CHUNK))
    rowdec = np.zeros((2, CHUNK, D_RET))
    wend = np.zeros((2, 8, CHUNK))
    cdec = np.zeros((2, 8, CHUNK))
    for h in range(RET_HEADS):
        dmat[0, h] = np.where(diff >= 0, np.exp(lg[0, h] * diff), 0.0)
        dmat[1, h] = np.where(diff <= 0, np.exp(-lg[1, h] * diff), 0.0)
        rowdec[0, :, h * RET_DV:(h + 1) * RET_DV] = np.exp(lg[0, h] * (i + 1))[:, None]
        rowdec[1, :, h * RET_DV:(h + 1) * RET_DV] = np.exp(lg[1, h] * (CHUNK - i))[:, None]
        wend[0, h] = np.exp(lg[0, h] * (CHUNK - 1 - i))
        wend[1, h] = np.exp(lg[1, h] * i)
        cdec[:, h] = np.exp(lg[:, h] * CHUNK)[:, None]
    return [jnp.asarray(a, F32) for a in (dmat, rowdec, wend, cdec)]


def _rope_tables():
    t = np.arange(DEC_LEN)
    row = (t // GRID_W).astype(np.float32)
    col = (t % GRID_W).astype(np.float32)
    quarter = RET_DK // 4
    freqs = (ROPE_BASE ** (-np.arange(quarter, dtype=np.float32) / quarter)).astype(np.float32)
    ar = (row[:, None] * freqs[None, :]).astype(np.float64)
    ac = (col[:, None] * freqs[None, :]).astype(np.float64)
    cos = np.concatenate([np.cos(ar), np.cos(ar), np.cos(ac), np.cos(ac)], axis=1)
    sin = np.concatenate([-np.sin(ar), np.sin(ar), -np.sin(ac), np.sin(ac)], axis=1)
    return jnp.asarray(cos, F32), jnp.asarray(sin, F32)


def _ret_kernel(nc, zero_init, emit_state, rope, has_prev, *refs):
    q_ref, k_ref, v_ref, g_ref = refs[:4]
    pos = 4
    cos_ref = sin_ref = None
    if rope:
        cos_ref, sin_ref = refs[pos:pos + 2]
        pos += 2
    dmat_ref, rowdec_ref, wend_ref, cdec_ref, gn_ref = refs[pos:pos + 5]
    pos += 5
    s0_ref = None
    if not zero_init:
        s0_ref = refs[pos]
        pos += 1
    if has_prev:
        pos += 1
    y_ref = refs[pos]
    pos += 1
    st_ref = None
    if emit_state:
        st_ref = refs[pos]
        pos += 1
    state_scr, stash_scr = refs[pos:]

    t = pl.program_id(1)
    fwd = t >= nc
    step = jnp.where(fwd, t - nc, t)
    chunk = jnp.where(fwd, t - nc, nc - 1 - t)
    row0 = pl.multiple_of(chunk * CHUNK, CHUNK)

    @pl.when(step == 0)
    def _():
        if zero_init:
            state_scr[...] = jnp.zeros_like(state_scr)
        else:
            state_scr[...] = s0_ref[...]

    q = q_ref[...]
    k = k_ref[...]
    if rope:
        cos = jnp.concatenate([cos_ref[...]] * RET_HEADS, axis=1)
        sin = jnp.concatenate([sin_ref[...]] * RET_HEADS, axis=1)
        lane = lax.broadcasted_iota(jnp.int32, (CHUNK, D_RET), 1)
        first = (lane // (RET_DK // 4)) % 2 == 0

        def rot(x):
            partner = jnp.where(first, pltpu.roll(x, D_RET - RET_DK // 4, axis=1), pltpu.roll(x, RET_DK // 4, axis=1))
            return x * cos + partner * sin

        q = rot(q)
        k = rot(k)
    k = k * (RET_DK ** -0.5)
    qb = q.astype(BF16)
    vb = v_ref[...].astype(BF16)
    rowdec = rowdec_ref[...]
    y_parts = []
    for h in range(RET_HEADS):
        sl = slice(h * RET_DK, (h + 1) * RET_DK)
        kh = k[:, sl]
        sc = _dot_nt(qb[:, sl], kh.astype(BF16)) * dmat_ref[h]
        yh = _dot(sc.astype(BF16), vb[:, sl])
        yh = yh + _dot(qb[:, sl], state_scr[h].astype(BF16)) * rowdec[:, sl]
        y_parts.append(yh)
        kt = (kh.T * wend_ref[h:h + 1, :]).astype(BF16)
        state_scr[h] = state_scr[h] * cdec_ref[h:h + 1, :] + _dot(kt, vb[:, sl])
    y = jnp.concatenate(y_parts, axis=1)

    @pl.when(jnp.logical_not(fwd))
    def _():
        stash_scr[pl.ds(row0, CHUNK), :] = y

    @pl.when(fwd)
    def _():
        yt = y + stash_scr[pl.ds(row0, CHUNK), :]
        outs = []
        for h in range(RET_HEADS):
            yh = yt[:, h * RET_DV:(h + 1) * RET_DV]
            yc = yh - jnp.mean(yh, axis=-1, keepdims=True)
            outs.append(yc * lax.rsqrt(jnp.mean(yc * yc, axis=-1, keepdims=True) + EPS))
        yn = jnp.concatenate(outs, axis=1)
        y_ref[...] = (yn * gn_ref[...] * _silu(g_ref[...])).astype(y_ref.dtype)

    if emit_state:
        @pl.when(step == nc - 1)
        def _():
            for h in range(RET_HEADS):
                st_ref[h] = state_scr[h].T


def _ret(proj, gn_g, n_seq, nc, row_off, s0, rope, layer=0, prev_states=None):
    zero_init = s0 is None
    emit_state = zero_init
    direction, chunk, rows, out_rows = _seq_specs(nc, row_off)
    dmat, rowdec, wend, cdec = _ret_consts()

    def col(c):
        return pl.BlockSpec((CHUNK, D_RET), lambda s, t: (rows(s, t), c // D_RET))

    in_specs = [col(COL_Q), col(COL_K), col(COL_V), col(COL_G)]
    args = [proj, proj, proj, proj]
    if rope:
        cos, sin = _rope_tables()
        tab = pl.BlockSpec((CHUNK, RET_DK), lambda s, t: (chunk(t), 0))
        in_specs += [tab, tab]
        args += [cos, sin]
    in_specs += [
        pl.BlockSpec((None, RET_HEADS, CHUNK, CHUNK), lambda s, t: (direction(t), 0, 0, 0)),
        pl.BlockSpec((None, CHUNK, D_RET), lambda s, t: (direction(t), 0, 0)),
        pl.BlockSpec((None, 8, CHUNK), lambda s, t: (direction(t), 0, 0)),
        pl.BlockSpec((None, 8, CHUNK), lambda s, t: (direction(t), 0, 0)),
        pl.BlockSpec((1, D_RET), lambda s, t: (0, 0)),
    ]
    args += [dmat, rowdec, wend, cdec, gn_g.reshape(1, D_RET)]
    st_spec = pl.BlockSpec((None, None, RET_HEADS, RET_DK, RET_DV), lambda s, t: (s, direction(t), 0, 0, 0))
    if not zero_init:
        in_specs.append(st_spec)
        args.append(s0)
    out_specs = [pl.BlockSpec((CHUNK, D_RET), lambda s, t: (out_rows(s, t) - row_off, 0))]
    out_shape = [jax.ShapeDtypeStruct((n_seq * nc * CHUNK, D_RET), BF16)]
    aliases = {}
    if emit_state:
        out_specs.append(pl.BlockSpec((None, None, None, RET_HEADS, RET_DV, RET_DK),
                                      lambda s, t: (s, layer, direction(t), 0, 0, 0)))
        out_shape.append(jax.ShapeDtypeStruct((n_seq, DEPTH, 2, RET_HEADS, RET_DV, RET_DK), F32))
        if prev_states is not None:
            in_specs.append(pl.BlockSpec(memory_space=pl.ANY))
            args.append(prev_states)
            aliases = {len(args) - 1: 1}
    res = pl.pallas_call(
        functools.partial(_ret_kernel, nc, zero_init, emit_state, rope, prev_states is not None),
        grid=(n_seq, 2 * nc),
        in_specs=in_specs,
        out_specs=out_specs,
        out_shape=out_shape,
        scratch_shapes=[pltpu.VMEM((RET_HEADS, RET_DK, RET_DV), F32), pltpu.VMEM((nc * CHUNK, D_RET), F32)],
        input_output_aliases=aliases,
        compiler_params=_cparams(("arbitrary", "arbitrary")),
        name="ret_ctx" if zero_init else "ret_dec",
    )(*args)
    return res if emit_state else (res[0], None)


OUTPROJ_TM = 512


def _outproj_kernel(x_ref, ya_ref, yb_ref, yc_ref, w_ref, g_ref, o_ref):
    y = _dot(ya_ref[...], w_ref[:D_SSD, :])
    y = y + _dot(yb_ref[...], w_ref[D_SSD:D_SSD + D_S5, :])
    y = y + _dot(yc_ref[...], w_ref[D_SSD + D_S5:, :])
    o_ref[...] = x_ref[...] + g_ref[...] * y


def _outproj(x, y_ssd, y_s5, y_ret, w_out, mod3, layer):
    tm = OUTPROJ_TM
    return pl.pallas_call(
        _outproj_kernel,
        grid=(N_TOK // tm,),
        in_specs=[
            pl.BlockSpec((tm, D_MODEL), lambda i: (i, 0)),
            pl.BlockSpec((tm, D_SSD), lambda i: (i, 0)),
            pl.BlockSpec((tm, D_S5), lambda i: (i, 0)),
            pl.BlockSpec((tm, D_RET), lambda i: (i, 0)),
            pl.BlockSpec((None, D_MODEL, D_MODEL), lambda i: (layer, 0, 0)),
            _mod_spec(layer, 2, tm),
        ],
        out_specs=pl.BlockSpec((tm, D_MODEL), lambda i: (i, 0)),
        out_shape=jax.ShapeDtypeStruct((N_TOK, D_MODEL), F32),
        compiler_params=_cparams(("arbitrary",)),
        name="outproj",
    )(x, y_ssd, y_s5, y_ret, w_out, mod3)


PEER_SC_TM = 256
PEER_NCAND = PEER_TOPK + 1


def _cand_pairs():
    return [(i, j) for i in range(PEER_NCAND) for j in range(PEER_NCAND) if (i + 1) * (j + 1) <= PEER_NCAND]


def _sorting_network(n):
    pairs = []
    p = 1
    while p < n:
        k = p
        while k >= 1:
            for j in range(k % p, n - k, 2 * k):
                for i in range(min(k, n - j - k)):
                    if (i + j) // (2 * p) == (i + j + k) // (2 * p):
                        pairs.append((i + j, i + j + k))
            k //= 2
        p *= 2
    return pairs


def _top_rows_keys(s, n):
    nt = s.shape[0] // 8
    lists = [s[8 * k:8 * (k + 1), :] for k in range(nt)]
    for a, b in _sorting_network(nt):
        hi = jnp.maximum(lists[a], lists[b])
        lists[b] = jnp.minimum(lists[a], lists[b])
        lists[a] = hi
    lists.append(jnp.full_like(lists[0], NEG_INF))
    rows = []
    for r in range(n):
        m = jnp.max(lists[0], axis=0, keepdims=True)
        rows.append(m)
        if r < n - 1:
            taken = lists[0] >= m
            for k in range(min(n - 1 - r, nt)):
                lists[k] = jnp.where(taken, lists[k + 1], lists[k])
    return rows


def _peer_scores_kernel(x_ref, sh_ref, sc_ref, g_ref, wq_ref, keys_ref,
                        h_ref, thr_ref, g1_ref, s2_ref, e2_ref, cand_scr):
    tm = x_ref.shape[0]
    hb = _rms_modulate(x_ref[...], g_ref[...], sc_ref[...], sh_ref[...]).astype(BF16)
    h_ref[...] = hb
    q = _dot(hb, wq_ref[...]).astype(BF16)
    half = PEER_DQ // 2
    pairs = _cand_pairs()
    n_rows = cand_scr.shape[0]
    cand_scr[len(pairs):, :] = jnp.full((n_rows - len(pairs), tm), NEG_INF, F32)
    for h in range(PEER_HEADS):
        q1 = q[:, h * PEER_DQ:h * PEER_DQ + half]
        q2 = q[:, h * PEER_DQ + half:(h + 1) * PEER_DQ]
        s1 = _dot_nt(keys_ref[h, 0], q1)
        s2 = _dot_nt(keys_ref[h, 1], q2)
        a = _top_rows_keys(s1, PEER_NCAND)
        b = _top_rows_keys(s2, PEER_NCAND)
        for r, (i, j) in enumerate(pairs):
            cand_scr[r:r + 1, :] = a[i] + b[j]
        c = _top_rows_keys(cand_scr[...], PEER_NCAND)
        top = a[0] + b[0]
        zsum = jnp.zeros_like(top)
        for r in range(PEER_TOPK):
            zsum = zsum + jnp.exp(c[r] - top)
        tau = 0.5 * (c[PEER_TOPK - 1] + c[PEER_TOPK])
        thr_ref[h] = tau - s1
        g1_ref[h] = jnp.exp(s1 - a[0]) * (0.5 / zsum)
        s2_ref[h] = s2
        e2_ref[h] = jnp.exp(s2 - b[0])


def _peer_scores(x, mod3, layer, norm_g, wq, keys):
    tm = PEER_SC_TM
    n_tok = x.shape[0]
    sc_spec = pl.BlockSpec((PEER_HEADS, PEER_NKEYS, tm), lambda i: (0, 0, i))
    sc_shape = jax.ShapeDtypeStruct((PEER_HEADS, PEER_NKEYS, n_tok), F32)
    n_cand_rows = 8 * pl.next_power_of_2(-(-len(_cand_pairs()) // 8))
    return pl.pallas_call(
        _peer_scores_kernel,
        grid=(n_tok // tm,),
        in_specs=[
            pl.BlockSpec((tm, D_MODEL), lambda i: (i, 0)),
            _mod_spec(layer, 3, tm),
            _mod_spec(layer, 4, tm),
            pl.BlockSpec((1, D_MODEL), lambda i: (0, 0)),
            pl.BlockSpec((None, D_MODEL, PEER_HEADS * PEER_DQ), lambda i: (layer, 0, 0)),
            pl.BlockSpec((None, PEER_HEADS, 2, PEER_NKEYS, PEER_DQ // 2), lambda i: (layer, 0, 0, 0, 0)),
        ],
        out_specs=[pl.BlockSpec((tm, D_MODEL), lambda i: (i, 0)), sc_spec, sc_spec, sc_spec, sc_spec],
        out_shape=[jax.ShapeDtypeStruct((n_tok, D_MODEL), BF16), sc_shape, sc_shape, sc_shape, sc_shape],
        scratch_shapes=[pltpu.VMEM((n_cand_rows, tm), F32)],
        compiler_params=_cparams(("arbitrary",)),
        name="peer_scores",
    )(x, mod3, mod3, norm_g.reshape(1, D_MODEL), wq, keys)


PEER_TM = 512
PEER_EBLK = 1024
GELU_C = math.sqrt(2.0 / math.pi)

def _peer_experts_kernel(x_ref, g2_ref, h_ref, thr_ref, g1_ref, s2_ref, e2_ref, u_ref, vt_ref,
                         o_ref, acc_scr, act_scr, w_scr, gl_scr):
    tm = x_ref.shape[0]
    eblk = u_ref.shape[0]
    j = pl.program_id(1)

    @pl.when(j == 0)
    def _():
        acc_scr[...] = jnp.zeros_like(acc_scr)

    nb = eblk // PEER_NKEYS
    rsub = 16

    @pl.when(j < pl.num_programs(1))
    def _():
        act_scr[...] = _dot_nt(u_ref[...], h_ref[...])
        for lg in range(tm // 128):
            sl = slice(lg * 128, (lg + 1) * 128)
            for r0 in range(0, PEER_NKEYS, rsub):
                w = [jnp.zeros((rsub, 128), F32) for _ in range(nb)]
                for h in range(PEER_HEADS):
                    s2 = s2_ref[h, r0:r0 + rsub, sl]
                    e2 = e2_ref[h, r0:r0 + rsub, sl]
                    for a in range(nb):
                        w[a] = w[a] + jnp.where(s2 >= thr_ref[h, a:a + 1, sl], e2, 0.0) * g1_ref[h, a:a + 1, sl]
                for a in range(nb):
                    w_scr[a * PEER_NKEYS + r0:a * PEER_NKEYS + r0 + rsub, sl] = w[a]

    @pl.when(j + 1 < pl.num_programs(1) + 1)
    def _():
        act = act_scr[...]
        inner = act * (GELU_C + (GELU_C * 0.044715) * (act * act))
        gl_scr[...] = ((act * w_scr[...]) * (1.0 + jnp.tanh(inner))).astype(BF16)

    acc_scr[...] += lax.dot_general(vt_ref[...], gl_scr[...], (((0,), (0,)), ((), ())),
                                    preferred_element_type=F32)

    @pl.when(j == pl.num_programs(1) - 1)
    def _():
        o_ref[...] = x_ref[...] + g2_ref[...] * acc_scr[...].T


def _peer_experts(x, mod3, layer, hb, thr, g1, s2, e2, u_bf, vt_bf):
    tm, eblk = PEER_TM, PEER_EBLK
    n_tok = x.shape[0]
    once = pl.Buffered(1)
    sc_spec = pl.BlockSpec((PEER_HEADS, PEER_NKEYS, tm), lambda i, j: (0, 0, i))
    k1_spec = pl.BlockSpec((PEER_HEADS, eblk // PEER_NKEYS, tm), lambda i, j: (0, j, i))
    return pl.pallas_call(
        _peer_experts_kernel,
        grid=(n_tok // tm, PEER_EXPERTS // eblk),
        in_specs=[
            pl.BlockSpec((tm, D_MODEL), lambda i, j: (i, 0), pipeline_mode=once),
            _mod_spec(layer, 5, tm),
            pl.BlockSpec((tm, D_MODEL), lambda i, j: (i, 0)),
            k1_spec, k1_spec, sc_spec, sc_spec,
            pl.BlockSpec((None, eblk, D_MODEL), lambda i, j: (layer, j, 0)),
            pl.BlockSpec((None, eblk, D_MODEL), lambda i, j: (layer, j, 0)),
        ],
        out_specs=pl.BlockSpec((tm, D_MODEL), lambda i, j: (i, 0)),
        out_shape=jax.ShapeDtypeStruct((n_tok, D_MODEL), F32),
        scratch_shapes=[pltpu.VMEM((D_MODEL, tm), F32), pltpu.VMEM((eblk, tm), F32), pltpu.VMEM((eblk, tm), F32),
                        pltpu.VMEM((eblk, tm), BF16)],
        compiler_params=_cparams(("arbitrary", "arbitrary")),
        name="peer_experts",
    )(x, mod3, hb, thr, g1, s2, e2, u_bf, vt_bf)


FINAL_TM = 512


def _final_norm_kernel(n_ctx_tiles, x_ref, g_ref, oc_ref, od_ref):
    x = x_ref[...]
    var = jnp.mean(x * x, axis=-1, keepdims=True)
    y = x * lax.rsqrt(var + EPS) * g_ref[...]
    i = pl.program_id(0)

    @pl.when(i < n_ctx_tiles)
    def _():
        oc_ref[...] = y

    @pl.when(i >= n_ctx_tiles)
    def _():
        od_ref[...] = y


def _final_norm(x, g):
    tm = FINAL_TM
    n_ctx_tiles = N_CTX_TOK // tm
    return pl.pallas_call(
        functools.partial(_final_norm_kernel, n_ctx_tiles),
        grid=(N_TOK // tm,),
        in_specs=[pl.BlockSpec((tm, D_MODEL), lambda i: (i, 0)), pl.BlockSpec((1, D_MODEL), lambda i: (0, 0))],
        out_specs=[pl.BlockSpec((tm, D_MODEL), lambda i: (jnp.minimum(i, n_ctx_tiles - 1), 0)),
                   pl.BlockSpec((tm, D_MODEL), lambda i: (jnp.maximum(i - n_ctx_tiles, 0), 0))],
        out_shape=[jax.ShapeDtypeStruct((N_CTX_TOK, D_MODEL), F32),
                   jax.ShapeDtypeStruct((N_TOK - N_CTX_TOK, D_MODEL), F32)],
        compiler_params=_cparams(("arbitrary",)),
        name="final_norm",
    )(x, g.reshape(1, D_MODEL))


def _permute_w_in(w):
    cuts = np.cumsum([D_SSD, SSD_CONV_CH, 2 * SSD_HEADS, D_S5, D_RET, D_RET, D_RET])
    z, xbc, dt, u, rq, rk, rv, rg = jnp.split(w, [int(c) for c in cuts], axis=2)
    pad = jnp.zeros((DEPTH, D_MODEL, CHUNK - SSD_HEADS), w.dtype)
    tail = jnp.zeros((DEPTH, D_MODEL, PROJ_W - COL_DT - 2 * CHUNK), w.dtype)
    out = jnp.concatenate([z, rq, rk, rv, rg, xbc, u, dt[..., :SSD_HEADS], pad, dt[..., SSD_HEADS:], pad, tail], axis=2)
    return out.astype(BF16)


def _pad_lanes(a, width):
    return jnp.pad(a, [(0, 0)] * (a.ndim - 1) + [(0, width - a.shape[-1])])


def _ssd_params(conv_w, conv_b, dt_bias, a_log, d_skip, norm_g):
    return {
        "conv_w": jnp.pad(conv_w, ((0, 8 - SSD_CONV_K), (0, 0))),
        "conv_b": conv_b.reshape(1, SSD_CONV_CH),
        "dt_bias": _pad_lanes(dt_bias, CHUNK).reshape(2, 1, CHUNK),
        "a_exp": _pad_lanes(jnp.exp(a_log), CHUNK).reshape(2, 1, CHUNK),
        "d_skip": jnp.repeat(d_skip, SSD_P).reshape(1, D_SSD),
        "norm_g": norm_g.reshape(1, D_SSD),
    }


def _s5_params(layer, k_re, k_im, p_re, p_im, b_re, b_im, c_re, c_im, d_skip, glu_w, glu_b):
    kr = k_re.reshape(2, S5_GROUPS, S5_STATE, 1)
    ki = k_im.reshape(2, S5_GROUPS, S5_STATE, 1)
    bb_re = kr * b_re - ki * b_im
    bb_im = kr * b_im + ki * b_re
    eye = jnp.eye(S5_GROUPS, dtype=F32)

    def blockdiag_in(bb):
        return jnp.einsum("gh,dgnc->dgchn", eye, bb).reshape(2, D_S5, S5_LANES)

    def blockdiag_out(cc):
        return jnp.einsum("gh,dgcn->dgnhc", eye, cc).reshape(2, S5_LANES, D_S5)

    wb = jnp.concatenate([blockdiag_in(bb_re), blockdiag_in(bb_im)], axis=2).astype(BF16)
    wc = jnp.concatenate([blockdiag_out(c_re), blockdiag_out(-c_im)], axis=1).astype(BF16)
    return {"layer": layer, "wb": wb, "wc": wc, "p_re": p_re, "p_im": p_im,
            "d_skip": d_skip.reshape(1, D_S5), "glu_w": glu_w.astype(BF16), "glu_b": glu_b.reshape(1, D_S5)}


def _ssd_state_in(s):
    b = s.shape[0]
    s = s.reshape(b, 2, SSD_GROUPS, SSD_HPG, SSD_P, SSD_N)
    return jnp.transpose(s, (0, 1, 2, 5, 3, 4)).reshape(b, 2, SSD_GROUPS, SSD_N, SSD_HPG * SSD_P)


def kernel(x_prompt, x_sample, c, state_ssd, state_s5_re, state_s5_im, state_ret, c_ctx, ada_w, ada_b, norm1_g, norm2_g, w_in, w_out, ssd_conv_w, ssd_conv_b, ssd_dt_bias, ssd_a_log, ssd_d, ssd_norm_g, s5_lambda_re, s5_lambda_im, s5_log_dt, s5_b_re, s5_b_im, s5_c_re, s5_c_im, s5_d, s5_glu_w, s5_glu_b, ret_gn_g, peer_wq, peer_keys, peer_u, peer_v, final_norm_g):
    nc_ctx = CTX_LEN // CHUNK
    nc_dec = DEC_LEN // CHUNK
    dec_row_off = N_CTX_TOK // CHUNK

    cond = jnp.concatenate([c_ctx[None, :], c, jnp.zeros((N_COND - 1 - N_DEC_SEQ, D_MODEL), F32)], axis=0)
    mod3 = _adaln(cond, ada_w, ada_b).reshape(DEPTH * N_COND * N_MOD, 1, D_MODEL)
    p_re, p_im, k_re, k_im = _s5_prep(s5_lambda_re, s5_lambda_im, s5_log_dt)

    x = jnp.concatenate([x_prompt.reshape(N_CTX_TOK, D_MODEL), x_sample.reshape(N_TOK - N_CTX_TOK, D_MODEL)], axis=0)
    st_ssd = jnp.zeros((N_CTX_SEQ, DEPTH, 2, SSD_HEADS, SSD_P, SSD_N), F32)
    st_ret = jnp.zeros((N_CTX_SEQ, DEPTH, 2, RET_HEADS, RET_DV, RET_DK), F32)
    new_re, new_im = [], []
    w_in_b = _permute_w_in(w_in)
    w_out_b = w_out.astype(BF16)
    wq_b = peer_wq.astype(BF16)
    keys_b = peer_keys.astype(BF16)
    u_b = peer_u.astype(BF16)
    vt_b = peer_v.astype(BF16)
    for l in range(DEPTH):
        proj = _inproj(x, mod3, l, norm1_g[l], w_in_b)

        sp = _ssd_params(ssd_conv_w[l], ssd_conv_b[l], ssd_dt_bias[l], ssd_a_log[l], ssd_d[l], ssd_norm_g[l])
        y_ssd_c, st_ssd = _ssd(proj, sp, N_CTX_SEQ, nc_ctx, 0, None, layer=l, prev_states=st_ssd)
        y_ssd_d, _ = _ssd(proj, sp, N_DEC_SEQ, nc_dec, dec_row_off, _ssd_state_in(state_ssd[:, l]))

        s5p = _s5_params(l, k_re[2 * l:2 * l + 2], k_im[2 * l:2 * l + 2], p_re, p_im, s5_b_re[l], s5_b_im[l],
                         s5_c_re[l], s5_c_im[l], s5_d[l], s5_glu_w[l], s5_glu_b[l])
        y_s5_c, st_re, st_im = _s5(proj, s5p, N_CTX_SEQ, CTX_LEN // S5_ROWS, 0, None)
        x0 = (state_s5_re[:, l].reshape(N_DEC_SEQ, 2, 1, S5_LANES), state_s5_im[:, l].reshape(N_DEC_SEQ, 2, 1, S5_LANES))
        y_s5_d, _, _ = _s5(proj, s5p, N_DEC_SEQ, DEC_LEN // S5_ROWS, N_CTX_TOK // S5_ROWS, x0)

        y_ret_c, st_ret = _ret(proj, ret_gn_g[l], N_CTX_SEQ, nc_ctx, 0, None, rope=False, layer=l, prev_states=st_ret)
        y_ret_d, _ = _ret(proj, ret_gn_g[l], N_DEC_SEQ, nc_dec, dec_row_off,
                          jnp.swapaxes(state_ret[:, l], -1, -2), rope=True)

        x = _outproj(x, jnp.concatenate([y_ssd_c, y_ssd_d]), jnp.concatenate([y_s5_c, y_s5_d]),
                     jnp.concatenate([y_ret_c, y_ret_d]), w_out_b, mod3, l)

        hb, thr, g1, s2, e2 = _peer_scores(x, mod3, l, norm2_g[l], wq_b, keys_b)
        x = _peer_experts(x, mod3, l, hb, thr, g1, s2, e2, u_b, vt_b)

        new_re.append(st_re.reshape(N_CTX_SEQ, 2, S5_GROUPS, S5_STATE))
        new_im.append(st_im.reshape(N_CTX_SEQ, 2, S5_GROUPS, S5_STATE))

    y_ctx, y_dec = _final_norm(x, final_norm_g)
    y_prompt = y_ctx.reshape(N_CTX_SEQ, CTX_LEN, D_MODEL)
    y_sample = y_dec.reshape(N_DEC_SEQ, DEC_LEN, D_MODEL)
    return (y_prompt, y_sample, st_ssd, jnp.stack(new_re, axis=1), jnp.stack(new_im, axis=1), st_ret)
```

```python
import functools
import math

import jax
import jax.numpy as jnp
import numpy as np
from jax import lax
from jax.experimental import pallas as pl
from jax.experimental.pallas import tpu as pltpu

F32 = jnp.float32
BF16 = jnp.bfloat16

D_MODEL = 2048
N_CTX_SEQ = 16
CTX_LEN = 256
N_DEC_SEQ = 2
DEC_LEN = 2048
N_CTX_TOK = N_CTX_SEQ * CTX_LEN
N_TOK = N_CTX_TOK + N_DEC_SEQ * DEC_LEN
DEPTH = 2
GRID_W = 64
CHUNK = 128

SSD_HEADS = 12
SSD_P = 64
D_SSD = SSD_HEADS * SSD_P
SSD_N = 128
SSD_GROUPS = 2
SSD_HPG = SSD_HEADS // SSD_GROUPS
SSD_CONV_K = 5
SSD_CONV_CH = D_SSD + 2 * SSD_GROUPS * SSD_N
S5_CH = 16
S5_GROUPS = 32
D_S5 = S5_CH * S5_GROUPS
S5_STATE = 64
S5_LANES = S5_GROUPS * S5_STATE
S5_TILE = 8
S5_ROWS = 256
RET_HEADS = 6
RET_DK = 128
RET_DV = 128
D_RET = RET_HEADS * RET_DV
ROPE_BASE = 10000.0
PEER_HEADS = 8
PEER_DQ = 256
PEER_NKEYS = 128
PEER_EXPERTS = PEER_NKEYS * PEER_NKEYS
PEER_TOPK = 16
N_MOD = 6
N_COND = 8
EPS = 1e-6

COL_Z, COL_Q, COL_K, COL_V, COL_G = 0, 768, 1536, 2304, 3072
COL_XBC = 3840
COL_U = 5120
COL_DT = 5632
PROJ_W = COL_DT + 2 * CHUNK

VMEM_LIMIT = 56 * 1024 * 1024

NEG_INF = float("-inf")


def _cparams(sem, vmem_limit=VMEM_LIMIT):
    return pltpu.CompilerParams(dimension_semantics=sem, vmem_limit_bytes=vmem_limit)


def _split3(a):
    hi = a.astype(BF16)
    r1 = a - hi.astype(F32)
    mid = r1.astype(BF16)
    lo = (r1 - mid.astype(F32)).astype(BF16)
    return hi, mid, lo


def _dot(a, b):
    return jnp.dot(a, b, preferred_element_type=F32)


def _dot_split_lhs(a, b_exact):
    hi, mid, lo = _split3(a)
    return _dot(hi, b_exact) + _dot(mid, b_exact) + _dot(lo, b_exact)


def _dot_split_rhs(a_exact, b):
    hi, mid, lo = _split3(b)
    return _dot(a_exact, hi) + _dot(a_exact, mid) + _dot(a_exact, lo)


def _dot_nt(a, b):
    return lax.dot_general(a, b, (((1,), (1,)), ((), ())), preferred_element_type=F32)


def _silu(x):
    return x * jax.nn.sigmoid(x)


def _softplus(x):
    return jnp.maximum(x, 0.0) + jnp.log1p(jnp.exp(-jnp.abs(x)))


ADA_TN = 1536


def _adaln_kernel(c_ref, w_ref, b_ref, o_ref):
    s = _silu(c_ref[...]).astype(BF16)
    o_ref[...] = _dot(s, w_ref[...].astype(BF16)) + b_ref[...]


def _adaln(cond, ada_w, ada_b):
    n_out = N_MOD * D_MODEL
    return pl.pallas_call(
        _adaln_kernel,
        grid=(DEPTH, n_out // ADA_TN),
        in_specs=[
            pl.BlockSpec((N_COND, D_MODEL), lambda l, j: (0, 0)),
            pl.BlockSpec((None, D_MODEL, ADA_TN), lambda l, j: (l, 0, j)),
            pl.BlockSpec((None, 1, ADA_TN), lambda l, j: (l, 0, j)),
        ],
        out_specs=pl.BlockSpec((None, N_COND, ADA_TN), lambda l, j: (l, 0, j)),
        out_shape=jax.ShapeDtypeStruct((DEPTH, N_COND, n_out), F32),
        compiler_params=_cparams(("arbitrary", "arbitrary")),
        name="adaln",
    )(cond, ada_w, ada_b.reshape(DEPTH, 1, n_out))


def _mod_spec(layer, which, tm):
    n_ctx_tiles = N_CTX_TOK // tm
    tiles_per_dec = DEC_LEN // tm

    def index(i, *_):
        cond = jnp.where(i < n_ctx_tiles, 0, 1 + (i - n_ctx_tiles) // tiles_per_dec)
        return ((layer * N_COND + cond) * N_MOD + which, 0, 0)

    return pl.BlockSpec((None, 1, D_MODEL), index)


def _rms_modulate(x, g, sc, sh):
    var = jnp.mean(x * x, axis=-1, keepdims=True)
    y = x * lax.rsqrt(var + EPS) * g
    return y * (1.0 + sc) + sh


INPROJ_TM = 256


def _inproj_kernel(x_ref, sh_ref, sc_ref, g_ref, w_ref, o_ref):
    h = _rms_modulate(x_ref[...], g_ref[...], sc_ref[...], sh_ref[...]).astype(BF16)
    o_ref[...] = _dot(h, w_ref[...])


def _inproj(x, mod3, layer, norm_g, w_perm):
    tm = INPROJ_TM
    return pl.pallas_call(
        _inproj_kernel,
        grid=(N_TOK // tm,),
        in_specs=[
            pl.BlockSpec((tm, D_MODEL), lambda i: (i, 0)),
            _mod_spec(layer, 0, tm),
            _mod_spec(layer, 1, tm),
            pl.BlockSpec((1, D_MODEL), lambda i: (0, 0)),
            pl.BlockSpec((None, D_MODEL, PROJ_W), lambda i: (layer, 0, 0), pipeline_mode=pl.Buffered(1)),
        ],
        out_specs=pl.BlockSpec((tm, PROJ_W), lambda i: (i, 0)),
        out_shape=jax.ShapeDtypeStruct((N_TOK, PROJ_W), F32),
        compiler_params=_cparams(("arbitrary",)),
        name="inproj",
    )(x, mod3, mod3, norm_g.reshape(1, D_MODEL), w_perm)


def _walk(nc):
    def direction(t):
        return jnp.where(t < nc, 1, 0)

    def chunk(t):
        return jnp.where(t < nc, nc - 1 - t, t - nc)

    return direction, chunk


def _seq_specs(nc, row_off):
    direction, chunk = _walk(nc)

    def rows(s, t):
        return row_off + s * nc + chunk(t)

    def out_rows(s, t):
        return row_off + s * nc + jnp.where(t < nc, 0, t - nc)

    return direction, chunk, rows, out_rows


def _ssd_consts():
    idx = np.arange(CHUNK)
    tri = np.stack([(idx[None, :] <= idx[:, None]), (idx[None, :] >= idx[:, None])]).astype(np.float32)
    e_p = np.zeros((CHUNK, D_SSD), np.float32)
    e_n = np.zeros((CHUNK, SSD_HEADS * CHUNK), np.float32)
    for h in range(SSD_HEADS):
        e_p[h, h * SSD_P:(h + 1) * SSD_P] = 1.0
        e_n[h, h * CHUNK:(h + 1) * CHUNK] = 1.0
    lane = np.arange(CHUNK)
    pair = np.concatenate([np.broadcast_to(lane < SSD_P, (CHUNK, CHUNK)),
                           np.broadcast_to(lane >= SSD_P, (CHUNK, CHUNK))]).astype(np.float32)
    return tri, e_p, e_n, pair


def _ssd_kernel(nc, zero_init, emit_state, has_prev, *refs):
    (z_ref, xc_ref, xp_ref, xn_ref, dt_ref, cw_ref, cb_ref, dtb_ref, aexp_ref, dsk_ref, ng_ref,
     tri_ref, ep_ref, en_ref, pair_ref) = refs[:15]
    pos = 15
    s0_ref = None
    if not zero_init:
        s0_ref = refs[pos]
        pos += 1
    if has_prev:
        pos += 1
    y_ref = refs[pos]
    pos += 1
    st_ref = None
    if emit_state:
        st_ref = refs[pos]
        pos += 1
    state_scr, stash_scr = refs[pos:]

    t = pl.program_id(1)
    fwd = t >= nc
    step = jnp.where(fwd, t - nc, t)
    chunk = jnp.where(fwd, t - nc, nc - 1 - t)
    row0 = pl.multiple_of(chunk * CHUNK, CHUNK)

    @pl.when(step == 0)
    def _():
        if zero_init:
            state_scr[...] = jnp.zeros_like(state_scr)
        else:
            state_scr[...] = s0_ref[...]

    prev = jnp.where(chunk > 0, xp_ref[...], 0.0)
    nxt = jnp.where(chunk < nc - 1, xn_ref[...], 0.0)
    ext = jnp.concatenate([prev, xc_ref[...], nxt], axis=0)
    conv = cb_ref[...]
    for k in range(SSD_CONV_K):
        off = 8 + k - SSD_CONV_K // 2
        conv = conv + cw_ref[k:k + 1, :] * ext[off:off + CHUNK, :]
    xbc = _silu(conv)
    x = xbc[:, :D_SSD]
    bm = xbc[:, D_SSD:D_SSD + SSD_GROUPS * SSD_N]
    cm = xbc[:, D_SSD + SSD_GROUPS * SSD_N:]

    dt = _softplus(dt_ref[...] + dtb_ref[...])
    la = -dt * aexp_ref[...]
    tri = tri_ref[...]
    cum = _dot_split_rhs(tri.astype(BF16), la)
    ep = ep_ref[...]
    cum_p = _dot_split_lhs(cum, ep)
    dt_p = _dot_split_lhs(dt, ep)
    cum_col = _dot_split_lhs(cum, en_ref[...])
    cum_t = cum.T
    tot_p = jnp.where(fwd, cum_p[CHUNK - 1:CHUNK, :], cum_p[0:1, :])

    v = x * dt_p
    vb = v.astype(BF16)
    mask = tri > 0.5
    pair = pair_ref[...]
    y_parts = []
    for g in range(SSD_GROUPS):
        cg = cm[:, g * SSD_N:(g + 1) * SSD_N].astype(BF16)
        bg = bm[:, g * SSD_N:(g + 1) * SSD_N]
        gmat = _dot_nt(cg, bg.astype(BF16))
        for hp in range(SSD_HPG // 2):
            scs = []
            for h in (g * SSD_HPG + 2 * hp, g * SSD_HPG + 2 * hp + 1):
                ci = cum_col[:, h * CHUNK:(h + 1) * CHUNK]
                cj = cum_t[h:h + 1, :]
                dec = jnp.exp(jnp.where(mask, ci - cj, NEG_INF))
                scs.append((gmat * dec).astype(BF16))
            c0 = (g * SSD_HPG + 2 * hp) * SSD_P
            v2 = vb[:, c0:c0 + 2 * SSD_P]
            vv = jnp.concatenate([v2, v2], axis=0) * pair
            y_parts.append(_dot(jnp.concatenate(scs, axis=1), vv))
    y = jnp.concatenate(y_parts, axis=1)

    w_p = SSD_HPG * SSD_P
    y_off = jnp.concatenate(
        [_dot(cm[:, g * SSD_N:(g + 1) * SSD_N].astype(BF16), state_scr[g].astype(BF16))
         for g in range(SSD_GROUPS)], axis=1)
    y = y + y_off * jnp.exp(cum_p)

    vw = (v * jnp.exp(tot_p - cum_p)).astype(BF16)
    cdec = jnp.exp(tot_p)
    for g in range(SSD_GROUPS):
        bt = bm[:, g * SSD_N:(g + 1) * SSD_N].T.astype(BF16)
        state_scr[g] = state_scr[g] * cdec[:, g * w_p:(g + 1) * w_p] + _dot(bt, vw[:, g * w_p:(g + 1) * w_p])

    @pl.when(jnp.logical_not(fwd))
    def _():
        stash_scr[pl.ds(row0, CHUNK), :] = y

    @pl.when(fwd)
    def _():
        ytot = y + stash_scr[pl.ds(row0, CHUNK), :] + x * dsk_ref[...]
        gated = ytot * _silu(z_ref[...])
        var = jnp.mean(gated * gated, axis=-1, keepdims=True)
        y_ref[...] = (gated * lax.rsqrt(var + EPS) * ng_ref[...]).astype(y_ref.dtype)

    if emit_state:
        @pl.when(step == nc - 1)
        def _():
            for g in range(SSD_GROUPS):
                st_t = state_scr[g].T
                for k in range(SSD_HPG):
                    st_ref[g * SSD_HPG + k] = st_t[k * SSD_P:(k + 1) * SSD_P, :]


def _ssd(proj, p, n_seq, nc, row_off, s0, layer=0, prev_states=None):
    zero_init = s0 is None
    emit_state = zero_init
    direction, chunk, rows, out_rows = _seq_specs(nc, row_off)
    tri, e_p, e_n, pair = _ssd_consts()
    n8 = N_TOK // 8
    w_st = SSD_HPG * SSD_P
    in_specs = [
        pl.BlockSpec((CHUNK, D_SSD), lambda s, t: (rows(s, t), COL_Z // D_SSD)),
        pl.BlockSpec((CHUNK, SSD_CONV_CH), lambda s, t: (rows(s, t), COL_XBC // SSD_CONV_CH)),
        pl.BlockSpec((8, SSD_CONV_CH),
                     lambda s, t: (jnp.maximum(rows(s, t) * (CHUNK // 8) - 1, 0), COL_XBC // SSD_CONV_CH)),
        pl.BlockSpec((8, SSD_CONV_CH),
                     lambda s, t: (jnp.minimum((rows(s, t) + 1) * (CHUNK // 8), n8 - 1), COL_XBC // SSD_CONV_CH)),
        pl.BlockSpec((CHUNK, CHUNK), lambda s, t: (rows(s, t), COL_DT // CHUNK + direction(t))),
        pl.BlockSpec((8, SSD_CONV_CH), lambda s, t: (0, 0)),
        pl.BlockSpec((1, SSD_CONV_CH), lambda s, t: (0, 0)),
        pl.BlockSpec((None, 1, CHUNK), lambda s, t: (direction(t), 0, 0)),
        pl.BlockSpec((None, 1, CHUNK), lambda s, t: (direction(t), 0, 0)),
        pl.BlockSpec((1, D_SSD), lambda s, t: (0, 0)),
        pl.BlockSpec((1, D_SSD), lambda s, t: (0, 0)),
        pl.BlockSpec((None, CHUNK, CHUNK), lambda s, t: (direction(t), 0, 0)),
        pl.BlockSpec((CHUNK, D_SSD), lambda s, t: (0, 0)),
        pl.BlockSpec((CHUNK, SSD_HEADS * CHUNK), lambda s, t: (0, 0)),
        pl.BlockSpec((2 * CHUNK, CHUNK), lambda s, t: (0, 0)),
    ]
    args = [proj, proj, proj, proj, proj, p["conv_w"], p["conv_b"], p["dt_bias"], p["a_exp"], p["d_skip"],
            p["norm_g"], jnp.asarray(tri), jnp.asarray(e_p, BF16), jnp.asarray(e_n, BF16), jnp.asarray(pair, BF16)]
    st_spec = pl.BlockSpec((None, None, SSD_GROUPS, SSD_N, w_st), lambda s, t: (s, direction(t), 0, 0, 0))
    if not zero_init:
        in_specs.append(st_spec)
        args.append(s0)
    out_specs = [pl.BlockSpec((CHUNK, D_SSD), lambda s, t: (out_rows(s, t) - row_off, 0))]
    out_shape = [jax.ShapeDtypeStruct((n_seq * nc * CHUNK, D_SSD), BF16)]
    aliases = {}
    if emit_state:
        out_specs.append(pl.BlockSpec((None, None, None, SSD_HEADS, SSD_P, SSD_N),
                                      lambda s, t: (s, layer, direction(t), 0, 0, 0)))
        out_shape.append(jax.ShapeDtypeStruct((n_seq, DEPTH, 2, SSD_HEADS, SSD_P, SSD_N), F32))
        if prev_states is not None:
            in_specs.append(pl.BlockSpec(memory_space=pl.ANY))
            args.append(prev_states)
            aliases = {len(args) - 1: 1}
    res = pl.pallas_call(
        functools.partial(_ssd_kernel, nc, zero_init, emit_state, prev_states is not None),
        grid=(n_seq, 2 * nc),
        in_specs=in_specs,
        out_specs=out_specs,
        out_shape=out_shape,
        scratch_shapes=[pltpu.VMEM((SSD_GROUPS, SSD_N, w_st), F32), pltpu.VMEM((nc * CHUNK, D_SSD), F32)],
        input_output_aliases=aliases,
        compiler_params=_cparams(("arbitrary", "arbitrary")),
        name="ssd_ctx" if zero_init else "ssd_dec",
    )(*args)
    return res if emit_state else (res[0], None)


def _s5_prep_kernel(lr_ref, li_ref, ldt_ref, pre_ref, pim_ref, kre_ref, kim_ref):
    lr = lr_ref[...]
    li = li_ref[...]
    dt = jnp.exp(ldt_ref[...])
    kk = (lax.broadcasted_iota(jnp.int32, (S5_TILE, S5_LANES), 0) + 1).astype(F32)
    mag = jnp.exp(kk * (lr * dt))
    ang = kk * (li * dt)
    p_re = mag * jnp.cos(ang)
    p_im = mag * jnp.sin(ang)
    pre_ref[...] = p_re
    pim_ref[...] = p_im
    a_re = p_re[0:1, :]
    a_im = p_im[0:1, :]
    den = lr * lr + li * li
    num_re = a_re - 1.0
    kre_ref[...] = (num_re * lr + a_im * li) / den
    kim_ref[...] = (a_im * lr - num_re * li) / den


def _s5_prep(lam_re, lam_im, log_dt):
    n = DEPTH * 2
    lr = lam_re.reshape(n, 1, S5_LANES)
    li = lam_im.reshape(n, 1, S5_LANES)
    ldt = jnp.broadcast_to(log_dt[..., None], (DEPTH, 2, S5_GROUPS, S5_STATE)).reshape(n, 1, S5_LANES)
    row = pl.BlockSpec((None, 1, S5_LANES), lambda i: (i, 0, 0))
    tab = pl.BlockSpec((None, S5_TILE, S5_LANES), lambda i: (i, 0, 0))
    return pl.pallas_call(
        _s5_prep_kernel,
        grid=(n,),
        in_specs=[row, row, row],
        out_specs=[tab, tab, row, row],
        out_shape=[jax.ShapeDtypeStruct((n, S5_TILE, S5_LANES), F32)] * 2
        + [jax.ShapeDtypeStruct((n, 1, S5_LANES), F32)] * 2,
        compiler_params=_cparams(("arbitrary",)),
        name="s5_prep",
    )(lr, li, ldt)


def _s5_kernel(nc, zero_init, emit_state, *refs):
    (u_ref, jm_ref, wb_ref, pre_ref, pim_ref, wc_ref, dsk_ref, gw_ref, gb_ref) = refs[:9]
    pos = 9
    x0re_ref = x0im_ref = None
    if not zero_init:
        x0re_ref, x0im_ref = refs[pos:pos + 2]
        pos += 2
    y_ref = refs[pos]
    pos += 1
    sre_ref = sim_ref = None
    if emit_state:
        sre_ref, sim_ref = refs[pos:pos + 2]
        pos += 2
    cre_scr, cim_scr, stash_scr = refs[pos:]

    t = pl.program_id(1)
    fwd = t >= nc
    step = jnp.where(fwd, t - nc, t)
    chunk = jnp.where(fwd, t - nc, nc - 1 - t)
    row0 = pl.multiple_of(chunk * S5_ROWS, S5_ROWS)

    @pl.when(step == 0)
    def _():
        if zero_init:
            cre_scr[...] = jnp.zeros_like(cre_scr)
            cim_scr[...] = jnp.zeros_like(cim_scr)
        else:
            cre_scr[...] = x0re_ref[...]
            cim_scr[...] = x0im_ref[...]

    u = u_ref[...]
    jm = jm_ref[...]
    us = _dot(jm, u.astype(BF16)).astype(BF16)
    bu = _dot(us, wb_ref[...])
    xr = bu[:, :S5_LANES]
    xi = bu[:, S5_LANES:]
    cr = cre_scr[...]
    ci = cim_scr[...]
    pr = pre_ref[...]
    pi = pim_ref[...]
    sub = lax.broadcasted_iota(jnp.int32, (S5_TILE, S5_LANES), 0)
    steps = []
    d = 1
    while d < S5_TILE:
        keep = sub >= d
        steps.append((d, jnp.where(keep, pre_ref[d - 1:d, :], 0.0), jnp.where(keep, pim_ref[d - 1:d, :], 0.0)))
        d *= 2
    tiles_r, tiles_i = [], []
    for b in range(S5_ROWS // S5_TILE):
        br = xr[b * S5_TILE:(b + 1) * S5_TILE, :]
        bi = xi[b * S5_TILE:(b + 1) * S5_TILE, :]
        for d, ar, ai in steps:
            sr = pltpu.roll(br, d, axis=0)
            si = pltpu.roll(bi, d, axis=0)
            br, bi = br + (ar * sr - ai * si), bi + (ar * si + ai * sr)
        br, bi = br + (pr * cr - pi * ci), bi + (pr * ci + pi * cr)
        cr = br[S5_TILE - 1:S5_TILE, :]
        ci = bi[S5_TILE - 1:S5_TILE, :]
        tiles_r.append(br)
        tiles_i.append(bi)
    xr = jnp.concatenate(tiles_r, axis=0)
    xi = jnp.concatenate(tiles_i, axis=0)
    cre_scr[...] = cr
    cim_scr[...] = ci

    ys = _dot(xr.astype(BF16), wc_ref[:S5_LANES, :]) + _dot(xi.astype(BF16), wc_ref[S5_LANES:, :])
    yh = ys.astype(BF16)
    yl = (ys - yh.astype(F32)).astype(BF16)
    y = _dot(jm, yh) + _dot(jm, yl)

    @pl.when(jnp.logical_not(fwd))
    def _():
        stash_scr[pl.ds(row0, S5_ROWS), :] = y

    @pl.when(fwd)
    def _():
        yt = y + stash_scr[pl.ds(row0, S5_ROWS), :] + u * dsk_ref[...]
        yt = jax.nn.gelu(yt)
        gate = jax.nn.sigmoid(_dot(yt.astype(BF16), gw_ref[...]) + gb_ref[...])
        y_ref[...] = (yt * gate).astype(y_ref.dtype)

    if emit_state:
        @pl.when(step == nc - 1)
        def _():
            sre_ref[...] = cre_scr[...]
            sim_ref[...] = cim_scr[...]


def _s5(proj, p, n_seq, nc, row_off, x0):
    zero_init = x0 is None
    emit_state = zero_init
    direction, chunk, rows, out_rows = _seq_specs(nc, row_off)
    eye = np.eye(S5_ROWS, dtype=np.float32)
    jm = jnp.asarray(np.stack([eye, eye[::-1]]), BF16)
    lyr = p["layer"]

    def dsel(t):
        return lyr * 2 + direction(t)

    in_specs = [
        pl.BlockSpec((S5_ROWS, D_S5), lambda s, t: (rows(s, t), COL_U // D_S5)),
        pl.BlockSpec((None, S5_ROWS, S5_ROWS), lambda s, t: (direction(t), 0, 0)),
        pl.BlockSpec((None, D_S5, 2 * S5_LANES), lambda s, t: (direction(t), 0, 0)),
        pl.BlockSpec((None, S5_TILE, S5_LANES), lambda s, t: (dsel(t), 0, 0)),
        pl.BlockSpec((None, S5_TILE, S5_LANES), lambda s, t: (dsel(t), 0, 0)),
        pl.BlockSpec((None, 2 * S5_LANES, D_S5), lambda s, t: (direction(t), 0, 0)),
        pl.BlockSpec((1, D_S5), lambda s, t: (0, 0)),
        pl.BlockSpec((D_S5, D_S5), lambda s, t: (0, 0)),
        pl.BlockSpec((1, D_S5), lambda s, t: (0, 0)),
    ]
    args = [proj, jm, p["wb"], p["p_re"], p["p_im"], p["wc"], p["d_skip"], p["glu_w"], p["glu_b"]]
    st_spec = pl.BlockSpec((None, None, 1, S5_LANES), lambda s, t: (s, direction(t), 0, 0))
    if not zero_init:
        in_specs += [st_spec, st_spec]
        args += [x0[0], x0[1]]
    out_specs = [pl.BlockSpec((S5_ROWS, D_S5), lambda s, t: (out_rows(s, t) - row_off, 0))]
    out_shape = [jax.ShapeDtypeStruct((n_seq * nc * S5_ROWS, D_S5), BF16)]
    if emit_state:
        out_specs += [st_spec, st_spec]
        out_shape += [jax.ShapeDtypeStruct((n_seq, 2, 1, S5_LANES), F32)] * 2
    res = pl.pallas_call(
        functools.partial(_s5_kernel, nc, zero_init, emit_state),
        grid=(n_seq, 2 * nc),
        in_specs=in_specs,
        out_specs=out_specs,
        out_shape=out_shape,
        scratch_shapes=[pltpu.VMEM((1, S5_LANES), F32), pltpu.VMEM((1, S5_LANES), F32),
                        pltpu.VMEM((nc * S5_ROWS, D_S5), F32)],
        compiler_params=_cparams(("arbitrary", "arbitrary")),
        name="s5_ctx" if zero_init else "s5_dec",
    )(*args)
    return res if emit_state else (res[0], None, None)


def _ret_consts():
    heads = np.arange(RET_HEADS, dtype=np.float64)
    lg = np.stack([np.log1p(-np.exp2(-5.0 - heads)), np.log1p(-np.exp2(-5.5 - heads))])
    i = np.arange(CHUNK, dtype=np.float64)
    diff = i[:, None] - i[None, :]
    dmat = np.zeros((2, RET_HEADS, CHUNK, CHUNK))
    rowdec = np.zeros((2, CHUNK, D_RET))
    wend = np.zeros((2, 8, CHUNK))
    cdec = np.zeros((2, 8, CHUNK))
    for h in range(RET_HEADS):
        dmat[0, h] = np.where(diff >= 0, np.exp(lg[0, h] * diff), 0.0)
        dmat[1, h] = np.where(diff <= 0, np.exp(-lg[1, h] * diff), 0.0)
        rowdec[0, :, h * RET_DV:(h + 1) * RET_DV] = np.exp(lg[0, h] * (i + 1))[:, None]
        rowdec[1, :, h * RET_DV:(h + 1) * RET_DV] = np.exp(lg[1, h] * (CHUNK - i))[:, None]
        wend[0, h] = np.exp(lg[0, h] * (CHUNK - 1 - i))
        wend[1, h] = np.exp(lg[1, h] * i)
        cdec[:, h] = np.exp(lg[:, h] * CHUNK)[:, None]
    return [jnp.asarray(a, F32) for a in (dmat, rowdec, wend, cdec)]


def _rope_tables():
    t = np.arange(DEC_LEN)
    row = (t // GRID_W).astype(np.float32)
    col = (t % GRID_W).astype(np.float32)
    quarter = RET_DK // 4
    freqs = (ROPE_BASE ** (-np.arange(quarter, dtype=np.float32) / quarter)).astype(np.float32)
    ar = (row[:, None] * freqs[None, :]).astype(np.float64)
    ac = (col[:, None] * freqs[None, :]).astype(np.float64)
    cos = np.concatenate([np.cos(ar), np.cos(ar), np.cos(ac), np.cos(ac)], axis=1)
    sin = np.concatenate([-np.sin(ar), np.sin(ar), -np.sin(ac), np.sin(ac)], axis=1)
    return jnp.asarray(cos, F32), jnp.asarray(sin, F32)


def _ret_kernel(nc, zero_init, emit_state, rope, has_prev, *refs):
    q_ref, k_ref, v_ref, g_ref = refs[:4]
    pos = 4
    cos_ref = sin_ref = None
    if rope:
        cos_ref, sin_ref = refs[pos:pos + 2]
        pos += 2
    dmat_ref, rowdec_ref, wend_ref, cdec_ref, gn_ref = refs[pos:pos + 5]
    pos += 5
    s0_ref = None
    if not zero_init:
        s0_ref = refs[pos]
        pos += 1
    if has_prev:
        pos += 1
    y_ref = refs[pos]
    pos += 1
    st_ref = None
    if emit_state:
        st_ref = refs[pos]
        pos += 1
    state_scr, stash_scr = refs[pos:]

    t = pl.program_id(1)
    fwd = t >= nc
    step = jnp.where(fwd, t - nc, t)
    chunk = jnp.where(fwd, t - nc, nc - 1 - t)
    row0 = pl.multiple_of(chunk * CHUNK, CHUNK)

    @pl.when(step == 0)
    def _():
        if zero_init:
            state_scr[...] = jnp.zeros_like(state_scr)
        else:
            state_scr[...] = s0_ref[...]

    q = q_ref[...]
    k = k_ref[...]
    if rope:
        cos = jnp.concatenate([cos_ref[...]] * RET_HEADS, axis=1)
        sin = jnp.concatenate([sin_ref[...]] * RET_HEADS, axis=1)
        lane = lax.broadcasted_iota(jnp.int32, (CHUNK, D_RET), 1)
        first = (lane // (RET_DK // 4)) % 2 == 0

        def rot(x):
            partner = jnp.where(first, pltpu.roll(x, D_RET - RET_DK // 4, axis=1), pltpu.roll(x, RET_DK // 4, axis=1))
            return x * cos + partner * sin

        q = rot(q)
        k = rot(k)
    k = k * (RET_DK ** -0.5)
    qb = q.astype(BF16)
    vb = v_ref[...].astype(BF16)
    rowdec = rowdec_ref[...]
    y_parts = []
    for h in range(RET_HEADS):
        sl = slice(h * RET_DK, (h + 1) * RET_DK)
        kh = k[:, sl]
        sc = _dot_nt(qb[:, sl], kh.astype(BF16)) * dmat_ref[h]
        yh = _dot(sc.astype(BF16), vb[:, sl])
        yh = yh + _dot(qb[:, sl], state_scr[h].astype(BF16)) * rowdec[:, sl]
        y_parts.append(yh)
        kt = (kh.T * wend_ref[h:h + 1, :]).astype(BF16)
        state_scr[h] = state_scr[h] * cdec_ref[h:h + 1, :] + _dot(kt, vb[:, sl])
    y = jnp.concatenate(y_parts, axis=1)

    @pl.when(jnp.logical_not(fwd))
    def _():
        stash_scr[pl.ds(row0, CHUNK), :] = y

    @pl.when(fwd)
    def _():
        yt = y + stash_scr[pl.ds(row0, CHUNK), :]
        outs = []
        for h in range(RET_HEADS):
            yh = yt[:, h * RET_DV:(h + 1) * RET_DV]
            yc = yh - jnp.mean(yh, axis=-1, keepdims=True)
            outs.append(yc * lax.rsqrt(jnp.mean(yc * yc, axis=-1, keepdims=True) + EPS))
        yn = jnp.concatenate(outs, axis=1)
        y_ref[...] = (yn * gn_ref[...] * _silu(g_ref[...])).astype(y_ref.dtype)

    if emit_state:
        @pl.when(step == nc - 1)
        def _():
            for h in range(RET_HEADS):
                st_ref[h] = state_scr[h].T


def _ret(proj, gn_g, n_seq, nc, row_off, s0, rope, layer=0, prev_states=None):
    zero_init = s0 is None
    emit_state = zero_init
    direction, chunk, rows, out_rows = _seq_specs(nc, row_off)
    dmat, rowdec, wend, cdec = _ret_consts()

    def col(c):
        return pl.BlockSpec((CHUNK, D_RET), lambda s, t: (rows(s, t), c // D_RET))

    in_specs = [col(COL_Q), col(COL_K), col(COL_V), col(COL_G)]
    args = [proj, proj, proj, proj]
    if rope:
        cos, sin = _rope_tables()
        tab = pl.BlockSpec((CHUNK, RET_DK), lambda s, t: (chunk(t), 0))
        in_specs += [tab, tab]
        args += [cos, sin]
    in_specs += [
        pl.BlockSpec((None, RET_HEADS, CHUNK, CHUNK), lambda s, t: (direction(t), 0, 0, 0)),
        pl.BlockSpec((None, CHUNK, D_RET), lambda s, t: (direction(t), 0, 0)),
        pl.BlockSpec((None, 8, CHUNK), lambda s, t: (direction(t), 0, 0)),
        pl.BlockSpec((None, 8, CHUNK), lambda s, t: (direction(t), 0, 0)),
        pl.BlockSpec((1, D_RET), lambda s, t: (0, 0)),
    ]
    args += [dmat, rowdec, wend, cdec, gn_g.reshape(1, D_RET)]
    st_spec = pl.BlockSpec((None, None, RET_HEADS, RET_DK, RET_DV), lambda s, t: (s, direction(t), 0, 0, 0))
    if not zero_init:
        in_specs.append(st_spec)
        args.append(s0)
    out_specs = [pl.BlockSpec((CHUNK, D_RET), lambda s, t: (out_rows(s, t) - row_off, 0))]
    out_shape = [jax.ShapeDtypeStruct((n_seq * nc * CHUNK, D_RET), BF16)]
    aliases = {}
    if emit_state:
        out_specs.append(pl.BlockSpec((None, None, None, RET_HEADS, RET_DV, RET_DK),
                                      lambda s, t: (s, layer, direction(t), 0, 0, 0)))
        out_shape.append(jax.ShapeDtypeStruct((n_seq, DEPTH, 2, RET_HEADS, RET_DV, RET_DK), F32))
        if prev_states is not None:
            in_specs.append(pl.BlockSpec(memory_space=pl.ANY))
            args.append(prev_states)
            aliases = {len(args) - 1: 1}
    res = pl.pallas_call(
        functools.partial(_ret_kernel, nc, zero_init, emit_state, rope, prev_states is not None),
        grid=(n_seq, 2 * nc),
        in_specs=in_specs,
        out_specs=out_specs,
        out_shape=out_shape,
        scratch_shapes=[pltpu.VMEM((RET_HEADS, RET_DK, RET_DV), F32), pltpu.VMEM((nc * CHUNK, D_RET), F32)],
        input_output_aliases=aliases,
        compiler_params=_cparams(("arbitrary", "arbitrary")),
        name="ret_ctx" if zero_init else "ret_dec",
    )(*args)
    return res if emit_state else (res[0], None)


OUTPROJ_TM = 512


def _outproj_kernel(x_ref, ya_ref, yb_ref, yc_ref, w_ref, g_ref, o_ref):
    y = _dot(ya_ref[...], w_ref[:D_SSD, :])
    y = y + _dot(yb_ref[...], w_ref[D_SSD:D_SSD + D_S5, :])
    y = y + _dot(yc_ref[...], w_ref[D_SSD + D_S5:, :])
    o_ref[...] = x_ref[...] + g_ref[...] * y


def _outproj(x, y_ssd, y_s5, y_ret, w_out, mod3, layer):
    tm = OUTPROJ_TM
    return pl.pallas_call(
        _outproj_kernel,
        grid=(N_TOK // tm,),
        in_specs=[
            pl.BlockSpec((tm, D_MODEL), lambda i: (i, 0)),
            pl.BlockSpec((tm, D_SSD), lambda i: (i, 0)),
            pl.BlockSpec((tm, D_S5), lambda i: (i, 0)),
            pl.BlockSpec((tm, D_RET), lambda i: (i, 0)),
            pl.BlockSpec((None, D_MODEL, D_MODEL), lambda i: (layer, 0, 0)),
            _mod_spec(layer, 2, tm),
        ],
        out_specs=pl.BlockSpec((tm, D_MODEL), lambda i: (i, 0)),
        out_shape=jax.ShapeDtypeStruct((N_TOK, D_MODEL), F32),
        compiler_params=_cparams(("arbitrary",)),
        name="outproj",
    )(x, y_ssd, y_s5, y_ret, w_out, mod3)


PEER_SC_TM = 256
PEER_NCAND = PEER_TOPK + 1


def _cand_pairs():
    return [(i, j) for i in range(PEER_NCAND) for j in range(PEER_NCAND) if (i + 1) * (j + 1) <= PEER_NCAND]


def _sorting_network(n):
    pairs = []
    p = 1
    while p < n:
        k = p
        while k >= 1:
            for j in range(k % p, n - k, 2 * k):
                for i in range(min(k, n - j - k)):
                    if (i + j) // (2 * p) == (i + j + k) // (2 * p):
                        pairs.append((i + j, i + j + k))
            k //= 2
        p *= 2
    return pairs


def _top_rows_keys(s, n):
    nt = s.shape[0] // 8
    lists = [s[8 * k:8 * (k + 1), :] for k in range(nt)]
    for a, b in _sorting_network(nt):
        hi = jnp.maximum(lists[a], lists[b])
        lists[b] = jnp.minimum(lists[a], lists[b])
        lists[a] = hi
    lists.append(jnp.full_like(lists[0], NEG_INF))
    rows = []
    for r in range(n):
        m = jnp.max(lists[0], axis=0, keepdims=True)
        rows.append(m)
        if r < n - 1:
            taken = lists[0] >= m
            for k in range(min(n - 1 - r, nt)):
                lists[k] = jnp.where(taken, lists[k + 1], lists[k])
    return rows


def _peer_scores_kernel(x_ref, sh_ref, sc_ref, g_ref, wq_ref, keys_ref,
                        h_ref, thr_ref, g1_ref, s2_ref, e2_ref, cand_scr):
    tm = x_ref.shape[0]
    hb = _rms_modulate(x_ref[...], g_ref[...], sc_ref[...], sh_ref[...]).astype(BF16)
    h_ref[...] = hb
    q = _dot(hb, wq_ref[...]).astype(BF16)
    half = PEER_DQ // 2
    pairs = _cand_pairs()
    n_rows = cand_scr.shape[0]
    cand_scr[len(pairs):, :] = jnp.full((n_rows - len(pairs), tm), NEG_INF, F32)
    for h in range(PEER_HEADS):
        q1 = q[:, h * PEER_DQ:h * PEER_DQ + half]
        q2 = q[:, h * PEER_DQ + half:(h + 1) * PEER_DQ]
        s1 = _dot_nt(keys_ref[h, 0], q1)
        s2 = _dot_nt(keys_ref[h, 1], q2)
        a = _top_rows_keys(s1, PEER_NCAND)
        b = _top_rows_keys(s2, PEER_NCAND)
        for r, (i, j) in enumerate(pairs):
            cand_scr[r:r + 1, :] = a[i] + b[j]
        c = _top_rows_keys(cand_scr[...], PEER_NCAND)
        top = a[0] + b[0]
        zsum = jnp.zeros_like(top)
        for r in range(PEER_TOPK):
            zsum = zsum + jnp.exp(c[r] - top)
        tau = 0.5 * (c[PEER_TOPK - 1] + c[PEER_TOPK])
        thr_ref[h] = tau - s1
        g1_ref[h] = jnp.exp(s1 - a[0]) * (0.5 / zsum)
        s2_ref[h] = s2
        e2_ref[h] = jnp.exp(s2 - b[0])


def _peer_scores(x, mod3, layer, norm_g, wq, keys):
    tm = PEER_SC_TM
    n_tok = x.shape[0]
    sc_spec = pl.BlockSpec((PEER_HEADS, PEER_NKEYS, tm), lambda i: (0, 0, i))
    sc_shape = jax.ShapeDtypeStruct((PEER_HEADS, PEER_NKEYS, n_tok), F32)
    n_cand_rows = 8 * pl.next_power_of_2(-(-len(_cand_pairs()) // 8))
    return pl.pallas_call(
        _peer_scores_kernel,
        grid=(n_tok // tm,),
        in_specs=[
            pl.BlockSpec((tm, D_MODEL), lambda i: (i, 0)),
            _mod_spec(layer, 3, tm),
            _mod_spec(layer, 4, tm),
            pl.BlockSpec((1, D_MODEL), lambda i: (0, 0)),
            pl.BlockSpec((None, D_MODEL, PEER_HEADS * PEER_DQ), lambda i: (layer, 0, 0)),
            pl.BlockSpec((None, PEER_HEADS, 2, PEER_NKEYS, PEER_DQ // 2), lambda i: (layer, 0, 0, 0, 0)),
        ],
        out_specs=[pl.BlockSpec((tm, D_MODEL), lambda i: (i, 0)), sc_spec, sc_spec, sc_spec, sc_spec],
        out_shape=[jax.ShapeDtypeStruct((n_tok, D_MODEL), BF16), sc_shape, sc_shape, sc_shape, sc_shape],
        scratch_shapes=[pltpu.VMEM((n_cand_rows, tm), F32)],
        compiler_params=_cparams(("arbitrary",)),
        name="peer_scores",
    )(x, mod3, mod3, norm_g.reshape(1, D_MODEL), wq, keys)


PEER_TM = 512
PEER_EBLK = 1024
GELU_C = math.sqrt(2.0 / math.pi)

def _peer_experts_kernel(x_ref, g2_ref, h_ref, thr_ref, g1_ref, s2_ref, e2_ref, u_ref, vt_ref,
                         o_ref, acc_scr, act_scr, w_scr, gl_scr):
    tm = x_ref.shape[0]
    eblk = u_ref.shape[0]
    j = pl.program_id(1)

    @pl.when(j == 0)
    def _():
        acc_scr[...] = jnp.zeros_like(acc_scr)

    nb = eblk // PEER_NKEYS
    rsub = 16

    @pl.when(j < pl.num_programs(1))
    def _():
        act_scr[...] = _dot_nt(u_ref[...], h_ref[...])
        for lg in range(tm // 128):
            sl = slice(lg * 128, (lg + 1) * 128)
            for r0 in range(0, PEER_NKEYS, rsub):
                w = [jnp.zeros((rsub, 128), F32) for _ in range(nb)]
                for h in range(PEER_HEADS):
                    s2 = s2_ref[h, r0:r0 + rsub, sl]
                    e2 = e2_ref[h, r0:r0 + rsub, sl]
                    for a in range(nb):
                        w[a] = w[a] + jnp.where(s2 >= thr_ref[h, a:a + 1, sl], e2, 0.0) * g1_ref[h, a:a + 1, sl]
                for a in range(nb):
                    w_scr[a * PEER_NKEYS + r0:a * PEER_NKEYS + r0 + rsub, sl] = w[a]

    @pl.when(j + 1 < pl.num_programs(1) + 1)
    def _():
        act = act_scr[...]
        inner = act * (GELU_C + (GELU_C * 0.044715) * (act * act))
        gl_scr[...] = ((act * w_scr[...]) * (1.0 + jnp.tanh(inner))).astype(BF16)

    acc_scr[...] += lax.dot_general(vt_ref[...], gl_scr[...], (((0,), (0,)), ((), ())),
                                    preferred_element_type=F32)

    @pl.when(j == pl.num_programs(1) - 1)
    def _():
        o_ref[...] = x_ref[...] + g2_ref[...] * acc_scr[...].T


def _peer_experts(x, mod3, layer, hb, thr, g1, s2, e2, u_bf, vt_bf):
    tm, eblk = PEER_TM, PEER_EBLK
    n_tok = x.shape[0]
    once = pl.Buffered(1)
    sc_spec = pl.BlockSpec((PEER_HEADS, PEER_NKEYS, tm), lambda i, j: (0, 0, i))
    k1_spec = pl.BlockSpec((PEER_HEADS, eblk // PEER_NKEYS, tm), lambda i, j: (0, j, i))
    return pl.pallas_call(
        _peer_experts_kernel,
        grid=(n_tok // tm, PEER_EXPERTS // eblk),
        in_specs=[
            pl.BlockSpec((tm, D_MODEL), lambda i, j: (i, 0), pipeline_mode=once),
            _mod_spec(layer, 5, tm),
            pl.BlockSpec((tm, D_MODEL), lambda i, j: (i, 0)),
            k1_spec, k1_spec, sc_spec, sc_spec,
            pl.BlockSpec((None, eblk, D_MODEL), lambda i, j: (layer, j, 0)),
            pl.BlockSpec((None, eblk, D_MODEL), lambda i, j: (layer, j, 0)),
        ],
        out_specs=pl.BlockSpec((tm, D_MODEL), lambda i, j: (i, 0)),
        out_shape=jax.ShapeDtypeStruct((n_tok, D_MODEL), F32),
        scratch_shapes=[pltpu.VMEM((D_MODEL, tm), F32), pltpu.VMEM((eblk, tm), F32), pltpu.VMEM((eblk, tm), F32),
                        pltpu.VMEM((eblk, tm), BF16)],
        compiler_params=_cparams(("arbitrary", "arbitrary")),
        name="peer_experts",
    )(x, mod3, hb, thr, g1, s2, e2, u_bf, vt_bf)


FINAL_TM = 512


def _final_norm_kernel(n_ctx_tiles, x_ref, g_ref, oc_ref, od_ref):
    x = x_ref[...]
    var = jnp.mean(x * x, axis=-1, keepdims=True)
    y = x * lax.rsqrt(var + EPS) * g_ref[...]
    i = pl.program_id(0)

    @pl.when(i < n_ctx_tiles)
    def _():
        oc_ref[...] = y

    @pl.when(i >= n_ctx_tiles)
    def _():
        od_ref[...] = y


def _final_norm(x, g):
    tm = FINAL_TM
    n_ctx_tiles = N_CTX_TOK // tm
    return pl.pallas_call(
        functools.partial(_final_norm_kernel, n_ctx_tiles),
        grid=(N_TOK // tm,),
        in_specs=[pl.BlockSpec((tm, D_MODEL), lambda i: (i, 0)), pl.BlockSpec((1, D_MODEL), lambda i: (0, 0))],
        out_specs=[pl.BlockSpec((tm, D_MODEL), lambda i: (jnp.minimum(i, n_ctx_tiles - 1), 0)),
                   pl.BlockSpec((tm, D_MODEL), lambda i: (jnp.maximum(i - n_ctx_tiles, 0), 0))],
        out_shape=[jax.ShapeDtypeStruct((N_CTX_TOK, D_MODEL), F32),
                   jax.ShapeDtypeStruct((N_TOK - N_CTX_TOK, D_MODEL), F32)],
        compiler_params=_cparams(("arbitrary",)),
        name="final_norm",
    )(x, g.reshape(1, D_MODEL))


def _permute_w_in(w):
    cuts = np.cumsum([D_SSD, SSD_CONV_CH, 2 * SSD_HEADS, D_S5, D_RET, D_RET, D_RET])
    z, xbc, dt, u, rq, rk, rv, rg = jnp.split(w.astype(BF16), [int(c) for c in cuts], axis=2)
    pad = jnp.zeros((DEPTH, D_MODEL, CHUNK - SSD_HEADS), BF16)
    return jnp.concatenate([z, rq, rk, rv, rg, xbc, u, dt[..., :SSD_HEADS], pad, dt[..., SSD_HEADS:], pad], axis=2)


def _pad_lanes(a, width):
    return jnp.pad(a, [(0, 0)] * (a.ndim - 1) + [(0, width - a.shape[-1])])


def _ssd_params(conv_w, conv_b, dt_bias, a_log, d_skip, norm_g):
    return {
        "conv_w": jnp.pad(conv_w, ((0, 8 - SSD_CONV_K), (0, 0))),
        "conv_b": conv_b.reshape(1, SSD_CONV_CH),
        "dt_bias": _pad_lanes(dt_bias, CHUNK).reshape(2, 1, CHUNK),
        "a_exp": _pad_lanes(jnp.exp(a_log), CHUNK).reshape(2, 1, CHUNK),
        "d_skip": jnp.repeat(d_skip, SSD_P).reshape(1, D_SSD),
        "norm_g": norm_g.reshape(1, D_SSD),
    }


def _s5_params(layer, k_re, k_im, p_re, p_im, b_re, b_im, c_re, c_im, d_skip, glu_w, glu_b):
    kr = k_re.reshape(2, S5_GROUPS, S5_STATE, 1)
    ki = k_im.reshape(2, S5_GROUPS, S5_STATE, 1)
    bb_re = kr * b_re - ki * b_im
    bb_im = kr * b_im + ki * b_re
    eye = jnp.eye(S5_GROUPS, dtype=F32)

    def blockdiag_in(bb):
        return jnp.einsum("gh,dgnc->dgchn", eye, bb).reshape(2, D_S5, S5_LANES)

    def blockdiag_out(cc):
        return jnp.einsum("gh,dgcn->dgnhc", eye, cc).reshape(2, S5_LANES, D_S5)

    wb = jnp.concatenate([blockdiag_in(bb_re), blockdiag_in(bb_im)], axis=2).astype(BF16)
    wc = jnp.concatenate([blockdiag_out(c_re), blockdiag_out(-c_im)], axis=1).astype(BF16)
    return {"layer": layer, "wb": wb, "wc": wc, "p_re": p_re, "p_im": p_im,
            "d_skip": d_skip.reshape(1, D_S5), "glu_w": glu_w.astype(BF16), "glu_b": glu_b.reshape(1, D_S5)}


def _ssd_state_in(s):
    b = s.shape[0]
    s = s.reshape(b, 2, SSD_GROUPS, SSD_HPG, SSD_P, SSD_N)
    return jnp.transpose(s, (0, 1, 2, 5, 3, 4)).reshape(b, 2, SSD_GROUPS, SSD_N, SSD_HPG * SSD_P)


def kernel(x_prompt, x_sample, c, state_ssd, state_s5_re, state_s5_im, state_ret, c_ctx, ada_w, ada_b, norm1_g, norm2_g, w_in, w_out, ssd_conv_w, ssd_conv_b, ssd_dt_bias, ssd_a_log, ssd_d, ssd_norm_g, s5_lambda_re, s5_lambda_im, s5_log_dt, s5_b_re, s5_b_im, s5_c_re, s5_c_im, s5_d, s5_glu_w, s5_glu_b, ret_gn_g, peer_wq, peer_keys, peer_u, peer_v, final_norm_g):
    nc_ctx = CTX_LEN // CHUNK
    nc_dec = DEC_LEN // CHUNK
    dec_row_off = N_CTX_TOK // CHUNK

    cond = jnp.concatenate([c_ctx[None, :], c, jnp.zeros((N_COND - 1 - N_DEC_SEQ, D_MODEL), F32)], axis=0)
    mod3 = _adaln(cond, ada_w, ada_b).reshape(DEPTH * N_COND * N_MOD, 1, D_MODEL)
    p_re, p_im, k_re, k_im = _s5_prep(s5_lambda_re, s5_lambda_im, s5_log_dt)

    x = jnp.concatenate([x_prompt.reshape(N_CTX_TOK, D_MODEL), x_sample.reshape(N_TOK - N_CTX_TOK, D_MODEL)], axis=0)
    st_ssd = jnp.zeros((N_CTX_SEQ, DEPTH, 2, SSD_HEADS, SSD_P, SSD_N), F32)
    st_ret = jnp.zeros((N_CTX_SEQ, DEPTH, 2, RET_HEADS, RET_DV, RET_DK), F32)
    new_re, new_im = [], []
    w_in_b = _permute_w_in(w_in)
    w_out_b = w_out.astype(BF16)
    wq_b = peer_wq.astype(BF16)
    keys_b = peer_keys.astype(BF16)
    u_b = peer_u.astype(BF16)
    vt_b = peer_v.astype(BF16)
    for l in range(DEPTH):
        proj = _inproj(x, mod3, l, norm1_g[l], w_in_b)

        sp = _ssd_params(ssd_conv_w[l], ssd_conv_b[l], ssd_dt_bias[l], ssd_a_log[l], ssd_d[l], ssd_norm_g[l])
        y_ssd_c, st_ssd = _ssd(proj, sp, N_CTX_SEQ, nc_ctx, 0, None, layer=l, prev_states=st_ssd)
        y_ssd_d, _ = _ssd(proj, sp, N_DEC_SEQ, nc_dec, dec_row_off, _ssd_state_in(state_ssd[:, l]))

        s5p = _s5_params(l, k_re[2 * l:2 * l + 2], k_im[2 * l:2 * l + 2], p_re, p_im, s5_b_re[l], s5_b_im[l],
                         s5_c_re[l], s5_c_im[l], s5_d[l], s5_glu_w[l], s5_glu_b[l])
        y_s5_c, st_re, st_im = _s5(proj, s5p, N_CTX_SEQ, CTX_LEN // S5_ROWS, 0, None)
        x0 = (state_s5_re[:, l].reshape(N_DEC_SEQ, 2, 1, S5_LANES), state_s5_im[:, l].reshape(N_DEC_SEQ, 2, 1, S5_LANES))
        y_s5_d, _, _ = _s5(proj, s5p, N_DEC_SEQ, DEC_LEN // S5_ROWS, N_CTX_TOK // S5_ROWS, x0)

        y_ret_c, st_ret = _ret(proj, ret_gn_g[l], N_CTX_SEQ, nc_ctx, 0, None, rope=False, layer=l, prev_states=st_ret)
        y_ret_d, _ = _ret(proj, ret_gn_g[l], N_DEC_SEQ, nc_dec, dec_row_off,
                          jnp.swapaxes(state_ret[:, l], -1, -2), rope=True)

        x = _outproj(x, jnp.concatenate([y_ssd_c, y_ssd_d]), jnp.concatenate([y_s5_c, y_s5_d]),
                     jnp.concatenate([y_ret_c, y_ret_d]), w_out_b, mod3, l)

        hb, thr, g1, s2, e2 = _peer_scores(x, mod3, l, norm2_g[l], wq_b, keys_b)
        x = _peer_experts(x, mod3, l, hb, thr, g1, s2, e2, u_b, vt_b)

        new_re.append(st_re.reshape(N_CTX_SEQ, 2, S5_GROUPS, S5_STATE))
        new_im.append(st_im.reshape(N_CTX_SEQ, 2, S5_GROUPS, S5_STATE))

    y_ctx, y_dec = _final_norm(x, final_norm_g)
    y_prompt = y_ctx.reshape(N_CTX_SEQ, CTX_LEN, D_MODEL)
    y_sample = y_dec.reshape(N_DEC_SEQ, DEC_LEN, D_MODEL)
    return (y_prompt, y_sample, st_ssd, jnp.stack(new_re, axis=1), jnp.stack(new_im, axis=1), st_ret)
```

```python
import functools
import math

import jax
import jax.numpy as jnp
import numpy as np
from jax import lax
from jax.experimental import pallas as pl
from jax.experimental.pallas import tpu as pltpu

F32 = jnp.float32
BF16 = jnp.bfloat16

D_MODEL = 2048
N_CTX_SEQ = 16
CTX_LEN = 256
N_DEC_SEQ = 2
DEC_LEN = 2048
N_CTX_TOK = N_CTX_SEQ * CTX_LEN
N_TOK = N_CTX_TOK + N_DEC_SEQ * DEC_LEN
DEPTH = 2
GRID_W = 64
CHUNK = 128

SSD_HEADS = 12
SSD_P = 64
D_SSD = SSD_HEADS * SSD_P
SSD_N = 128
SSD_GROUPS = 2
SSD_HPG = SSD_HEADS // SSD_GROUPS
SSD_CONV_K = 5
SSD_CONV_CH = D_SSD + 2 * SSD_GROUPS * SSD_N
S5_CH = 16
S5_GROUPS = 32
D_S5 = S5_CH * S5_GROUPS
S5_STATE = 64
S5_LANES = S5_GROUPS * S5_STATE
S5_TILE = 8
S5_ROWS = 256
RET_HEADS = 6
RET_DK = 128
RET_DV = 128
D_RET = RET_HEADS * RET_DV
ROPE_BASE = 10000.0
PEER_HEADS = 8
PEER_DQ = 256
PEER_NKEYS = 128
PEER_EXPERTS = PEER_NKEYS * PEER_NKEYS
PEER_TOPK = 16
N_MOD = 6
N_COND = 8
EPS = 1e-6

COL_Z, COL_Q, COL_K, COL_V, COL_G = 0, 768, 1536, 2304, 3072
COL_XBC = 3840
COL_U = 5120
COL_DT = 5632
PROJ_W = COL_DT + 2 * CHUNK

VMEM_LIMIT = 56 * 1024 * 1024

NEG_INF = float("-inf")


def _cparams(sem, vmem_limit=VMEM_LIMIT):
    return pltpu.CompilerParams(dimension_semantics=sem, vmem_limit_bytes=vmem_limit)


def _split3(a):
    hi = a.astype(BF16)
    r1 = a - hi.astype(F32)
    mid = r1.astype(BF16)
    lo = (r1 - mid.astype(F32)).astype(BF16)
    return hi, mid, lo


def _dot(a, b):
    return jnp.dot(a, b, preferred_element_type=F32)


def _dot_split_lhs(a, b_exact):
    hi, mid, lo = _split3(a)
    return _dot(hi, b_exact) + _dot(mid, b_exact) + _dot(lo, b_exact)


def _dot_split_rhs(a_exact, b):
    hi, mid, lo = _split3(b)
    return _dot(a_exact, hi) + _dot(a_exact, mid) + _dot(a_exact, lo)


def _dot_nt(a, b):
    return lax.dot_general(a, b, (((1,), (1,)), ((), ())), preferred_element_type=F32)


def _silu(x):
    return x * jax.nn.sigmoid(x)


def _softplus(x):
    return jnp.maximum(x, 0.0) + jnp.log1p(jnp.exp(-jnp.abs(x)))


ADA_TN = 1536


def _adaln_kernel(c_ref, w_ref, b_ref, o_ref):
    s = _silu(c_ref[...]).astype(BF16)
    o_ref[...] = _dot(s, w_ref[...].astype(BF16)) + b_ref[...]


def _adaln(cond, ada_w, ada_b):
    n_out = N_MOD * D_MODEL
    return pl.pallas_call(
        _adaln_kernel,
        grid=(DEPTH, n_out // ADA_TN),
        in_specs=[
            pl.BlockSpec((N_COND, D_MODEL), lambda l, j: (0, 0)),
            pl.BlockSpec((None, D_MODEL, ADA_TN), lambda l, j: (l, 0, j)),
            pl.BlockSpec((None, 1, ADA_TN), lambda l, j: (l, 0, j)),
        ],
        out_specs=pl.BlockSpec((None, N_COND, ADA_TN), lambda l, j: (l, 0, j)),
        out_shape=jax.ShapeDtypeStruct((DEPTH, N_COND, n_out), F32),
        compiler_params=_cparams(("arbitrary", "arbitrary")),
        name="adaln",
    )(cond, ada_w, ada_b.reshape(DEPTH, 1, n_out))


def _mod_spec(layer, which, tm):
    n_ctx_tiles = N_CTX_TOK // tm
    tiles_per_dec = DEC_LEN // tm

    def index(i, *_):
        cond = jnp.where(i < n_ctx_tiles, 0, 1 + (i - n_ctx_tiles) // tiles_per_dec)
        return ((layer * N_COND + cond) * N_MOD + which, 0, 0)

    return pl.BlockSpec((None, 1, D_MODEL), index)


def _rms_modulate(x, g, sc, sh):
    var = jnp.mean(x * x, axis=-1, keepdims=True)
    y = x * lax.rsqrt(var + EPS) * g
    return y * (1.0 + sc) + sh


INPROJ_TM = 256


def _inproj_kernel(x_ref, sh_ref, sc_ref, g_ref, w_ref, o_ref):
    h = _rms_modulate(x_ref[...], g_ref[...], sc_ref[...], sh_ref[...]).astype(BF16)
    o_ref[...] = _dot(h, w_ref[...])


def _inproj(x, mod3, layer, norm_g, w_perm):
    tm = INPROJ_TM
    return pl.pallas_call(
        _inproj_kernel,
        grid=(N_TOK // tm,),
        in_specs=[
            pl.BlockSpec((tm, D_MODEL), lambda i: (i, 0)),
            _mod_spec(layer, 0, tm),
            _mod_spec(layer, 1, tm),
            pl.BlockSpec((1, D_MODEL), lambda i: (0, 0)),
            pl.BlockSpec((None, D_MODEL, PROJ_W), lambda i: (layer, 0, 0), pipeline_mode=pl.Buffered(1)),
        ],
        out_specs=pl.BlockSpec((tm, PROJ_W), lambda i: (i, 0)),
        out_shape=jax.ShapeDtypeStruct((N_TOK, PROJ_W), F32),
        compiler_params=_cparams(("arbitrary",)),
        name="inproj",
    )(x, mod3, mod3, norm_g.reshape(1, D_MODEL), w_perm)


def _walk(nc):
    def direction(t):
        return jnp.where(t < nc, 1, 0)

    def chunk(t):
        return jnp.where(t < nc, nc - 1 - t, t - nc)

    return direction, chunk


def _seq_specs(nc, row_off):
    direction, chunk = _walk(nc)

    def rows(s, t):
        return row_off + s * nc + chunk(t)

    def out_rows(s, t):
        return row_off + s * nc + jnp.where(t < nc, 0, t - nc)

    return direction, chunk, rows, out_rows


def _ssd_consts():
    idx = np.arange(CHUNK)
    tri = np.stack([(idx[None, :] <= idx[:, None]), (idx[None, :] >= idx[:, None])]).astype(np.float32)
    e_p = np.zeros((CHUNK, D_SSD), np.float32)
    e_n = np.zeros((CHUNK, SSD_HEADS * CHUNK), np.float32)
    for h in range(SSD_HEADS):
        e_p[h, h * SSD_P:(h + 1) * SSD_P] = 1.0
        e_n[h, h * CHUNK:(h + 1) * CHUNK] = 1.0
    lane = np.arange(CHUNK)
    pair = np.concatenate([np.broadcast_to(lane < SSD_P, (CHUNK, CHUNK)),
                           np.broadcast_to(lane >= SSD_P, (CHUNK, CHUNK))]).astype(np.float32)
    return tri, e_p, e_n, pair


def _ssd_kernel(nc, zero_init, emit_state, has_prev, *refs):
    (z_ref, xc_ref, xp_ref, xn_ref, dt_ref, cw_ref, cb_ref, dtb_ref, aexp_ref, dsk_ref, ng_ref,
     tri_ref, ep_ref, en_ref, pair_ref) = refs[:15]
    pos = 15
    s0_ref = None
    if not zero_init:
        s0_ref = refs[pos]
        pos += 1
    if has_prev:
        pos += 1
    y_ref = refs[pos]
    pos += 1
    st_ref = None
    if emit_state:
        st_ref = refs[pos]
        pos += 1
    state_scr, stash_scr = refs[pos:]

    t = pl.program_id(1)
    fwd = t >= nc
    step = jnp.where(fwd, t - nc, t)
    chunk = jnp.where(fwd, t - nc, nc - 1 - t)
    row0 = pl.multiple_of(chunk * CHUNK, CHUNK)

    @pl.when(step == 0)
    def _():
        if zero_init:
            state_scr[...] = jnp.zeros_like(state_scr)
        else:
            state_scr[...] = s0_ref[...]

    prev = jnp.where(chunk > 0, xp_ref[...], 0.0)
    nxt = jnp.where(chunk < nc - 1, xn_ref[...], 0.0)
    ext = jnp.concatenate([prev, xc_ref[...], nxt], axis=0)
    conv = cb_ref[...]
    for k in range(SSD_CONV_K):
        off = 8 + k - SSD_CONV_K // 2
        conv = conv + cw_ref[k:k + 1, :] * ext[off:off + CHUNK, :]
    xbc = _silu(conv)
    x = xbc[:, :D_SSD]
    bm = xbc[:, D_SSD:D_SSD + SSD_GROUPS * SSD_N]
    cm = xbc[:, D_SSD + SSD_GROUPS * SSD_N:]

    dt = _softplus(dt_ref[...] + dtb_ref[...])
    la = -dt * aexp_ref[...]
    tri = tri_ref[...]
    cum = _dot_split_rhs(tri.astype(BF16), la)
    ep = ep_ref[...]
    cum_p = _dot_split_lhs(cum, ep)
    dt_p = _dot_split_lhs(dt, ep)
    cum_col = _dot_split_lhs(cum, en_ref[...])
    cum_t = cum.T
    tot_p = jnp.where(fwd, cum_p[CHUNK - 1:CHUNK, :], cum_p[0:1, :])

    v = x * dt_p
    vb = v.astype(BF16)
    mask = tri > 0.5
    pair = pair_ref[...]
    y_parts = []
    for g in range(SSD_GROUPS):
        cg = cm[:, g * SSD_N:(g + 1) * SSD_N].astype(BF16)
        bg = bm[:, g * SSD_N:(g + 1) * SSD_N]
        gmat = _dot_nt(cg, bg.astype(BF16))
        for hp in range(SSD_HPG // 2):
            scs = []
            for h in (g * SSD_HPG + 2 * hp, g * SSD_HPG + 2 * hp + 1):
                ci = cum_col[:, h * CHUNK:(h + 1) * CHUNK]
                cj = cum_t[h:h + 1, :]
                dec = jnp.exp(jnp.where(mask, ci - cj, NEG_INF))
                scs.append((gmat * dec).astype(BF16))
            c0 = (g * SSD_HPG + 2 * hp) * SSD_P
            v2 = vb[:, c0:c0 + 2 * SSD_P]
            vv = jnp.concatenate([v2, v2], axis=0) * pair
            y_parts.append(_dot(jnp.concatenate(scs, axis=1), vv))
    y = jnp.concatenate(y_parts, axis=1)

    w_p = SSD_HPG * SSD_P
    y_off = jnp.concatenate(
        [_dot(cm[:, g * SSD_N:(g + 1) * SSD_N].astype(BF16), state_scr[g].astype(BF16))
         for g in range(SSD_GROUPS)], axis=1)
    y = y + y_off * jnp.exp(cum_p)

    vw = (v * jnp.exp(tot_p - cum_p)).astype(BF16)
    cdec = jnp.exp(tot_p)
    for g in range(SSD_GROUPS):
        bt = bm[:, g * SSD_N:(g + 1) * SSD_N].T.astype(BF16)
        state_scr[g] = state_scr[g] * cdec[:, g * w_p:(g + 1) * w_p] + _dot(bt, vw[:, g * w_p:(g + 1) * w_p])

    @pl.when(jnp.logical_not(fwd))
    def _():
        stash_scr[pl.ds(row0, CHUNK), :] = y

    @pl.when(fwd)
    def _():
        ytot = y + stash_scr[pl.ds(row0, CHUNK), :] + x * dsk_ref[...]
        gated = ytot * _silu(z_ref[...])
        var = jnp.mean(gated * gated, axis=-1, keepdims=True)
        y_ref[...] = (gated * lax.rsqrt(var + EPS) * ng_ref[...]).astype(y_ref.dtype)

    if emit_state:
        @pl.when(step == nc - 1)
        def _():
            for g in range(SSD_GROUPS):
                st_t = state_scr[g].T
                for k in range(SSD_HPG):
                    st_ref[g * SSD_HPG + k] = st_t[k * SSD_P:(k + 1) * SSD_P, :]


def _ssd(proj, p, n_seq, nc, row_off, s0, layer=0, prev_states=None):
    zero_init = s0 is None
    emit_state = zero_init
    direction, chunk, rows, out_rows = _seq_specs(nc, row_off)
    tri, e_p, e_n, pair = _ssd_consts()
    n8 = N_TOK // 8
    w_st = SSD_HPG * SSD_P
    in_specs = [
        pl.BlockSpec((CHUNK, D_SSD), lambda s, t: (rows(s, t), COL_Z // D_SSD)),
        pl.BlockSpec((CHUNK, SSD_CONV_CH), lambda s, t: (rows(s, t), COL_XBC // SSD_CONV_CH)),
        pl.BlockSpec((8, SSD_CONV_CH),
                     lambda s, t: (jnp.maximum(rows(s, t) * (CHUNK // 8) - 1, 0), COL_XBC // SSD_CONV_CH)),
        pl.BlockSpec((8, SSD_CONV_CH),
                     lambda s, t: (jnp.minimum((rows(s, t) + 1) * (CHUNK // 8), n8 - 1), COL_XBC // SSD_CONV_CH)),
        pl.BlockSpec((CHUNK, CHUNK), lambda s, t: (rows(s, t), COL_DT // CHUNK + direction(t))),
        pl.BlockSpec((8, SSD_CONV_CH), lambda s, t: (0, 0)),
        pl.BlockSpec((1, SSD_CONV_CH), lambda s, t: (0, 0)),
        pl.BlockSpec((None, 1, CHUNK), lambda s, t: (direction(t), 0, 0)),
        pl.BlockSpec((None, 1, CHUNK), lambda s, t: (direction(t), 0, 0)),
        pl.BlockSpec((1, D_SSD), lambda s, t: (0, 0)),
        pl.BlockSpec((1, D_SSD), lambda s, t: (0, 0)),
        pl.BlockSpec((None, CHUNK, CHUNK), lambda s, t: (direction(t), 0, 0)),
        pl.BlockSpec((CHUNK, D_SSD), lambda s, t: (0, 0)),
        pl.BlockSpec((CHUNK, SSD_HEADS * CHUNK), lambda s, t: (0, 0)),
        pl.BlockSpec((2 * CHUNK, CHUNK), lambda s, t: (0, 0)),
    ]
    args = [proj, proj, proj, proj, proj, p["conv_w"], p["conv_b"], p["dt_bias"], p["a_exp"], p["d_skip"],
            p["norm_g"], jnp.asarray(tri), jnp.asarray(e_p, BF16), jnp.asarray(e_n, BF16), jnp.asarray(pair, BF16)]
    st_spec = pl.BlockSpec((None, None, SSD_GROUPS, SSD_N, w_st), lambda s, t: (s, direction(t), 0, 0, 0))
    if not zero_init:
        in_specs.append(st_spec)
        args.append(s0)
    out_specs = [pl.BlockSpec((CHUNK, D_SSD), lambda s, t: (out_rows(s, t) - row_off, 0))]
    out_shape = [jax.ShapeDtypeStruct((n_seq * nc * CHUNK, D_SSD), BF16)]
    aliases = {}
    if emit_state:
        out_specs.append(pl.BlockSpec((None, None, None, SSD_HEADS, SSD_P, SSD_N),
                                      lambda s, t: (s, layer, direction(t), 0, 0, 0)))
        out_shape.append(jax.ShapeDtypeStruct((n_seq, DEPTH, 2, SSD_HEADS, SSD_P, SSD_N), F32))
        if prev_states is not None:
            in_specs.append(pl.BlockSpec(memory_space=pl.ANY))
            args.append(prev_states)
            aliases = {len(args) - 1: 1}
    res = pl.pallas_call(
        functools.partial(_ssd_kernel, nc, zero_init, emit_state, prev_states is not None),
        grid=(n_seq, 2 * nc),
        in_specs=in_specs,
        out_specs=out_specs,
        out_shape=out_shape,
        scratch_shapes=[pltpu.VMEM((SSD_GROUPS, SSD_N, w_st), F32), pltpu.VMEM((nc * CHUNK, D_SSD), F32)],
        input_output_aliases=aliases,
        compiler_params=_cparams(("arbitrary", "arbitrary")),
        name="ssd_ctx" if zero_init else "ssd_dec",
    )(*args)
    return res if emit_state else (res[0], None)


def _s5_prep_kernel(lr_ref, li_ref, ldt_ref, pre_ref, pim_ref, kre_ref, kim_ref):
    lr = lr_ref[...]
    li = li_ref[...]
    dt = jnp.exp(ldt_ref[...])
    kk = (lax.broadcasted_iota(jnp.int32, (S5_TILE, S5_LANES), 0) + 1).astype(F32)
    mag = jnp.exp(kk * (lr * dt))
    ang = kk * (li * dt)
    p_re = mag * jnp.cos(ang)
    p_im = mag * jnp.sin(ang)
    pre_ref[...] = p_re
    pim_ref[...] = p_im
    a_re = p_re[0:1, :]
    a_im = p_im[0:1, :]
    den = lr * lr + li * li
    num_re = a_re - 1.0
    kre_ref[...] = (num_re * lr + a_im * li) / den
    kim_ref[...] = (a_im * lr - num_re * li) / den


def _s5_prep(lam_re, lam_im, log_dt):
    n = DEPTH * 2
    lr = lam_re.reshape(n, 1, S5_LANES)
    li = lam_im.reshape(n, 1, S5_LANES)
    ldt = jnp.broadcast_to(log_dt[..., None], (DEPTH, 2, S5_GROUPS, S5_STATE)).reshape(n, 1, S5_LANES)
    row = pl.BlockSpec((None, 1, S5_LANES), lambda i: (i, 0, 0))
    tab = pl.BlockSpec((None, S5_TILE, S5_LANES), lambda i: (i, 0, 0))
    return pl.pallas_call(
        _s5_prep_kernel,
        grid=(n,),
        in_specs=[row, row, row],
        out_specs=[tab, tab, row, row],
        out_shape=[jax.ShapeDtypeStruct((n, S5_TILE, S5_LANES), F32)] * 2
        + [jax.ShapeDtypeStruct((n, 1, S5_LANES), F32)] * 2,
        compiler_params=_cparams(("arbitrary",)),
        name="s5_prep",
    )(lr, li, ldt)


def _s5_kernel(nc, zero_init, emit_state, *refs):
    (u_ref, jm_ref, wb_ref, pre_ref, pim_ref, wc_ref, dsk_ref, gw_ref, gb_ref) = refs[:9]
    pos = 9
    x0re_ref = x0im_ref = None
    if not zero_init:
        x0re_ref, x0im_ref = refs[pos:pos + 2]
        pos += 2
    y_ref = refs[pos]
    pos += 1
    sre_ref = sim_ref = None
    if emit_state:
        sre_ref, sim_ref = refs[pos:pos + 2]
        pos += 2
    cre_scr, cim_scr, stash_scr = refs[pos:]

    t = pl.program_id(1)
    fwd = t >= nc
    step = jnp.where(fwd, t - nc, t)
    chunk = jnp.where(fwd, t - nc, nc - 1 - t)
    row0 = pl.multiple_of(chunk * S5_ROWS, S5_ROWS)

    @pl.when(step == 0)
    def _():
        if zero_init:
            cre_scr[...] = jnp.zeros_like(cre_scr)
            cim_scr[...] = jnp.zeros_like(cim_scr)
        else:
            cre_scr[...] = x0re_ref[...]
            cim_scr[...] = x0im_ref[...]

    u = u_ref[...]
    jm = jm_ref[...]
    us = _dot(jm, u.astype(BF16)).astype(BF16)
    bu = _dot(us, wb_ref[...])
    xr = bu[:, :S5_LANES]
    xi = bu[:, S5_LANES:]
    cr = cre_scr[...]
    ci = cim_scr[...]
    pr = pre_ref[...]
    pi = pim_ref[...]
    sub = lax.broadcasted_iota(jnp.int32, (S5_TILE, S5_LANES), 0)
    steps = []
    d = 1
    while d < S5_TILE:
        keep = sub >= d
        steps.append((d, jnp.where(keep, pre_ref[d - 1:d, :], 0.0), jnp.where(keep, pim_ref[d - 1:d, :], 0.0)))
        d *= 2
    tiles_r, tiles_i = [], []
    for b in range(S5_ROWS // S5_TILE):
        br = xr[b * S5_TILE:(b + 1) * S5_TILE, :]
        bi = xi[b * S5_TILE:(b + 1) * S5_TILE, :]
        for d, ar, ai in steps:
            sr = pltpu.roll(br, d, axis=0)
            si = pltpu.roll(bi, d, axis=0)
            br, bi = br + (ar * sr - ai * si), bi + (ar * si + ai * sr)
        br, bi = br + (pr * cr - pi * ci), bi + (pr * ci + pi * cr)
        cr = br[S5_TILE - 1:S5_TILE, :]
        ci = bi[S5_TILE - 1:S5_TILE, :]
        tiles_r.append(br)
        tiles_i.append(bi)
    xr = jnp.concatenate(tiles_r, axis=0)
    xi = jnp.concatenate(tiles_i, axis=0)
    cre_scr[...] = cr
    cim_scr[...] = ci

    ys = _dot(xr.astype(BF16), wc_ref[:S5_LANES, :]) + _dot(xi.astype(BF16), wc_ref[S5_LANES:, :])
    yh = ys.astype(BF16)
    yl = (ys - yh.astype(F32)).astype(BF16)
    y = _dot(jm, yh) + _dot(jm, yl)

    @pl.when(jnp.logical_not(fwd))
    def _():
        stash_scr[pl.ds(row0, S5_ROWS), :] = y

    @pl.when(fwd)
    def _():
        yt = y + stash_scr[pl.ds(row0, S5_ROWS), :] + u * dsk_ref[...]
        yt = jax.nn.gelu(yt)
        gate = jax.nn.sigmoid(_dot(yt.astype(BF16), gw_ref[...]) + gb_ref[...])
        y_ref[...] = (yt * gate).astype(y_ref.dtype)

    if emit_state:
        @pl.when(step == nc - 1)
        def _():
            sre_ref[...] = cre_scr[...]
            sim_ref[...] = cim_scr[...]


def _s5(proj, p, n_seq, nc, row_off, x0):
    zero_init = x0 is None
    emit_state = zero_init
    direction, chunk, rows, out_rows = _seq_specs(nc, row_off)
    eye = np.eye(S5_ROWS, dtype=np.float32)
    jm = jnp.asarray(np.stack([eye, eye[::-1]]), BF16)
    lyr = p["layer"]

    def dsel(t):
        return lyr * 2 + direction(t)

    in_specs = [
        pl.BlockSpec((S5_ROWS, D_S5), lambda s, t: (rows(s, t), COL_U // D_S5)),
        pl.BlockSpec((None, S5_ROWS, S5_ROWS), lambda s, t: (direction(t), 0, 0)),
        pl.BlockSpec((None, D_S5, 2 * S5_LANES), lambda s, t: (direction(t), 0, 0)),
        pl.BlockSpec((None, S5_TILE, S5_LANES), lambda s, t: (dsel(t), 0, 0)),
        pl.BlockSpec((None, S5_TILE, S5_LANES), lambda s, t: (dsel(t), 0, 0)),
        pl.BlockSpec((None, 2 * S5_LANES, D_S5), lambda s, t: (direction(t), 0, 0)),
        pl.BlockSpec((1, D_S5), lambda s, t: (0, 0)),
        pl.BlockSpec((D_S5, D_S5), lambda s, t: (0, 0)),
        pl.BlockSpec((1, D_S5), lambda s, t: (0, 0)),
    ]
    args = [proj, jm, p["wb"], p["p_re"], p["p_im"], p["wc"], p["d_skip"], p["glu_w"], p["glu_b"]]
    st_spec = pl.BlockSpec((None, None, 1, S5_LANES), lambda s, t: (s, direction(t), 0, 0))
    if not zero_init:
        in_specs += [st_spec, st_spec]
        args += [x0[0], x0[1]]
    out_specs = [pl.BlockSpec((S5_ROWS, D_S5), lambda s, t: (out_rows(s, t) - row_off, 0))]
    out_shape = [jax.ShapeDtypeStruct((n_seq * nc * S5_ROWS, D_S5), BF16)]
    if emit_state:
        out_specs += [st_spec, st_spec]
        out_shape += [jax.ShapeDtypeStruct((n_seq, 2, 1, S5_LANES), F32)] * 2
    res = pl.pallas_call(
        functools.partial(_s5_kernel, nc, zero_init, emit_state),
        grid=(n_seq, 2 * nc),
        in_specs=in_specs,
        out_specs=out_specs,
        out_shape=out_shape,
        scratch_shapes=[pltpu.VMEM((1, S5_LANES), F32), pltpu.VMEM((1, S5_LANES), F32),
                        pltpu.VMEM((nc * S5_ROWS, D_S5), F32)],
        compiler_params=_cparams(("arbitrary", "arbitrary")),
        name="s5_ctx" if zero_init else "s5_dec",
    )(*args)
    return res if emit_state else (res[0], None, None)


def _ret_consts():
    heads = np.arange(RET_HEADS, dtype=np.float64)
    lg = np.stack([np.log1p(-np.exp2(-5.0 - heads)), np.log1p(-np.exp2(-5.5 - heads))])
    i = np.arange(CHUNK, dtype=np.float64)
    diff = i[:, None] - i[None, :]
    dmat = np.zeros((2, RET_HEADS, CHUNK, CHUNK))
    rowdec = np.zeros((2, CHUNK, D_RET))
    wend = np.zeros((2, 8, CHUNK))
    cdec = np.zeros((2, 8, CHUNK))
    for h in range(RET_HEADS):
        dmat[0, h] = np.where(diff >= 0, np.exp(lg[0, h] * diff), 0.0)
        dmat[1, h] = np.where(diff <= 0, np.exp(-lg[1, h] * diff), 0.0)
        rowdec[0, :, h * RET_DV:(h + 1) * RET_DV] = np.exp(lg[0, h] * (i + 1))[:, None]
        rowdec[1, :, h * RET_DV:(h + 1) * RET_DV] = np.exp(lg[1, h] * (CHUNK - i))[:, None]
        wend[0, h] = np.exp(lg[0, h] * (CHUNK - 1 - i))
        wend[1, h] = np.exp(lg[1, h] * i)
        cdec[:, h] = np.exp(lg[:, h] * CHUNK)[:, None]
    return [jnp.asarray(a, F32) for a in (dmat, rowdec, wend, cdec)]


def _rope_tables():
    t = np.arange(DEC_LEN)
    row = (t // GRID_W).astype(np.float32)
    col = (t % GRID_W).astype(np.float32)
    quarter = RET_DK // 4
    freqs = (ROPE_BASE ** (-np.arange(quarter, dtype=np.float32) / quarter)).astype(np.float32)
    ar = (row[:, None] * freqs[None, :]).astype(np.float64)
    ac = (col[:, None] * freqs[None, :]).astype(np.float64)
    cos = np.concatenate([np.cos(ar), np.cos(ar), np.cos(ac), np.cos(ac)], axis=1)
    sin = np.concatenate([-np.sin(ar), np.sin(ar), -np.sin(ac), np.sin(ac)], axis=1)
    return jnp.asarray(cos, F32), jnp.asarray(sin, F32)


def _ret_kernel(nc, zero_init, emit_state, rope, has_prev, *refs):
    q_ref, k_ref, v_ref, g_ref = refs[:4]
    pos = 4
    cos_ref = sin_ref = None
    if rope:
        cos_ref, sin_ref = refs[pos:pos + 2]
        pos += 2
    dmat_ref, rowdec_ref, wend_ref, cdec_ref, gn_ref = refs[pos:pos + 5]
    pos += 5
    s0_ref = None
    if not zero_init:
        s0_ref = refs[pos]
        pos += 1
    if has_prev:
        pos += 1
    y_ref = refs[pos]
    pos += 1
    st_ref = None
    if emit_state:
        st_ref = refs[pos]
        pos += 1
    state_scr, stash_scr = refs[pos:]

    t = pl.program_id(1)
    fwd = t >= nc
    step = jnp.where(fwd, t - nc, t)
    chunk = jnp.where(fwd, t - nc, nc - 1 - t)
    row0 = pl.multiple_of(chunk * CHUNK, CHUNK)

    @pl.when(step == 0)
    def _():
        if zero_init:
            state_scr[...] = jnp.zeros_like(state_scr)
        else:
            state_scr[...] = s0_ref[...]

    q = q_ref[...]
    k = k_ref[...]
    if rope:
        cos = jnp.concatenate([cos_ref[...]] * RET_HEADS, axis=1)
        sin = jnp.concatenate([sin_ref[...]] * RET_HEADS, axis=1)
        lane = lax.broadcasted_iota(jnp.int32, (CHUNK, D_RET), 1)
        first = (lane // (RET_DK // 4)) % 2 == 0

        def rot(x):
            partner = jnp.where(first, pltpu.roll(x, D_RET - RET_DK // 4, axis=1), pltpu.roll(x, RET_DK // 4, axis=1))
            return x * cos + partner * sin

        q = rot(q)
        k = rot(k)
    k = k * (RET_DK ** -0.5)
    qb = q.astype(BF16)
    vb = v_ref[...].astype(BF16)
    rowdec = rowdec_ref[...]
    y_parts = []
    for h in range(RET_HEADS):
        sl = slice(h * RET_DK, (h + 1) * RET_DK)
        kh = k[:, sl]
        sc = _dot_nt(qb[:, sl], kh.astype(BF16)) * dmat_ref[h]
        yh = _dot(sc.astype(BF16), vb[:, sl])
        yh = yh + _dot(qb[:, sl], state_scr[h].astype(BF16)) * rowdec[:, sl]
        y_parts.append(yh)
        kt = (kh.T * wend_ref[h:h + 1, :]).astype(BF16)
        state_scr[h] = state_scr[h] * cdec_ref[h:h + 1, :] + _dot(kt, vb[:, sl])
    y = jnp.concatenate(y_parts, axis=1)

    @pl.when(jnp.logical_not(fwd))
    def _():
        stash_scr[pl.ds(row0, CHUNK), :] = y

    @pl.when(fwd)
    def _():
        yt = y + stash_scr[pl.ds(row0, CHUNK), :]
        outs = []
        for h in range(RET_HEADS):
            yh = yt[:, h * RET_DV:(h + 1) * RET_DV]
            yc = yh - jnp.mean(yh, axis=-1, keepdims=True)
            outs.append(yc * lax.rsqrt(jnp.mean(yc * yc, axis=-1, keepdims=True) + EPS))
        yn = jnp.concatenate(outs, axis=1)
        y_ref[...] = (yn * gn_ref[...] * _silu(g_ref[...])).astype(y_ref.dtype)

    if emit_state:
        @pl.when(step == nc - 1)
        def _():
            for h in range(RET_HEADS):
                st_ref[h] = state_scr[h].T


def _ret(proj, gn_g, n_seq, nc, row_off, s0, rope, layer=0, prev_states=None):
    zero_init = s0 is None
    emit_state = zero_init
    direction, chunk, rows, out_rows = _seq_specs(nc, row_off)
    dmat, rowdec, wend, cdec = _ret_consts()

    def col(c):
        return pl.BlockSpec((CHUNK, D_RET), lambda s, t: (rows(s, t), c // D_RET))

    in_specs = [col(COL_Q), col(COL_K), col(COL_V), col(COL_G)]
    args = [proj, proj, proj, proj]
    if rope:
        cos, sin = _rope_tables()
        tab = pl.BlockSpec((CHUNK, RET_DK), lambda s, t: (chunk(t), 0))
        in_specs += [tab, tab]
        args += [cos, sin]
    in_specs += [
        pl.BlockSpec((None, RET_HEADS, CHUNK, CHUNK), lambda s, t: (direction(t), 0, 0, 0)),
        pl.BlockSpec((None, CHUNK, D_RET), lambda s, t: (direction(t), 0, 0)),
        pl.BlockSpec((None, 8, CHUNK), lambda s, t: (direction(t), 0, 0)),
        pl.BlockSpec((None, 8, CHUNK), lambda s, t: (direction(t), 0, 0)),
        pl.BlockSpec((1, D_RET), lambda s, t: (0, 0)),
    ]
    args += [dmat, rowdec, wend, cdec, gn_g.reshape(1, D_RET)]
    st_spec = pl.BlockSpec((None, None, RET_HEADS, RET_DK, RET_DV), lambda s, t: (s, direction(t), 0, 0, 0))
    if not zero_init:
        in_specs.append(st_spec)
        args.append(s0)
    out_specs = [pl.BlockSpec((CHUNK, D_RET), lambda s, t: (out_rows(s, t) - row_off, 0))]
    out_shape = [jax.ShapeDtypeStruct((n_seq * nc * CHUNK, D_RET), BF16)]
    aliases = {}
    if emit_state:
        out_specs.append(pl.BlockSpec((None, None, None, RET_HEADS, RET_DV, RET_DK),
                                      lambda s, t: (s, layer, direction(t), 0, 0, 0)))
        out_shape.append(jax.ShapeDtypeStruct((n_seq, DEPTH, 2, RET_HEADS, RET_DV, RET_DK), F32))
        if prev_states is not None:
            in_specs.append(pl.BlockSpec(memory_space=pl.ANY))
            args.append(prev_states)
            aliases = {len(args) - 1: 1}
    res = pl.pallas_call(
        functools.partial(_ret_kernel, nc, zero_init, emit_state, rope, prev_states is not None),
        grid=(n_seq, 2 * nc),
        in_specs=in_specs,
        out_specs=out_specs,
        out_shape=out_shape,
        scratch_shapes=[pltpu.VMEM((RET_HEADS, RET_DK, RET_DV), F32), pltpu.VMEM((nc * CHUNK, D_RET), F32)],
        input_output_aliases=aliases,
        compiler_params=_cparams(("arbitrary", "arbitrary")),
        name="ret_ctx" if zero_init else "ret_dec",
    )(*args)
    return res if emit_state else (res[0], None)


OUTPROJ_TM = 512


def _outproj_kernel(x_ref, ya_ref, yb_ref, yc_ref, w_ref, g_ref, o_ref):
    y = _dot(ya_ref[...], w_ref[:D_SSD, :])
    y = y + _dot(yb_ref[...], w_ref[D_SSD:D_SSD + D_S5, :])
    y = y + _dot(yc_ref[...], w_ref[D_SSD + D_S5:, :])
    o_ref[...] = x_ref[...] + g_ref[...] * y


def _outproj(x, y_ssd, y_s5, y_ret, w_out, mod3, layer):
    tm = OUTPROJ_TM
    return pl.pallas_call(
        _outproj_kernel,
        grid=(N_TOK // tm,),
        in_specs=[
            pl.BlockSpec((tm, D_MODEL), lambda i: (i, 0)),
            pl.BlockSpec((tm, D_SSD), lambda i: (i, 0)),
            pl.BlockSpec((tm, D_S5), lambda i: (i, 0)),
            pl.BlockSpec((tm, D_RET), lambda i: (i, 0)),
            pl.BlockSpec((None, D_MODEL, D_MODEL), lambda i: (layer, 0, 0)),
            _mod_spec(layer, 2, tm),
        ],
        out_specs=pl.BlockSpec((tm, D_MODEL), lambda i: (i, 0)),
        out_shape=jax.ShapeDtypeStruct((N_TOK, D_MODEL), F32),
        compiler_params=_cparams(("arbitrary",)),
        name="outproj",
    )(x, y_ssd, y_s5, y_ret, w_out, mod3)


PEER_SC_TM = 256
PEER_NCAND = PEER_TOPK + 1


def _cand_pairs():
    return [(i, j) for i in range(PEER_NCAND) for j in range(PEER_NCAND) if (i + 1) * (j + 1) <= PEER_NCAND]


def _sorting_network(n):
    pairs = []
    p = 1
    while p < n:
        k = p
        while k >= 1:
            for j in range(k % p, n - k, 2 * k):
                for i in range(min(k, n - j - k)):
                    if (i + j) // (2 * p) == (i + j + k) // (2 * p):
                        pairs.append((i + j, i + j + k))
            k //= 2
        p *= 2
    return pairs


def _top_rows_keys(s, n):
    nt = s.shape[0] // 8
    lists = [s[8 * k:8 * (k + 1), :] for k in range(nt)]
    for a, b in _sorting_network(nt):
        hi = jnp.maximum(lists[a], lists[b])
        lists[b] = jnp.minimum(lists[a], lists[b])
        lists[a] = hi
    lists.append(jnp.full_like(lists[0], NEG_INF))
    rows = []
    for r in range(n):
        m = jnp.max(lists[0], axis=0, keepdims=True)
        rows.append(m)
        if r < n - 1:
            taken = lists[0] >= m
            for k in range(min(n - 1 - r, nt)):
                lists[k] = jnp.where(taken, lists[k + 1], lists[k])
    return rows


def _peer_scores_kernel(x_ref, sh_ref, sc_ref, g_ref, wq_ref, keys_ref,
                        h_ref, thr_ref, g1_ref, s2_ref, e2_ref, cand_scr):
    tm = x_ref.shape[0]
    hb = _rms_modulate(x_ref[...], g_ref[...], sc_ref[...], sh_ref[...]).astype(BF16)
    h_ref[...] = hb
    q = _dot(hb, wq_ref[...]).astype(BF16)
    half = PEER_DQ // 2
    pairs = _cand_pairs()
    n_rows = cand_scr.shape[0]
    cand_scr[len(pairs):, :] = jnp.full((n_rows - len(pairs), tm), NEG_INF, F32)
    for h in range(PEER_HEADS):
        q1 = q[:, h * PEER_DQ:h * PEER_DQ + half]
        q2 = q[:, h * PEER_DQ + half:(h + 1) * PEER_DQ]
        s1 = _dot_nt(keys_ref[h, 0], q1)
        s2 = _dot_nt(keys_ref[h, 1], q2)
        a = _top_rows_keys(s1, PEER_NCAND)
        b = _top_rows_keys(s2, PEER_NCAND)
        for r, (i, j) in enumerate(pairs):
            cand_scr[r:r + 1, :] = a[i] + b[j]
        c = _top_rows_keys(cand_scr[...], PEER_NCAND)
        top = a[0] + b[0]
        zsum = jnp.zeros_like(top)
        for r in range(PEER_TOPK):
            zsum = zsum + jnp.exp(c[r] - top)
        tau = 0.5 * (c[PEER_TOPK - 1] + c[PEER_TOPK])
        thr_ref[h] = tau - s1
        g1_ref[h] = jnp.exp(s1 - a[0]) * (0.5 / zsum)
        s2_ref[h] = s2
        e2_ref[h] = jnp.exp(s2 - b[0])


def _peer_scores(x, mod3, layer, norm_g, wq, keys):
    tm = PEER_SC_TM
    n_tok = x.shape[0]
    sc_spec = pl.BlockSpec((PEER_HEADS, PEER_NKEYS, tm), lambda i: (0, 0, i))
    sc_shape = jax.ShapeDtypeStruct((PEER_HEADS, PEER_NKEYS, n_tok), F32)
    n_cand_rows = 8 * pl.next_power_of_2(-(-len(_cand_pairs()) // 8))
    return pl.pallas_call(
        _peer_scores_kernel,
        grid=(n_tok // tm,),
        in_specs=[
            pl.BlockSpec((tm, D_MODEL), lambda i: (i, 0)),
            _mod_spec(layer, 3, tm),
            _mod_spec(layer, 4, tm),
            pl.BlockSpec((1, D_MODEL), lambda i: (0, 0)),
            pl.BlockSpec((None, D_MODEL, PEER_HEADS * PEER_DQ), lambda i: (layer, 0, 0)),
            pl.BlockSpec((None, PEER_HEADS, 2, PEER_NKEYS, PEER_DQ // 2), lambda i: (layer, 0, 0, 0, 0)),
        ],
        out_specs=[pl.BlockSpec((tm, D_MODEL), lambda i: (i, 0)), sc_spec, sc_spec, sc_spec, sc_spec],
        out_shape=[jax.ShapeDtypeStruct((n_tok, D_MODEL), BF16), sc_shape, sc_shape, sc_shape, sc_shape],
        scratch_shapes=[pltpu.VMEM((n_cand_rows, tm), F32)],
        compiler_params=_cparams(("arbitrary",)),
        name="peer_scores",
    )(x, mod3, mod3, norm_g.reshape(1, D_MODEL), wq, keys)


PEER_TM = 512
PEER_EBLK = 1024
GELU_C = math.sqrt(2.0 / math.pi)

def _peer_experts_kernel(n_ctx_tiles, x_ref, g2_ref, h_ref, thr_ref, g1_ref, s2_ref, e2_ref, u_ref, vt_ref, *rest):
    if n_ctx_tiles is None:
        o_ref, acc_scr, act_scr, w_scr, gl_scr = rest
    else:
        fg_ref, oc_ref, od_ref, acc_scr, act_scr, w_scr, gl_scr = rest
    tm = x_ref.shape[0]
    eblk = u_ref.shape[0]
    j = pl.program_id(1)

    @pl.when(j == 0)
    def _():
        acc_scr[...] = jnp.zeros_like(acc_scr)

    nb = eblk // PEER_NKEYS
    rsub = 16

    @pl.when(j < pl.num_programs(1))
    def _():
        act_scr[...] = _dot_nt(u_ref[...], h_ref[...])
        for lg in range(tm // 128):
            sl = slice(lg * 128, (lg + 1) * 128)
            for r0 in range(0, PEER_NKEYS, rsub):
                w = [jnp.zeros((rsub, 128), F32) for _ in range(nb)]
                for h in range(PEER_HEADS):
                    s2 = s2_ref[h, r0:r0 + rsub, sl]
                    e2 = e2_ref[h, r0:r0 + rsub, sl]
                    for a in range(nb):
                        w[a] = w[a] + jnp.where(s2 >= thr_ref[h, a:a + 1, sl], e2, 0.0) * g1_ref[h, a:a + 1, sl]
                for a in range(nb):
                    w_scr[a * PEER_NKEYS + r0:a * PEER_NKEYS + r0 + rsub, sl] = w[a]

    @pl.when(j + 1 < pl.num_programs(1) + 1)
    def _():
        act = act_scr[...]
        inner = act * (GELU_C + (GELU_C * 0.044715) * (act * act))
        gl_scr[...] = ((act * w_scr[...]) * (1.0 + jnp.tanh(inner))).astype(BF16)

    acc_scr[...] += lax.dot_general(vt_ref[...], gl_scr[...], (((0,), (0,)), ((), ())),
                                    preferred_element_type=F32)

    @pl.when(j == pl.num_programs(1) - 1)
    def _():
        y = x_ref[...] + g2_ref[...] * acc_scr[...].T
        if n_ctx_tiles is None:
            o_ref[...] = y
        else:
            var = jnp.mean(y * y, axis=-1, keepdims=True)
            yn = y * lax.rsqrt(var + EPS) * fg_ref[...]
            i = pl.program_id(0)

            @pl.when(i < n_ctx_tiles)
            def _():
                oc_ref[...] = yn

            @pl.when(i >= n_ctx_tiles)
            def _():
                od_ref[...] = yn


def _peer_experts(x, mod3, layer, hb, thr, g1, s2, e2, u_bf, vt_bf, final_g=None):
    tm, eblk = PEER_TM, PEER_EBLK
    n_tok = x.shape[0]
    n_ctx_tiles = None if final_g is None else N_CTX_TOK // tm
    once = pl.Buffered(1)
    sc_spec = pl.BlockSpec((PEER_HEADS, PEER_NKEYS, tm), lambda i, j: (0, 0, i),
                           pipeline_mode=None if final_g is None else once)
    k1_spec = pl.BlockSpec((PEER_HEADS, eblk // PEER_NKEYS, tm), lambda i, j: (0, j, i))
    in_specs = [
        pl.BlockSpec((tm, D_MODEL), lambda i, j: (i, 0), pipeline_mode=once),
        _mod_spec(layer, 5, tm),
        pl.BlockSpec((tm, D_MODEL), lambda i, j: (i, 0)),
        k1_spec, k1_spec, sc_spec, sc_spec,
        pl.BlockSpec((None, eblk, D_MODEL), lambda i, j: (layer, j, 0)),
        pl.BlockSpec((None, eblk, D_MODEL), lambda i, j: (layer, j, 0)),
    ]
    args = [x, mod3, hb, thr, g1, s2, e2, u_bf, vt_bf]
    if final_g is None:
        out_specs = pl.BlockSpec((tm, D_MODEL), lambda i, j: (i, 0))
        out_shape = jax.ShapeDtypeStruct((n_tok, D_MODEL), F32)
    else:
        in_specs.append(pl.BlockSpec((1, D_MODEL), lambda i, j: (0, 0)))
        args.append(final_g.reshape(1, D_MODEL))
        out_specs = [pl.BlockSpec((tm, D_MODEL), lambda i, j: (jnp.minimum(i, n_ctx_tiles - 1), 0), pipeline_mode=once),
                     pl.BlockSpec((tm, D_MODEL), lambda i, j: (jnp.maximum(i - n_ctx_tiles, 0), 0), pipeline_mode=once)]
        out_shape = [jax.ShapeDtypeStruct((N_CTX_TOK, D_MODEL), F32),
                     jax.ShapeDtypeStruct((n_tok - N_CTX_TOK, D_MODEL), F32)]
    return pl.pallas_call(
        functools.partial(_peer_experts_kernel, n_ctx_tiles),
        grid=(n_tok // tm, PEER_EXPERTS // eblk),
        in_specs=in_specs,
        out_specs=out_specs,
        out_shape=out_shape,
        scratch_shapes=[pltpu.VMEM((D_MODEL, tm), F32), pltpu.VMEM((eblk, tm), F32), pltpu.VMEM((eblk, tm), F32),
                        pltpu.VMEM((eblk, tm), BF16)],
        compiler_params=_cparams(("arbitrary", "arbitrary")),
        name="peer_experts",
    )(*args)


FINAL_TM = 512


def _final_norm_kernel(n_ctx_tiles, x_ref, g_ref, oc_ref, od_ref):
    x = x_ref[...]
    var = jnp.mean(x * x, axis=-1, keepdims=True)
    y = x * lax.rsqrt(var + EPS) * g_ref[...]
    i = pl.program_id(0)

    @pl.when(i < n_ctx_tiles)
    def _():
        oc_ref[...] = y

    @pl.when(i >= n_ctx_tiles)
    def _():
        od_ref[...] = y


def _final_norm(x, g):
    tm = FINAL_TM
    n_ctx_tiles = N_CTX_TOK // tm
    return pl.pallas_call(
        functools.partial(_final_norm_kernel, n_ctx_tiles),
        grid=(N_TOK // tm,),
        in_specs=[pl.BlockSpec((tm, D_MODEL), lambda i: (i, 0)), pl.BlockSpec((1, D_MODEL), lambda i: (0, 0))],
        out_specs=[pl.BlockSpec((tm, D_MODEL), lambda i: (jnp.minimum(i, n_ctx_tiles - 1), 0)),
                   pl.BlockSpec((tm, D_MODEL), lambda i: (jnp.maximum(i - n_ctx_tiles, 0), 0))],
        out_shape=[jax.ShapeDtypeStruct((N_CTX_TOK, D_MODEL), F32),
                   jax.ShapeDtypeStruct((N_TOK - N_CTX_TOK, D_MODEL), F32)],
        compiler_params=_cparams(("arbitrary",)),
        name="final_norm",
    )(x, g.reshape(1, D_MODEL))


def _permute_w_in(w):
    cuts = np.cumsum([D_SSD, SSD_CONV_CH, 2 * SSD_HEADS, D_S5, D_RET, D_RET, D_RET])
    z, xbc, dt, u, rq, rk, rv, rg = jnp.split(w.astype(BF16), [int(c) for c in cuts], axis=2)
    pad = jnp.zeros((DEPTH, D_MODEL, CHUNK - SSD_HEADS), BF16)
    return jnp.concatenate([z, rq, rk, rv, rg, xbc, u, dt[..., :SSD_HEADS], pad, dt[..., SSD_HEADS:], pad], axis=2)


def _pad_lanes(a, width):
    return jnp.pad(a, [(0, 0)] * (a.ndim - 1) + [(0, width - a.shape[-1])])


def _ssd_params(conv_w, conv_b, dt_bias, a_log, d_skip, norm_g):
    return {
        "conv_w": jnp.pad(conv_w, ((0, 8 - SSD_CONV_K), (0, 0))),
        "conv_b": conv_b.reshape(1, SSD_CONV_CH),
        "dt_bias": _pad_lanes(dt_bias, CHUNK).reshape(2, 1, CHUNK),
        "a_exp": _pad_lanes(jnp.exp(a_log), CHUNK).reshape(2, 1, CHUNK),
        "d_skip": jnp.repeat(d_skip, SSD_P).reshape(1, D_SSD),
        "norm_g": norm_g.reshape(1, D_SSD),
    }


def _s5_params(layer, k_re, k_im, p_re, p_im, b_re, b_im, c_re, c_im, d_skip, glu_w, glu_b):
    kr = k_re.reshape(2, S5_GROUPS, S5_STATE, 1)
    ki = k_im.reshape(2, S5_GROUPS, S5_STATE, 1)
    bb_re = kr * b_re - ki * b_im
    bb_im = kr * b_im + ki * b_re
    eye = jnp.eye(S5_GROUPS, dtype=F32)

    def blockdiag_in(bb):
        return jnp.einsum("gh,dgnc->dgchn", eye, bb).reshape(2, D_S5, S5_LANES)

    def blockdiag_out(cc):
        return jnp.einsum("gh,dgcn->dgnhc", eye, cc).reshape(2, S5_LANES, D_S5)

    wb = jnp.concatenate([blockdiag_in(bb_re), blockdiag_in(bb_im)], axis=2).astype(BF16)
    wc = jnp.concatenate([blockdiag_out(c_re), blockdiag_out(-c_im)], axis=1).astype(BF16)
    return {"layer": layer, "wb": wb, "wc": wc, "p_re": p_re, "p_im": p_im,
            "d_skip": d_skip.reshape(1, D_S5), "glu_w": glu_w.astype(BF16), "glu_b": glu_b.reshape(1, D_S5)}


def _ssd_state_in(s):
    b = s.shape[0]
    s = s.reshape(b, 2, SSD_GROUPS, SSD_HPG, SSD_P, SSD_N)
    return jnp.transpose(s, (0, 1, 2, 5, 3, 4)).reshape(b, 2, SSD_GROUPS, SSD_N, SSD_HPG * SSD_P)


def kernel(x_prompt, x_sample, c, state_ssd, state_s5_re, state_s5_im, state_ret, c_ctx, ada_w, ada_b, norm1_g, norm2_g, w_in, w_out, ssd_conv_w, ssd_conv_b, ssd_dt_bias, ssd_a_log, ssd_d, ssd_norm_g, s5_lambda_re, s5_lambda_im, s5_log_dt, s5_b_re, s5_b_im, s5_c_re, s5_c_im, s5_d, s5_glu_w, s5_glu_b, ret_gn_g, peer_wq, peer_keys, peer_u, peer_v, final_norm_g):
    nc_ctx = CTX_LEN // CHUNK
    nc_dec = DEC_LEN // CHUNK
    dec_row_off = N_CTX_TOK // CHUNK

    cond = jnp.concatenate([c_ctx[None, :], c, jnp.zeros((N_COND - 1 - N_DEC_SEQ, D_MODEL), F32)], axis=0)
    mod3 = _adaln(cond, ada_w, ada_b).reshape(DEPTH * N_COND * N_MOD, 1, D_MODEL)
    p_re, p_im, k_re, k_im = _s5_prep(s5_lambda_re, s5_lambda_im, s5_log_dt)

    x = jnp.concatenate([x_prompt.reshape(N_CTX_TOK, D_MODEL), x_sample.reshape(N_TOK - N_CTX_TOK, D_MODEL)], axis=0)
    st_ssd = jnp.zeros((N_CTX_SEQ, DEPTH, 2, SSD_HEADS, SSD_P, SSD_N), F32)
    st_ret = jnp.zeros((N_CTX_SEQ, DEPTH, 2, RET_HEADS, RET_DV, RET_DK), F32)
    new_re, new_im = [], []
    w_in_b = _permute_w_in(w_in)
    w_out_b = w_out.astype(BF16)
    wq_b = peer_wq.astype(BF16)
    keys_b = peer_keys.astype(BF16)
    u_b = peer_u.astype(BF16)
    vt_b = peer_v.astype(BF16)
    for l in range(DEPTH):
        proj = _inproj(x, mod3, l, norm1_g[l], w_in_b)

        sp = _ssd_params(ssd_conv_w[l], ssd_conv_b[l], ssd_dt_bias[l], ssd_a_log[l], ssd_d[l], ssd_norm_g[l])
        y_ssd_c, st_ssd = _ssd(proj, sp, N_CTX_SEQ, nc_ctx, 0, None, layer=l, prev_states=st_ssd)
        y_ssd_d, _ = _ssd(proj, sp, N_DEC_SEQ, nc_dec, dec_row_off, _ssd_state_in(state_ssd[:, l]))

        s5p = _s5_params(l, k_re[2 * l:2 * l + 2], k_im[2 * l:2 * l + 2], p_re, p_im, s5_b_re[l], s5_b_im[l],
                         s5_c_re[l], s5_c_im[l], s5_d[l], s5_glu_w[l], s5_glu_b[l])
        y_s5_c, st_re, st_im = _s5(proj, s5p, N_CTX_SEQ, CTX_LEN // S5_ROWS, 0, None)
        x0 = (state_s5_re[:, l].reshape(N_DEC_SEQ, 2, 1, S5_LANES), state_s5_im[:, l].reshape(N_DEC_SEQ, 2, 1, S5_LANES))
        y_s5_d, _, _ = _s5(proj, s5p, N_DEC_SEQ, DEC_LEN // S5_ROWS, N_CTX_TOK // S5_ROWS, x0)

        y_ret_c, st_ret = _ret(proj, ret_gn_g[l], N_CTX_SEQ, nc_ctx, 0, None, rope=False, layer=l, prev_states=st_ret)
        y_ret_d, _ = _ret(proj, ret_gn_g[l], N_DEC_SEQ, nc_dec, dec_row_off,
                          jnp.swapaxes(state_ret[:, l], -1, -2), rope=True)

        x = _outproj(x, jnp.concatenate([y_ssd_c, y_ssd_d]), jnp.concatenate([y_s5_c, y_s5_d]),
                     jnp.concatenate([y_ret_c, y_ret_d]), w_out_b, mod3, l)

        hb, thr, g1, s2, e2 = _peer_scores(x, mod3, l, norm2_g[l], wq_b, keys_b)
        if l < DEPTH - 1:
            x = _peer_experts(x, mod3, l, hb, thr, g1, s2, e2, u_b, vt_b)
        else:
            y_ctx, y_dec = _peer_experts(x, mod3, l, hb, thr, g1, s2, e2, u_b, vt_b, final_g=final_norm_g)

        new_re.append(st_re.reshape(N_CTX_SEQ, 2, S5_GROUPS, S5_STATE))
        new_im.append(st_im.reshape(N_CTX_SEQ, 2, S5_GROUPS, S5_STATE))

    y_prompt = y_ctx.reshape(N_CTX_SEQ, CTX_LEN, D_MODEL)
    y_sample = y_dec.reshape(N_DEC_SEQ, DEC_LEN, D_MODEL)
    return (y_prompt, y_sample, st_ssd, jnp.stack(new_re, axis=1), jnp.stack(new_im, axis=1), st_ret)
```
